```python
import jax
import jax.numpy as jnp
from jax import lax
import numpy as np

D_MODEL = 1024
BATCH = 2
SEQ = 16384
DEPTH = 2

N_MIXERS = 2
PLE_DIM = 256
D_FF = 2816
NORM_EPS = 1e-6
ROPE_THETA = 10000.0

NSA_HEADS = 16
NSA_KV_GROUPS = 4
NSA_HEAD_DIM = D_MODEL // NSA_HEADS
CMP_BLOCK = 32
CMP_STRIDE = 16
SEL_BLOCK = 64
N_SELECT = 16
WINDOW = 512
Q_BLOCK = 128
NSA_Q_COLS = NSA_HEADS * NSA_HEAD_DIM
NSA_KV_COLS = NSA_KV_GROUPS * NSA_HEAD_DIM
NSA_IN_COLS = NSA_Q_COLS + 6 * NSA_KV_COLS + 3 * NSA_HEADS

HGRN_EXPAND = 128
HGRN_HEADS = D_MODEL // HGRN_EXPAND
HGRN_DK = HGRN_EXPAND
HGRN_DV = D_MODEL // HGRN_HEADS
HGRN_CHUNK = 64
HGRN_IN_COLS = 2 * HGRN_HEADS * HGRN_DK + 2 * HGRN_HEADS * HGRN_DV

kernel_name = "hybrid_nsa_hgrn2_macaron_block"


def rms_norm(x, gain):
    xf = x.astype(jnp.float32)
    y = xf * lax.rsqrt(jnp.mean(xf * xf, axis=-1, keepdims=True) + NORM_EPS)
    return (y * gain.astype(jnp.float32)).astype(x.dtype)


def rope(x, pos):
    half = x.shape[-1] // 2
    inv = ROPE_THETA ** (-jnp.arange(half, dtype=jnp.float32) / half)
    ang = pos.astype(jnp.float32)[:, None] * inv[None, :]
    cos = jnp.cos(ang)[None, :, None, :]
    sin = jnp.sin(ang)[None, :, None, :]
    x1 = x[..., :half].astype(jnp.float32)
    x2 = x[..., half:].astype(jnp.float32)
    return jnp.concatenate([x1 * cos - x2 * sin, x2 * cos + x1 * sin], axis=-1).astype(x.dtype)


def masked_softmax(s, mask):
    s = jnp.where(mask, s.astype(jnp.float32), -jnp.inf)
    m = jnp.max(s, axis=-1, keepdims=True)
    m = jnp.where(jnp.isfinite(m), m, 0.0)
    e = jnp.exp(s - m)
    return e / jnp.maximum(jnp.sum(e, axis=-1, keepdims=True), 1e-30)


def swiglu(x, w_in, w_out):
    g, u = jnp.split(x @ w_in, 2, axis=-1)
    return (jax.nn.silu(g) * u) @ w_out


def compress_blocks(kv, pos_enc, w1, w2):
    b, s, g, d = kv.shape
    ratio = CMP_BLOCK // CMP_STRIDE
    n_cmp = s // CMP_STRIDE - ratio + 1
    c = kv.reshape(b, s // CMP_STRIDE, CMP_STRIDE, g, d)
    blocks = jnp.concatenate([c[:, r:r + n_cmp] for r in range(ratio)], axis=2)
    blocks = blocks + pos_enc[:, None, :].astype(kv.dtype)
    flat = blocks.transpose(0, 1, 3, 2, 4).reshape(b, n_cmp, g, CMP_BLOCK * d)
    return jax.nn.silu(flat @ w1) @ w2


def selection_importance(p_cmp, n_sel):
    r = SEL_BLOCK // CMP_STRIDE
    lo = -(CMP_BLOCK // CMP_STRIDE - 1)
    hi = r - 1
    n_cmp = p_cmp.shape[-1]
    pad = [(0, 0)] * (p_cmp.ndim - 1) + [(-lo, r * n_sel + hi + 1 - n_cmp)]
    pp = jnp.pad(p_cmp, pad)
    return sum(pp[..., o - lo:o - lo + r * n_sel:r] for o in range(lo, hi + 1))


def nsa_mixer(h, w_in, w_out, cmp_pos, cmp_w1, cmp_w2):
    b, s, _ = h.shape
    H, G, D = NSA_HEADS, NSA_KV_GROUPS, NSA_HEAD_DIM
    HG = H // G
    splits = [NSA_Q_COLS + i * NSA_KV_COLS for i in range(7)]
    q, k_c, v_c, k_s, v_s, k_w, v_w, gates = jnp.split(h @ w_in, splits, axis=-1)
    pos = jnp.arange(s)
    q = rope(q.reshape(b, s, H, D), pos) * (D ** -0.5)
    k_c, k_s, k_w = [rope(t.reshape(b, s, G, D), pos) for t in (k_c, k_s, k_w)]
    v_c, v_s, v_w = [t.reshape(b, s, G, D) for t in (v_c, v_s, v_w)]
    gates = jax.nn.sigmoid(gates.astype(jnp.float32)).reshape(b, s, H, 3)

    k_cmp = compress_blocks(k_c, cmp_pos[0], cmp_w1[0], cmp_w2[0])
    v_cmp = compress_blocks(v_c, cmp_pos[1], cmp_w1[1], cmp_w2[1])
    n_cmp = k_cmp.shape[1]
    cmp_end = jnp.arange(n_cmp) * CMP_STRIDE + CMP_BLOCK - 1

    n_sel = s // SEL_BLOCK
    top_n = min(N_SELECT, n_sel)
    k_sel = k_s.reshape(b, n_sel, SEL_BLOCK, G, D).transpose(0, 3, 1, 2, 4)
    v_sel = v_s.reshape(b, n_sel, SEL_BLOCK, G, D).transpose(0, 3, 1, 2, 4)
    blk = jnp.arange(n_sel)
    bi = jnp.arange(b)[:, None, None, None]
    gi = jnp.arange(G)[None, :, None, None]

    k_win = jnp.pad(k_w, ((0, 0), (WINDOW, 0), (0, 0), (0, 0)))
    v_win = jnp.pad(v_w, ((0, 0), (WINDOW, 0), (0, 0), (0, 0)))

    def query_block(j):
        start = j * Q_BLOCK
        t = start + jnp.arange(Q_BLOCK)
        qb = lax.dynamic_slice_in_dim(q, start, Q_BLOCK, axis=1).reshape(b, Q_BLOCK, G, HG, D)
        gb = lax.dynamic_slice_in_dim(gates, start, Q_BLOCK, axis=1).reshape(b, Q_BLOCK, G, HG, 3)

        s_c = jnp.einsum('bqghd,bkgd->bghqk', qb, k_cmp)
        p_c = masked_softmax(s_c, cmp_end[None, :] <= t[:, None])
        o_c = jnp.einsum('bghqk,bkgd->bqghd', p_c.astype(v_cmp.dtype), v_cmp)

        imp = selection_importance(jnp.sum(p_c, axis=2), n_sel)
        cur = t // SEL_BLOCK
        forced = (blk[None, :] == 0) | (blk[None, :] == cur[:, None]) | (blk[None, :] == cur[:, None] - 1)
        valid = blk[None, :] * SEL_BLOCK <= t[:, None]
        imp = jnp.where(forced, jnp.inf, jnp.where(valid, imp, -jnp.inf))
        _, idx = lax.top_k(imp, top_n)
        k_g = k_sel[bi, gi, idx]
        v_g = v_sel[bi, gi, idx]
        s_s = jnp.einsum('bqghd,bgqnkd->bghqnk', qb, k_g).reshape(b, G, HG, Q_BLOCK, top_n * SEL_BLOCK)
        tok = (idx[..., None] * SEL_BLOCK + jnp.arange(SEL_BLOCK)).reshape(b, G, Q_BLOCK, top_n * SEL_BLOCK)
        p_s = masked_softmax(s_s, tok[:, :, None] <= t[:, None])
        p_s = p_s.reshape(b, G, HG, Q_BLOCK, top_n, SEL_BLOCK)
        o_s = jnp.einsum('bghqnk,bgqnkd->bqghd', p_s.astype(v_g.dtype), v_g)

        kw = lax.dynamic_slice_in_dim(k_win, start, Q_BLOCK + WINDOW, axis=1)
        vw = lax.dynamic_slice_in_dim(v_win, start, Q_BLOCK + WINDOW, axis=1)
        kpos = start - WINDOW + jnp.arange(Q_BLOCK + WINDOW)
        mask_w = (kpos[None, :] <= t[:, None]) & (kpos[None, :] > t[:, None] - WINDOW) & (kpos[None, :] >= 0)
        s_w = jnp.einsum('bqghd,bkgd->bghqk', qb, kw)
        p_w = masked_softmax(s_w, mask_w)
        o_w = jnp.einsum('bghqk,bkgd->bqghd', p_w.astype(vw.dtype), vw)

        o = gb[..., 0:1] * o_c + gb[..., 1:2] * o_s + gb[..., 2:3] * o_w
        return o.reshape(b, Q_BLOCK, H * D).astype(h.dtype)

    out = lax.map(query_block, jnp.arange(s // Q_BLOCK))
    out = out.transpose(1, 0, 2, 3).reshape(b, s, H * D)
    return out @ w_out


def hgrn2_mixer(h, w_in, w_out, norm_gain, lower_bound):
    b, s, _ = h.shape
    H, DK, DV, C = HGRN_HEADS, HGRN_DK, HGRN_DV, HGRN_CHUNK
    q, f, i, g = jnp.split(h @ w_in, [H * DK, 2 * H * DK, 2 * H * DK + H * DV], axis=-1)
    q = jax.nn.silu(q.astype(jnp.float32))
    lb = lower_bound.astype(jnp.float32)
    log_f = jnp.logaddexp(jnp.log(lb), jnp.log1p(-lb) + jax.nn.log_sigmoid(f.astype(jnp.float32)))
    k = -jnp.expm1(log_f)

    def to_chunks(t, d):
        return t.reshape(b, s // C, C, H, d).transpose(1, 0, 3, 2, 4)

    xs = (to_chunks(q, DK), to_chunks(k, DK), to_chunks(i.astype(jnp.float32), DV), to_chunks(log_f, DK))
    causal = jnp.tril(jnp.ones((C, C), dtype=bool))[None, None, :, :, None]

    def chunk_step(state, inp):
        q_c, k_c, v_c, lf_c = inp
        cum = jnp.cumsum(lf_c, axis=2)
        diff = cum[:, :, :, None, :] - cum[:, :, None, :, :]
        decay = jnp.exp(jnp.where(causal, diff, -jnp.inf))
        att = jnp.einsum('bhtd,bhsd,bhtsd->bhts', q_c, k_c, decay)
        o = jnp.einsum('bhts,bhse->bhte', att, v_c) + jnp.einsum('bhtd,bhde->bhte', q_c * jnp.exp(cum), state)
        last = cum[:, :, -1]
        state = jnp.exp(last)[..., None] * state + jnp.einsum(
            'bhsd,bhse->bhde', k_c * jnp.exp(last[:, :, None, :] - cum), v_c)
        return state, o

    state0 = jnp.zeros((b, H, DK, DV), jnp.float32)
    _, o = lax.scan(chunk_step, state0, xs)
    o = o.transpose(1, 0, 3, 2, 4).reshape(b, s, H, DV)
    o = rms_norm(o, norm_gain) * jax.nn.silu(g.astype(jnp.float32)).reshape(b, s, H, DV)
    return o.reshape(b, s, H * DV).astype(h.dtype) @ w_out


def setup_inputs(seed: int = 0) -> dict:
    key = jax.random.key(seed)
    ks = jax.random.split(key, 16)
    n_nsa = (DEPTH + 1) // 2
    n_hgrn = DEPTH // 2
    f32 = jnp.float32

    def w(k, shape, fan_in):
        return jax.random.normal(k, shape, f32) * (fan_in ** -0.5)

    return {
        "x": jax.random.normal(ks[0], (BATCH, SEQ, D_MODEL), f32),
        "p": jax.random.normal(ks[1], (DEPTH, BATCH, SEQ, PLE_DIM), f32),
        "norm_gains": 1.0 + 0.05 * jax.random.normal(ks[2], (DEPTH, 8, D_MODEL), f32),
        "ffn_w_in": w(ks[3], (DEPTH, 2, D_MODEL, 2 * D_FF), D_MODEL),
        "ffn_w_out": w(ks[4], (DEPTH, 2, D_FF, D_MODEL), D_FF),
        "ple_w_in": w(ks[5], (DEPTH, PLE_DIM, D_MODEL), PLE_DIM),
        "ple_w_gate": w(ks[6], (DEPTH, D_MODEL, D_MODEL), D_MODEL),
        "nsa_w_in": w(ks[7], (n_nsa, D_MODEL, NSA_IN_COLS), D_MODEL),
        "nsa_w_out": w(ks[8], (n_nsa, NSA_Q_COLS, D_MODEL), NSA_Q_COLS),
        "nsa_cmp_pos": 0.5 * jax.random.normal(ks[9], (n_nsa, 2, CMP_BLOCK, NSA_HEAD_DIM), f32),
        "nsa_cmp_w1": w(ks[10], (n_nsa, 2, CMP_BLOCK * NSA_HEAD_DIM, NSA_HEAD_DIM), CMP_BLOCK * NSA_HEAD_DIM),
        "nsa_cmp_w2": w(ks[11], (n_nsa, 2, NSA_HEAD_DIM, NSA_HEAD_DIM), NSA_HEAD_DIM),
        "hgrn_w_in": w(ks[12], (n_hgrn, D_MODEL, HGRN_IN_COLS), D_MODEL),
        "hgrn_w_out": w(ks[13], (n_hgrn, HGRN_HEADS * HGRN_DV, D_MODEL), HGRN_HEADS * HGRN_DV),
        "hgrn_norm": 1.0 + 0.05 * jax.random.normal(ks[14], (n_hgrn, HGRN_DV), f32),
        "hgrn_lb_logits": jax.random.normal(ks[15], (DEPTH, HGRN_HEADS * HGRN_DK), f32),
    }


def reference(x, p, norm_gains, ffn_w_in, ffn_w_out, ple_w_in, ple_w_gate,
              nsa_w_in, nsa_w_out, nsa_cmp_pos, nsa_cmp_w1, nsa_cmp_w2,
              hgrn_w_in, hgrn_w_out, hgrn_norm, hgrn_lb_logits):
    lb_sm = jax.nn.softmax(hgrn_lb_logits.astype(jnp.float32), axis=0)
    lower_bounds = jnp.cumsum(lb_sm, axis=0) - lb_sm[0]
    for layer in range(DEPTH):
        ng = norm_gains[layer]
        x = x + 0.5 * rms_norm(swiglu(rms_norm(x, ng[0]), ffn_w_in[layer, 0], ffn_w_out[layer, 0]), ng[1])
        hn = rms_norm(x, ng[2])
        j = layer // N_MIXERS
        if layer % N_MIXERS == 0:
            y = nsa_mixer(hn, nsa_w_in[j], nsa_w_out[j], nsa_cmp_pos[j], nsa_cmp_w1[j], nsa_cmp_w2[j])
        else:
            y = hgrn2_mixer(hn, hgrn_w_in[j], hgrn_w_out[j], hgrn_norm[j], lower_bounds[layer])
        x = x + rms_norm(y, ng[3])
        x = x + 0.5 * rms_norm(swiglu(rms_norm(x, ng[4]), ffn_w_in[layer, 1], ffn_w_out[layer, 1]), ng[5])
        gate = jax.nn.sigmoid(rms_norm(x, ng[6]) @ ple_w_gate[layer])
        x = x + rms_norm((p[layer] @ ple_w_in[layer]) * gate, ng[7])
    return x
```

```python
import functools

import jax
import jax.numpy as jnp
from jax import lax
from jax.experimental import pallas as pl
from jax.experimental.pallas import tpu as pltpu

F32 = jnp.float32
BF16 = jnp.bfloat16
NEG_INF = float("-inf")

D_MODEL = 1024
NORM_EPS = 1e-6
ROPE_THETA = 10000.0

NSA_HEADS = 16
NSA_GROUPS = 4
NSA_HPG = NSA_HEADS // NSA_GROUPS
NSA_HD = 64
NSA_KV_COLS = NSA_GROUPS * NSA_HD
CMP_BLOCK = 32
CMP_STRIDE = 16
SEL_BLOCK = 64
N_SELECT = 16
WINDOW = 512
Q_BLOCK = 128

HGRN_HEADS = 8
HGRN_DK = 128
HGRN_SUB = 16

LANES = 128
VMEM_LIMIT = 56 * 1024 * 1024


def _params(*sem):
    return pltpu.CompilerParams(dimension_semantics=sem, vmem_limit_bytes=VMEM_LIMIT)


def _rms(x, gain):
    return x * lax.rsqrt(jnp.mean(x * x, axis=-1, keepdims=True) + NORM_EPS) * gain


def _dot(a, b):
    return jnp.dot(a, b, preferred_element_type=F32)


def _dot_nt(a, b):
    return lax.dot_general(a, b, (((1,), (1,)), ((), ())), preferred_element_type=F32)


def _dot_tn(a, b):
    return lax.dot_general(a, b, (((0,), (0,)), ((), ())), preferred_element_type=F32)


def _dot_split(x, w):
    hi = x.astype(BF16)
    lo = (x - hi.astype(F32)).astype(BF16)
    return _dot(hi, w) + _dot(lo, w)


def _softmax_rows(s):
    m = jnp.max(s, axis=-1, keepdims=True)
    m = jnp.where(m == NEG_INF, 0.0, m)
    e = jnp.exp(s - m)
    return e / jnp.maximum(jnp.sum(e, axis=-1, keepdims=True), 1e-30)


def _ffn_kernel(x_ref, g0_ref, g1_ref, wg_ref, wu_ref, wo_ref, o_ref, xn_ref, acc_ref):
    j = pl.program_id(1)

    @pl.when(j == 0)
    def _():
        xn_ref[...] = _rms(x_ref[...], g0_ref[...]).astype(BF16)
        acc_ref[...] = jnp.zeros_like(acc_ref)

    xn = xn_ref[...]
    g = _dot(xn, wg_ref[...])
    u = _dot(xn, wu_ref[...])
    h = (g * jax.nn.sigmoid(g) * u).astype(BF16)
    acc_ref[...] += _dot(h, wo_ref[...])

    @pl.when(j == pl.num_programs(1) - 1)
    def _():
        o_ref[...] = x_ref[...] + 0.5 * _rms(acc_ref[...], g1_ref[...])


def _ffn_block(x, g_pre, g_post, w_in, w_out, tm=1024, tf=256):
    t, d = x.shape
    dff = w_out.shape[0]
    tm = min(tm, t)
    nf = dff // tf
    row = pl.BlockSpec((tm, d), lambda i, j: (i, 0))
    vec = pl.BlockSpec((1, d), lambda i, j: (0, 0))
    return pl.pallas_call(
        _ffn_kernel,
        grid=(t // tm, nf),
        in_specs=[row, vec, vec,
                  pl.BlockSpec((d, tf), lambda i, j: (0, j)),
                  pl.BlockSpec((d, tf), lambda i, j: (0, j + nf)),
                  pl.BlockSpec((tf, d), lambda i, j: (j, 0))],
        out_specs=row,
        out_shape=jax.ShapeDtypeStruct((t, d), F32),
        scratch_shapes=[pltpu.VMEM((tm, d), BF16), pltpu.VMEM((tm, d), F32)],
        compiler_params=_params("parallel", "arbitrary"),
    )(x, g_pre, g_post, w_in, w_in, w_out)


def _ple_kernel(x_ref, p_ref, g0_ref, g1_ref, wg_ref, wp_ref, o_ref):
    x = x_ref[...]
    xn = _rms(x, g0_ref[...]).astype(BF16)
    gate = jax.nn.sigmoid(_dot(xn, wg_ref[...]))
    e = _dot(p_ref[...].astype(BF16), wp_ref[...])
    o_ref[...] = x + _rms(e * gate, g1_ref[...])


def _ple_block(x, p, g_pre, g_post, w_gate, w_p, tm=1024):
    t, d = x.shape
    pd = p.shape[1]
    tm = min(tm, t)
    row = pl.BlockSpec((tm, d), lambda i: (i, 0))
    vec = pl.BlockSpec((1, d), lambda i: (0, 0))
    return pl.pallas_call(
        _ple_kernel,
        grid=(t // tm,),
        in_specs=[row, pl.BlockSpec((tm, pd), lambda i: (i, 0)), vec, vec,
                  pl.BlockSpec((d, d), lambda i: (0, 0)),
                  pl.BlockSpec((pd, d), lambda i: (0, 0))],
        out_specs=row,
        out_shape=jax.ShapeDtypeStruct((t, d), F32),
        compiler_params=_params("parallel"),
    )(x, p, g_pre, g_post, w_gate, w_p)


def _rope_tile(xb, cos_t, sin_t, first_half):
    fwd = pltpu.roll(xb, LANES - NSA_HD // 2, 1)
    bwd = pltpu.roll(xb, NSA_HD // 2, 1)
    return xb * cos_t + jnp.where(first_half, fwd, bwd) * sin_t


def _nsa_in_kernel(x_ref, gain_ref, w_ref, cos_ref, sin_ref,
                   q_ref, kc_ref, vc_ref, ks_ref, vs_ref, kw_ref, vw_ref, gates_ref):
    xn = _rms(x_ref[...], gain_ref[...]).astype(BF16)
    cos_t = cos_ref[...]
    sin_t = sin_ref[...]
    tm = cos_t.shape[0]
    lane = lax.broadcasted_iota(jnp.int32, (tm, LANES), 1)
    first_half = (lane % NSA_HD) < (NSA_HD // 2)

    def chunk(c):
        return _dot(xn, w_ref[:, 256 * c:256 * (c + 1)])

    def roped(r):
        return [_rope_tile(r[:, LANES * h:LANES * (h + 1)], cos_t, sin_t, first_half) for h in range(2)]

    def store_heads(ref, base, tiles, scale):
        for h, tile in enumerate(tiles):
            ref[base + 2 * h] = (tile[:, :NSA_HD] * scale).astype(ref.dtype)
            ref[base + 2 * h + 1] = (tile[:, NSA_HD:] * scale).astype(ref.dtype)

    for c in range(4):
        store_heads(q_ref, 4 * c, roped(chunk(c)), NSA_HD ** -0.5)
    kc = roped(chunk(4))
    kc_ref[...] = jnp.concatenate(kc, axis=1)
    vc_ref[...] = chunk(5)
    store_heads(ks_ref, 0, roped(chunk(6)), 1.0)
    r = chunk(7)
    store_heads(vs_ref, 0, [r[:, :LANES], r[:, LANES:]], 1.0)
    store_heads(kw_ref, 0, roped(chunk(8)), 1.0)
    r = chunk(9)
    store_heads(vw_ref, 0, [r[:, :LANES], r[:, LANES:]], 1.0)
    gates_ref[:, :256] = jax.nn.sigmoid(chunk(10))
    gates_ref[:, 256:] = jax.nn.sigmoid(chunk(11))


def _nsa_in_proj(x, gain, w, cos_t, sin_t, tm=512):
    b, s, d = x.shape
    tm = min(tm, s)
    g, hd = NSA_GROUPS, NSA_HD
    kv_heads = pl.BlockSpec((None, g, tm, hd), lambda bi, i: (bi, 0, i, 0))
    kv_flat = pl.BlockSpec((None, tm, g * hd), lambda bi, i: (bi, i, 0))
    tab = pl.BlockSpec((tm, LANES), lambda bi, i: (i, 0))
    return pl.pallas_call(
        _nsa_in_kernel,
        grid=(b, s // tm),
        in_specs=[pl.BlockSpec((None, tm, d), lambda bi, i: (bi, i, 0)),
                  pl.BlockSpec((1, d), lambda bi, i: (0, 0)),
                  pl.BlockSpec(w.shape, lambda bi, i: (0, 0)),
                  tab, tab],
        out_specs=[pl.BlockSpec((None, NSA_HEADS, tm, hd), lambda bi, i: (bi, 0, i, 0)),
                   kv_flat, kv_flat, kv_heads, kv_heads, kv_heads, kv_heads,
                   pl.BlockSpec((None, tm, 4 * LANES), lambda bi, i: (bi, i, 0))],
        out_shape=[jax.ShapeDtypeStruct((b, NSA_HEADS, s, hd), BF16),
                   jax.ShapeDtypeStruct((b, s, g * hd), F32),
                   jax.ShapeDtypeStruct((b, s, g * hd), F32),
                   jax.ShapeDtypeStruct((b, g, s, hd), BF16),
                   jax.ShapeDtypeStruct((b, g, s, hd), BF16),
                   jax.ShapeDtypeStruct((b, g, s, hd), BF16),
                   jax.ShapeDtypeStruct((b, g, s, hd), BF16),
                   jax.ShapeDtypeStruct((b, s, 4 * LANES), F32)],
        compiler_params=_params("parallel", "parallel"),
    )(x, gain, w, cos_t, sin_t)


def _cmp_proj_kernel(x_ref, plo_ref, phi_ref, a_ref, b_ref, ya_ref, yb_ref):
    x = x_ref[...]
    ya_ref[...] = _dot((x + plo_ref[...]).astype(BF16), a_ref[...])
    yb_ref[...] = _dot((x + phi_ref[...]).astype(BF16), b_ref[...])


def _cmp_mix_kernel(ya_ref, yb_ref, w2_ref, o_ref):
    n = ya_ref.shape[0]
    z = ya_ref[...] + pltpu.roll(yb_ref[...], n - 1, 0)
    h = (z * jax.nn.sigmoid(z)).astype(BF16)
    out = _dot(h, w2_ref[...])
    for g in range(NSA_GROUPS):
        o_ref[g] = out[:, NSA_HD * g:NSA_HD * (g + 1)].astype(o_ref.dtype)


def _compress(kv, pos, w1, w2, tc=256):
    b, s, _ = kv.shape
    g, hd = NSA_GROUPS, NSA_HD
    n = s // CMP_STRIDE
    wide = CMP_STRIDE * g * hd
    tc = min(tc, n)
    x = kv.reshape(b, n, wide)
    eye = jnp.eye(g, dtype=F32)
    w1r = w1.reshape(CMP_BLOCK, hd, hd)
    mat_a = jnp.einsum("pde,gh->pgdhe", w1r[:CMP_STRIDE], eye).reshape(wide, g * hd).astype(BF16)
    mat_b = jnp.einsum("pde,gh->pgdhe", w1r[CMP_STRIDE:], eye).reshape(wide, g * hd).astype(BF16)
    p_lo = jnp.broadcast_to(pos[:CMP_STRIDE, None, :], (CMP_STRIDE, g, hd)).reshape(1, wide)
    p_hi = jnp.broadcast_to(pos[CMP_STRIDE:, None, :], (CMP_STRIDE, g, hd)).reshape(1, wide)
    w2bd = jnp.einsum("de,gh->gdhe", w2, eye).reshape(g * hd, g * hd).astype(BF16)
    vec = pl.BlockSpec((1, wide), lambda bi, i: (0, 0))
    mat = pl.BlockSpec((wide, g * hd), lambda bi, i: (0, 0))
    yspec = pl.BlockSpec((None, tc, g * hd), lambda bi, i: (bi, i, 0))
    ya, yb = pl.pallas_call(
        _cmp_proj_kernel,
        grid=(b, n // tc),
        in_specs=[pl.BlockSpec((None, tc, wide), lambda bi, i: (bi, i, 0)), vec, vec, mat, mat],
        out_specs=[yspec, yspec],
        out_shape=[jax.ShapeDtypeStruct((b, n, g * hd), F32)] * 2,
        compiler_params=_params("parallel", "parallel"),
    )(x, p_lo, p_hi, mat_a, mat_b)
    full = pl.BlockSpec((None, n, g * hd), lambda bi: (bi, 0, 0))
    return pl.pallas_call(
        _cmp_mix_kernel,
        grid=(b,),
        in_specs=[full, full, pl.BlockSpec((g * hd, g * hd), lambda bi: (0, 0))],
        out_specs=pl.BlockSpec((None, g, n, hd), lambda bi: (bi, 0, 0, 0)),
        out_shape=jax.ShapeDtypeStruct((b, g, n, hd), BF16),
        compiler_params=_params("parallel"),
    )(ya, yb, w2bd)


def _nsa_cw_kernel(q_ref, kc_ref, vc_ref, kw_ref, vw_ref, gates_ref, amat_ref, ocw_ref, idx_ref):
    qb = Q_BLOCK
    rows = NSA_HPG * qb
    t0 = pl.program_id(2) * qb
    q = q_ref[...].reshape(rows, NSA_HD)
    n_cmp = kc_ref.shape[0]
    n_sel = amat_ref.shape[1]
    t_row = t0 + lax.broadcasted_iota(jnp.int32, (rows, 1), 0) % qb

    s = _dot_nt(q, kc_ref[...])
    cmp_end = lax.broadcasted_iota(jnp.int32, (1, n_cmp), 1) * CMP_STRIDE + (CMP_BLOCK - 1)
    p = _softmax_rows(jnp.where(cmp_end <= t_row, s, NEG_INF))
    o_c = _dot(p.astype(BF16), vc_ref[...])
    p_grp = p[0:qb] + p[qb:2 * qb] + p[2 * qb:3 * qb] + p[3 * qb:4 * qb]
    imp = _dot_split(p_grp, amat_ref[...])

    t_tok = t0 + lax.broadcasted_iota(jnp.int32, (qb, 1), 0)
    cur = (t_tok // SEL_BLOCK).astype(F32)
    blk = lax.broadcasted_iota(jnp.int32, (1, n_sel), 1).astype(F32)
    forced = (blk == 0.0) | (blk == cur - 1.0)
    score = jnp.where(forced, jnp.inf, jnp.where(blk < cur, imp, NEG_INF))
    score = jnp.where(blk == cur, NEG_INF, score)
    slot = lax.broadcasted_iota(jnp.int32, (qb, N_SELECT), 1)
    sel = jnp.where(slot == 0, cur, 0.0)
    for n in range(1, N_SELECT):
        m = jnp.max(score, axis=-1, keepdims=True)
        pick = jnp.min(jnp.where(score == m, blk, float(n_sel)), axis=-1, keepdims=True)
        score = jnp.where(blk == pick, NEG_INF, score)
        sel = jnp.where(slot == n, pick, sel)
    idx_ref[...] = sel.astype(jnp.int32)

    w_len = WINDOW + qb
    w_start = pl.multiple_of(jnp.maximum(t0 - WINDOW, 0), qb)
    s_w = _dot_nt(q, kw_ref[pl.ds(w_start, w_len), :])
    kpos = w_start + lax.broadcasted_iota(jnp.int32, (1, w_len), 1)
    mask_w = (kpos <= t_row) & (kpos > t_row - WINDOW)
    p_w = _softmax_rows(jnp.where(mask_w, s_w, NEG_INF))
    o_w = _dot(p_w.astype(BF16), vw_ref[pl.ds(w_start, w_len), :])

    gates = gates_ref[...]
    outs = []
    for h in range(NSA_HPG):
        g_c = gates[:, 3 * h:3 * h + 1]
        g_w = gates[:, 3 * h + 2:3 * h + 3]
        outs.append(g_c * o_c[h * qb:(h + 1) * qb] + g_w * o_w[h * qb:(h + 1) * qb])
    ocw_ref[...] = jnp.concatenate(outs, axis=1)


def _nsa_cmp_win(q, k_cmp, v_cmp, k_w, v_w, gates, amat):
    b, _, s, hd = q.shape
    g = NSA_GROUPS
    n_cmp = k_cmp.shape[2]
    cmp_spec = pl.BlockSpec((None, None, n_cmp, hd), lambda bi, gi, j: (bi, gi, 0, 0))
    win_spec = pl.BlockSpec((None, None, s, hd), lambda bi, gi, j: (bi, gi, 0, 0))
    return pl.pallas_call(
        _nsa_cw_kernel,
        grid=(b, g, s // Q_BLOCK),
        in_specs=[pl.BlockSpec((None, NSA_HPG, Q_BLOCK, hd), lambda bi, gi, j: (bi, gi, j, 0)),
                  cmp_spec, cmp_spec, win_spec, win_spec,
                  pl.BlockSpec((None, Q_BLOCK, LANES), lambda bi, gi, j: (bi, j, gi)),
                  pl.BlockSpec(amat.shape, lambda bi, gi, j: (0, 0))],
        out_specs=[pl.BlockSpec((None, Q_BLOCK, NSA_HPG * hd), lambda bi, gi, j: (bi, j, gi)),
                   pl.BlockSpec((None, None, Q_BLOCK, N_SELECT), lambda bi, gi, j: (bi, gi, j, 0))],
        out_shape=[jax.ShapeDtypeStruct((b, s, NSA_HEADS * hd), F32),
                   jax.ShapeDtypeStruct((b, g, s, N_SELECT), jnp.int32)],
        compiler_params=_params("parallel", "parallel", "arbitrary"),
    )(q, k_cmp, v_cmp, k_w, v_w, gates, amat)


def _roll_rows(x, shift):
    return jnp.concatenate([pltpu.roll(x[r:r + 8], shift, 0) for r in range(0, x.shape[0], 8)], axis=0)


def _nsa_sel_kernel(idx_ref, q2_ref, k2_ref, v2_ref, o_ref, *, tq):
    half_blk = SEL_BLOCK // 2
    n_col = N_SELECT * half_blk
    t0 = pl.program_id(2) * tq
    row = lax.broadcasted_iota(jnp.int32, (16, n_col), 0)
    col = lax.broadcasted_iota(jnp.int32, (16, n_col), 1)
    pos0 = 2 * col + (row % 8) // NSA_HPG
    in_slot0 = col < half_blk
    second = (row >= 8).astype(jnp.int32)
    row8 = lax.broadcasted_iota(jnp.int32, (8, LANES), 0)

    def gather(ref, t):
        blocks = []
        for n in range(N_SELECT):
            off = pl.multiple_of(idx_ref[0, t * N_SELECT + n] * half_blk, half_blk)
            blocks.append(ref[pl.ds(off, half_blk), :])
        return jnp.concatenate(blocks, axis=0)

    def pair(tp, carry):
        ta = t0 + 2 * tp
        n_valid = jnp.minimum(ta // SEL_BLOCK + 1, N_SELECT)
        ok = jnp.where(in_slot0, pos0 - (ta % SEL_BLOCK + second), col - n_valid * half_blk + 1) <= 0
        q16 = q2_ref[pl.ds(pl.multiple_of(16 * tp, 16), 16), :]
        sa = _dot_nt(q16, gather(k2_ref, 2 * tp))
        sb = _dot_nt(q16, gather(k2_ref, 2 * tp + 1))
        s = jnp.where(ok, jnp.concatenate([sa[:8], sb[8:]], axis=0), NEG_INF)
        m = jnp.max(s, axis=-1, keepdims=True)
        m = jnp.maximum(m, _roll_rows(m, NSA_HPG))
        e = jnp.exp(s - m)
        l = jnp.sum(e, axis=-1, keepdims=True)
        l = l + _roll_rows(l, NSA_HPG)
        p = (e / l).astype(BF16)
        oa = _dot(p, gather(v2_ref, 2 * tp))
        ob = _dot(p, gather(v2_ref, 2 * tp + 1))
        x = jnp.concatenate([oa[:8], ob[8:]], axis=0)
        y = x + pltpu.roll(_roll_rows(x, NSA_HPG), NSA_HD, 1)
        out = jnp.where(row8 < NSA_HPG, y[:8], pltpu.roll(y[8:], NSA_HPG, 0))
        o_ref[pl.ds(pl.multiple_of(8 * tp, 8), 8), :] = out
        return carry

    lax.fori_loop(0, tq // 2, pair, 0)


def _nsa_selected(idx, q2, k2, v2, tq=128):
    b, g, s, _ = idx.shape
    tq = min(tq, s)
    nt = s // tq
    idx_flat = idx.reshape(b * g * nt, 1, tq * N_SELECT)
    kv_spec = pl.BlockSpec((None, None, s // 2, LANES), lambda bi, gi, j: (bi, gi, 0, 0))
    return pl.pallas_call(
        functools.partial(_nsa_sel_kernel, tq=tq),
        grid=(b, g, nt),
        in_specs=[pl.BlockSpec((None, 1, tq * N_SELECT), lambda bi, gi, j: ((bi * g + gi) * nt + j, 0, 0),
                               memory_space=pltpu.SMEM),
                  pl.BlockSpec((None, None, tq * 8, LANES), lambda bi, gi, j: (bi, gi, j, 0)),
                  kv_spec, kv_spec],
        out_specs=pl.BlockSpec((None, None, tq * NSA_HPG, LANES), lambda bi, gi, j: (bi, gi, j, 0)),
        out_shape=jax.ShapeDtypeStruct((b, g, s * NSA_HPG, LANES), F32),
        compiler_params=_params("parallel", "parallel", "arbitrary"),
    )(idx_flat, q2, k2, v2)


def _nsa_out_kernel(x_ref, ocw_ref, os_ref, gates_ref, emat_ref, w_ref, gain_ref, o_ref):
    g_s = _dot_split(gates_ref[...], emat_ref[...])
    a = (ocw_ref[...] + g_s * os_ref[...]).astype(BF16)
    o_ref[...] = x_ref[...] + _rms(_dot(a, w_ref[...]), gain_ref[...])


def _nsa_out_proj(x, o_cw, o_s, gates, emat, w, gain, tm=1024):
    t, d = x.shape
    tm = min(tm, t)
    row = pl.BlockSpec((tm, d), lambda i: (i, 0))
    return pl.pallas_call(
        _nsa_out_kernel,
        grid=(t // tm,),
        in_specs=[row, row, row,
                  pl.BlockSpec((tm, gates.shape[1]), lambda i: (i, 0)),
                  pl.BlockSpec(emat.shape, lambda i: (0, 0)),
                  pl.BlockSpec(w.shape, lambda i: (0, 0)),
                  pl.BlockSpec((1, d), lambda i: (0, 0))],
        out_specs=row,
        out_shape=jax.ShapeDtypeStruct((t, d), F32),
        compiler_params=_params("parallel"),
    )(x, o_cw, o_s, gates, emat, w, gain)


def _proj_out_kernel(x_ref, a_ref, w_ref, gain_ref, o_ref):
    o_ref[...] = x_ref[...] + _rms(_dot(a_ref[...], w_ref[...]), gain_ref[...])


def _proj_out(x, a, w, gain, tm=1024):
    t, d = x.shape
    tm = min(tm, t)
    row = pl.BlockSpec((tm, d), lambda i: (i, 0))
    return pl.pallas_call(
        _proj_out_kernel,
        grid=(t // tm,),
        in_specs=[row, pl.BlockSpec((tm, a.shape[1]), lambda i: (i, 0)),
                  pl.BlockSpec(w.shape, lambda i: (0, 0)),
                  pl.BlockSpec((1, d), lambda i: (0, 0))],
        out_specs=row,
        out_shape=jax.ShapeDtypeStruct((t, d), F32),
        compiler_params=_params("parallel"),
    )(x, a, w, gain)


def _nsa_mixer(x, gain_pre, gain_post, w_in, w_out, cmp_pos, cmp_w1, cmp_w2):
    b, s, d = x.shape
    g, hd, hpg = NSA_GROUPS, NSA_HD, NSA_HPG
    n_gate = 3 * hpg
    q_cols = NSA_HEADS * hd
    kv_end = q_cols + 6 * NSA_KV_COLS

    gate_w = w_in[:, kv_end:].reshape(d, g, n_gate)
    gate_w = jnp.pad(gate_w, ((0, 0), (0, 0), (0, LANES - n_gate))).reshape(d, g * LANES)
    w_all = jnp.concatenate([w_in[:, :kv_end], gate_w], axis=1).astype(BF16)

    half = hd // 2
    inv = ROPE_THETA ** (-jnp.arange(half, dtype=F32) / half)
    ang = jnp.arange(s).astype(F32)[:, None] * inv[None, :]
    cos, sin = jnp.cos(ang), jnp.sin(ang)
    cos_t = jnp.tile(cos, (1, LANES // half))
    sin_t = jnp.tile(jnp.concatenate([-sin, sin], axis=1), (1, LANES // hd))

    q, k_c, v_c, k_s, v_s, k_w, v_w, gates = _nsa_in_proj(x, gain_pre, w_all, cos_t, sin_t)
    k_cmp = _compress(k_c, cmp_pos[0], cmp_w1[0], cmp_w2[0])
    v_cmp = _compress(v_c, cmp_pos[1], cmp_w1[1], cmp_w2[1])

    n_cmp_rows = s // CMP_STRIDE
    n_sel = s // SEL_BLOCK
    ratio = SEL_BLOCK // CMP_STRIDE
    c_id = jnp.arange(n_cmp_rows)[:, None]
    n_id = jnp.arange(n_sel)[None, :]
    lo = ratio * n_id - (CMP_BLOCK // CMP_STRIDE - 1)
    amat = ((c_id >= lo) & (c_id <= ratio * n_id + ratio - 1) & (c_id < n_cmp_rows - 1)).astype(BF16)

    o_cw, idx = _nsa_cmp_win(q, k_cmp, v_cmp, k_w, v_w, gates, amat)

    qt = q.reshape(b, g, hpg, s, hd).transpose(0, 1, 3, 2, 4)
    zero = jnp.zeros_like(qt)
    q2 = jnp.stack([jnp.concatenate([qt, zero], axis=-1), jnp.concatenate([zero, qt], axis=-1)], axis=3)
    q2 = q2.reshape(b, g, s * 2 * hpg, 2 * hd)
    k2 = k_s.reshape(b, g, s // 2, 2 * hd)
    v2 = v_s.reshape(b, g, s // 2, 2 * hd)
    o_s = _nsa_selected(idx, q2, k2, v2)
    o_s = o_s[..., :hd].reshape(b, g, s, hpg * hd).transpose(0, 2, 1, 3).reshape(b * s, q_cols)

    r_id = jnp.arange(g * LANES)[:, None]
    col = jnp.arange(q_cols)[None, :]
    emat = (r_id == (col // (hpg * hd)) * LANES + 3 * ((col // hd) % hpg) + 1).astype(BF16)

    out = _nsa_out_proj(x.reshape(b * s, d), o_cw.reshape(b * s, q_cols), o_s,
                        gates.reshape(b * s, g * LANES), emat, w_out.astype(BF16), gain_post)
    return out.reshape(b, s, d)


def _hgrn_in_kernel(x_ref, gain_ref, w_ref, lb_ref, q_ref, k_ref, lf_ref, v_ref, gs_ref):
    xn = _rms(x_ref[...], gain_ref[...]).astype(BF16)
    n_chunk = D_MODEL // 256

    def chunk(c):
        return _dot(xn, w_ref[:, 256 * c:256 * (c + 1)])

    for c in range(n_chunk):
        cols = slice(256 * c, 256 * (c + 1))
        r = chunk(c)
        q_ref[:, cols] = r * jax.nn.sigmoid(r)
        f = chunk(n_chunk + c)
        lb = lb_ref[:, cols]
        log_sig = jnp.minimum(f, 0.0) - jnp.log1p(jnp.exp(-jnp.abs(f)))
        a = jnp.log(lb)
        bb = jnp.log1p(-lb) + log_sig
        log_f = jnp.maximum(a, bb) + jnp.log1p(jnp.exp(-jnp.abs(a - bb)))
        lf_ref[:, cols] = log_f
        k_ref[:, cols] = 1.0 - jnp.exp(log_f)
        v_ref[:, cols] = chunk(2 * n_chunk + c)
        r = chunk(3 * n_chunk + c)
        gs_ref[:, cols] = r * jax.nn.sigmoid(r)


def _hgrn_in_proj(x, gain, w, lb, tm=512):
    t, d = x.shape
    tm = min(tm, t)
    row = pl.BlockSpec((tm, d), lambda i: (i, 0))
    vec = pl.BlockSpec((1, d), lambda i: (0, 0))
    return pl.pallas_call(
        _hgrn_in_kernel,
        grid=(t // tm,),
        in_specs=[row, vec, pl.BlockSpec(w.shape, lambda i: (0, 0)), vec],
        out_specs=[row] * 5,
        out_shape=[jax.ShapeDtypeStruct((t, d), F32)] * 5,
        compiler_params=_params("parallel"),
    )(x, gain, w, lb)


def _hgrn_scan_kernel(q_ref, k_ref, lf_ref, v_ref, gs_ref, gain_ref, o_ref, state_ref, *, rows):
    sub = HGRN_SUB

    @pl.when(pl.program_id(2) == 0)
    def _():
        state_ref[...] = jnp.zeros_like(state_ref)

    lf = lf_ref[...]
    r_id = lax.broadcasted_iota(jnp.int32, (rows, rows), 0)
    c_id = lax.broadcasted_iota(jnp.int32, (rows, rows), 1)
    same = (r_id // sub) == (c_id // sub)
    tri = jnp.where(same & (c_id <= r_id), 1.0, 0.0)
    box = jnp.where(same, 1.0, 0.0)
    cum = jnp.dot(tri, lf, preferred_element_type=F32, precision=lax.Precision.HIGHEST)
    tot = jnp.dot(box, lf, preferred_element_type=F32, precision=lax.Precision.HIGHEST)
    q = q_ref[...]
    k = k_ref[...]
    v = v_ref[...]
    q_dec = (q * jnp.exp(cum)).astype(BF16)
    k_dec = (k * jnp.exp(tot - cum)).astype(BF16)
    dec = jnp.exp(tot)
    ones = jnp.ones((HGRN_DK, LANES), BF16)
    row_id = lax.broadcasted_iota(jnp.int32, (sub, HGRN_DK), 0)

    state = state_ref[...]
    outs = []
    for i in range(rows // sub):
        sl = slice(sub * i, sub * (i + 1))
        qi, ki, vi, ci = q[sl], k[sl], v[sl], cum[sl]
        terms = []
        for j in range(sub):
            w = jnp.exp(jnp.where(row_id >= j, ci - ci[j:j + 1], NEG_INF))
            terms.append(qi * w * ki[j:j + 1])
        att = _dot(jnp.concatenate(terms, axis=0).astype(BF16), ones)
        o_i = _dot_nt(q_dec[sl], state.astype(BF16))
        for j in range(sub):
            o_i = o_i + att[sub * j:sub * (j + 1)] * vi[j:j + 1]
        outs.append(o_i)
        state = state * dec[sub * i:sub * i + 1] + _dot_tn(vi.astype(BF16), k_dec[sl])
    state_ref[...] = state

    o = jnp.concatenate(outs, axis=0)
    o_ref[...] = (_rms(o, gain_ref[...]) * gs_ref[...]).astype(o_ref.dtype)


def _hgrn_scan(q, k, lf, v, gs, gain, rows=128):
    b, s, d = q.shape
    rows = min(rows, s)
    blk = pl.BlockSpec((None, rows, LANES), lambda bi, h, l: (bi, l, h))
    return pl.pallas_call(
        functools.partial(_hgrn_scan_kernel, rows=rows),
        grid=(b, d // LANES, s // rows),
        in_specs=[blk] * 5 + [pl.BlockSpec((1, LANES), lambda bi, h, l: (0, 0))],
        out_specs=blk,
        out_shape=jax.ShapeDtypeStruct((b, s, d), BF16),
        scratch_shapes=[pltpu.VMEM((LANES, HGRN_DK), F32)],
        compiler_params=_params("parallel", "parallel", "arbitrary"),
    )(q, k, lf, v, gs, gain)


def _hgrn_mixer(x, gain_pre, gain_post, w_in, w_out, norm_gain, lower_bound):
    b, s, d = x.shape
    x2 = x.reshape(b * s, d)
    q, k, lf, v, gs = _hgrn_in_proj(x2, gain_pre, w_in.astype(BF16), lower_bound.reshape(1, d))
    shape = (b, s, d)
    o = _hgrn_scan(q.reshape(shape), k.reshape(shape), lf.reshape(shape), v.reshape(shape),
                   gs.reshape(shape), norm_gain.reshape(1, LANES))
    return _proj_out(x2, o.reshape(b * s, d), w_out.astype(BF16), gain_post).reshape(shape)


def kernel(x, p, norm_gains, ffn_w_in, ffn_w_out, ple_w_in, ple_w_gate, nsa_w_in, nsa_w_out, nsa_cmp_pos,
           nsa_cmp_w1, nsa_cmp_w2, hgrn_w_in, hgrn_w_out, hgrn_norm, hgrn_lb_logits):
    b, s, d = x.shape
    depth = norm_gains.shape[0]
    lb_sm = jax.nn.softmax(hgrn_lb_logits.astype(F32), axis=0)
    lower_bounds = jnp.cumsum(lb_sm, axis=0) - lb_sm[0]
    for layer in range(depth):
        ng = norm_gains[layer].reshape(8, 1, d)
        x2 = x.reshape(b * s, d)
        x2 = _ffn_block(x2, ng[0], ng[1], ffn_w_in[layer, 0].astype(BF16), ffn_w_out[layer, 0].astype(BF16))
        x = x2.reshape(b, s, d)
        j = layer // 2
        if layer % 2 == 0:
            x = _nsa_mixer(x, ng[2], ng[3], nsa_w_in[j], nsa_w_out[j], nsa_cmp_pos[j], nsa_cmp_w1[j],
                           nsa_cmp_w2[j])
        else:
            x = _hgrn_mixer(x, ng[2], ng[3], hgrn_w_in[j], hgrn_w_out[j], hgrn_norm[j], lower_bounds[layer])
        x2 = x.reshape(b * s, d)
        x2 = _ffn_block(x2, ng[4], ng[5], ffn_w_in[layer, 1].astype(BF16), ffn_w_out[layer, 1].astype(BF16))
        x2 = _ple_block(x2, p[layer].reshape(b * s, -1), ng[6], ng[7], ple_w_gate[layer].astype(BF16),
                        ple_w_in[layer].astype(BF16))
        x = x2.reshape(b, s, d)
    return x
```

```python
import functools

import jax
import jax.numpy as jnp
from jax import lax
from jax.experimental import pallas as pl
from jax.experimental.pallas import tpu as pltpu

F32 = jnp.float32
BF16 = jnp.bfloat16
NEG_INF = float("-inf")

D_MODEL = 1024
NORM_EPS = 1e-6
ROPE_THETA = 10000.0

NSA_HEADS = 16
NSA_GROUPS = 4
NSA_HPG = NSA_HEADS // NSA_GROUPS
NSA_HD = 64
NSA_KV_COLS = NSA_GROUPS * NSA_HD
CMP_BLOCK = 32
CMP_STRIDE = 16
SEL_BLOCK = 64
N_SELECT = 16
WINDOW = 512
Q_BLOCK = 128

HGRN_HEADS = 8
HGRN_DK = 128
HGRN_SUB = 16

LANES = 128
VMEM_LIMIT = 56 * 1024 * 1024


def _params(*sem):
    return pltpu.CompilerParams(dimension_semantics=sem, vmem_limit_bytes=VMEM_LIMIT)


def _rms(x, gain):
    return x * lax.rsqrt(jnp.mean(x * x, axis=-1, keepdims=True) + NORM_EPS) * gain


def _dot(a, b):
    return jnp.dot(a, b, preferred_element_type=F32)


def _dot_nt(a, b):
    return lax.dot_general(a, b, (((1,), (1,)), ((), ())), preferred_element_type=F32)


def _dot_tn(a, b):
    return lax.dot_general(a, b, (((0,), (0,)), ((), ())), preferred_element_type=F32)


def _dot_split(x, w):
    hi = x.astype(BF16)
    lo = (x - hi.astype(F32)).astype(BF16)
    return _dot(hi, w) + _dot(lo, w)


def _softmax_rows(s):
    m = jnp.max(s, axis=-1, keepdims=True)
    m = jnp.where(m == NEG_INF, 0.0, m)
    e = jnp.exp(s - m)
    return e / jnp.maximum(jnp.sum(e, axis=-1, keepdims=True), 1e-30)


def _ffn_kernel(x_ref, g0_ref, g1_ref, wg_ref, wu_ref, wo_ref, o_ref, xn_ref, acc_ref):
    j = pl.program_id(1)

    @pl.when(j == 0)
    def _():
        xn_ref[...] = _rms(x_ref[...], g0_ref[...]).astype(BF16)
        acc_ref[...] = jnp.zeros_like(acc_ref)

    xn = xn_ref[...]
    g = _dot(xn, wg_ref[...])
    u = _dot(xn, wu_ref[...])
    h = (g * jax.nn.sigmoid(g) * u).astype(BF16)
    acc_ref[...] += _dot(h, wo_ref[...])

    @pl.when(j == pl.num_programs(1) - 1)
    def _():
        o_ref[...] = x_ref[...] + 0.5 * _rms(acc_ref[...], g1_ref[...])


def _ffn_block(x, g_pre, g_post, w_in, w_out, tm=1024, tf=256):
    t, d = x.shape
    dff = w_out.shape[0]
    tm = min(tm, t)
    nf = dff // tf
    row = pl.BlockSpec((tm, d), lambda i, j: (i, 0))
    vec = pl.BlockSpec((1, d), lambda i, j: (0, 0))
    return pl.pallas_call(
        _ffn_kernel,
        name="ffn",
        grid=(t // tm, nf),
        in_specs=[row, vec, vec,
                  pl.BlockSpec((d, tf), lambda i, j: (0, j)),
                  pl.BlockSpec((d, tf), lambda i, j: (0, j + nf)),
                  pl.BlockSpec((tf, d), lambda i, j: (j, 0))],
        out_specs=row,
        out_shape=jax.ShapeDtypeStruct((t, d), F32),
        scratch_shapes=[pltpu.VMEM((tm, d), BF16), pltpu.VMEM((tm, d), F32)],
        compiler_params=_params("parallel", "arbitrary"),
    )(x, g_pre, g_post, w_in, w_in, w_out)


def _ple_kernel(x_ref, p_ref, g0_ref, g1_ref, wg_ref, wp_ref, o_ref):
    x = x_ref[...]
    xn = _rms(x, g0_ref[...]).astype(BF16)
    gate = jax.nn.sigmoid(_dot(xn, wg_ref[...]))
    e = _dot(p_ref[...].astype(BF16), wp_ref[...])
    o_ref[...] = x + _rms(e * gate, g1_ref[...])


def _ple_block(x, p, g_pre, g_post, w_gate, w_p, tm=1024):
    t, d = x.shape
    pd = p.shape[1]
    tm = min(tm, t)
    row = pl.BlockSpec((tm, d), lambda i: (i, 0))
    vec = pl.BlockSpec((1, d), lambda i: (0, 0))
    return pl.pallas_call(
        _ple_kernel,
        name="ple",
        grid=(t // tm,),
        in_specs=[row, pl.BlockSpec((tm, pd), lambda i: (i, 0)), vec, vec,
                  pl.BlockSpec((d, d), lambda i: (0, 0)),
                  pl.BlockSpec((pd, d), lambda i: (0, 0))],
        out_specs=row,
        out_shape=jax.ShapeDtypeStruct((t, d), F32),
        compiler_params=_params("parallel"),
    )(x, p, g_pre, g_post, w_gate, w_p)


def _rope_tile(xb, cos_t, sin_t, first_half):
    fwd = pltpu.roll(xb, LANES - NSA_HD // 2, 1)
    bwd = pltpu.roll(xb, NSA_HD // 2, 1)
    return xb * cos_t + jnp.where(first_half, fwd, bwd) * sin_t


def _nsa_in_kernel(x_ref, gain_ref, w_ref, cos_ref, sin_ref,
                   q_ref, kc_ref, vc_ref, ks_ref, vs_ref, kw_ref, vw_ref, gates_ref):
    xn = _rms(x_ref[...], gain_ref[...]).astype(BF16)
    cos_t = cos_ref[...]
    sin_t = sin_ref[...]
    tm = cos_t.shape[0]
    lane = lax.broadcasted_iota(jnp.int32, (tm, LANES), 1)
    first_half = (lane % NSA_HD) < (NSA_HD // 2)

    def chunk(c):
        return _dot(xn, w_ref[:, 256 * c:256 * (c + 1)])

    def roped(r):
        return [_rope_tile(r[:, LANES * h:LANES * (h + 1)], cos_t, sin_t, first_half) for h in range(2)]

    def store_heads(ref, base, tiles, scale):
        for h, tile in enumerate(tiles):
            ref[base + 2 * h] = (tile[:, :NSA_HD] * scale).astype(ref.dtype)
            ref[base + 2 * h + 1] = (tile[:, NSA_HD:] * scale).astype(ref.dtype)

    for c in range(4):
        store_heads(q_ref, 4 * c, roped(chunk(c)), NSA_HD ** -0.5)
    kc = roped(chunk(4))
    kc_ref[...] = jnp.concatenate(kc, axis=1)
    vc_ref[...] = chunk(5)
    store_heads(ks_ref, 0, roped(chunk(6)), 1.0)
    r = chunk(7)
    store_heads(vs_ref, 0, [r[:, :LANES], r[:, LANES:]], 1.0)
    store_heads(kw_ref, 0, roped(chunk(8)), 1.0)
    r = chunk(9)
    store_heads(vw_ref, 0, [r[:, :LANES], r[:, LANES:]], 1.0)
    gates_ref[:, :256] = jax.nn.sigmoid(chunk(10))
    gates_ref[:, 256:] = jax.nn.sigmoid(chunk(11))


def _nsa_in_proj(x, gain, w, cos_t, sin_t, tm=512):
    b, s, d = x.shape
    tm = min(tm, s)
    g, hd = NSA_GROUPS, NSA_HD
    kv_heads = pl.BlockSpec((None, g, tm, hd), lambda bi, i: (bi, 0, i, 0))
    kv_flat = pl.BlockSpec((None, tm, g * hd), lambda bi, i: (bi, i, 0))
    tab = pl.BlockSpec((tm, LANES), lambda bi, i: (i, 0))
    return pl.pallas_call(
        _nsa_in_kernel,
        name="nsa_in",
        grid=(b, s // tm),
        in_specs=[pl.BlockSpec((None, tm, d), lambda bi, i: (bi, i, 0)),
                  pl.BlockSpec((1, d), lambda bi, i: (0, 0)),
                  pl.BlockSpec(w.shape, lambda bi, i: (0, 0)),
                  tab, tab],
        out_specs=[pl.BlockSpec((None, NSA_HEADS, tm, hd), lambda bi, i: (bi, 0, i, 0)),
                   kv_flat, kv_flat, kv_heads, kv_heads, kv_heads, kv_heads,
                   pl.BlockSpec((None, tm, 4 * LANES), lambda bi, i: (bi, i, 0))],
        out_shape=[jax.ShapeDtypeStruct((b, NSA_HEADS, s, hd), BF16),
                   jax.ShapeDtypeStruct((b, s, g * hd), F32),
                   jax.ShapeDtypeStruct((b, s, g * hd), F32),
                   jax.ShapeDtypeStruct((b, g, s, hd), BF16),
                   jax.ShapeDtypeStruct((b, g, s, hd), BF16),
                   jax.ShapeDtypeStruct((b, g, s, hd), BF16),
                   jax.ShapeDtypeStruct((b, g, s, hd), BF16),
                   jax.ShapeDtypeStruct((b, s, 4 * LANES), F32)],
        compiler_params=_params("parallel", "parallel"),
    )(x, gain, w, cos_t, sin_t)


def _cmp_proj_kernel(x_ref, plo_ref, phi_ref, a_ref, b_ref, ya_ref, yb_ref):
    x = x_ref[...]
    ya_ref[...] = _dot((x + plo_ref[...]).astype(BF16), a_ref[...])
    yb_ref[...] = _dot((x + phi_ref[...]).astype(BF16), b_ref[...])


def _cmp_mix_kernel(ya_ref, yb_ref, w2_ref, o_ref):
    n = ya_ref.shape[0]
    z = ya_ref[...] + pltpu.roll(yb_ref[...], n - 1, 0)
    h = (z * jax.nn.sigmoid(z)).astype(BF16)
    out = _dot(h, w2_ref[...])
    for g in range(NSA_GROUPS):
        o_ref[g] = out[:, NSA_HD * g:NSA_HD * (g + 1)].astype(o_ref.dtype)


def _compress(kv, pos, w1, w2, tc=256):
    b, s, _ = kv.shape
    g, hd = NSA_GROUPS, NSA_HD
    n = s // CMP_STRIDE
    wide = CMP_STRIDE * g * hd
    tc = min(tc, n)
    x = kv.reshape(b, n, wide)
    eye = jnp.eye(g, dtype=F32)
    w1r = w1.reshape(CMP_BLOCK, hd, hd)
    mat_a = jnp.einsum("pde,gh->pgdhe", w1r[:CMP_STRIDE], eye).reshape(wide, g * hd).astype(BF16)
    mat_b = jnp.einsum("pde,gh->pgdhe", w1r[CMP_STRIDE:], eye).reshape(wide, g * hd).astype(BF16)
    p_lo = jnp.broadcast_to(pos[:CMP_STRIDE, None, :], (CMP_STRIDE, g, hd)).reshape(1, wide)
    p_hi = jnp.broadcast_to(pos[CMP_STRIDE:, None, :], (CMP_STRIDE, g, hd)).reshape(1, wide)
    w2bd = jnp.einsum("de,gh->gdhe", w2, eye).reshape(g * hd, g * hd).astype(BF16)
    vec = pl.BlockSpec((1, wide), lambda bi, i: (0, 0))
    mat = pl.BlockSpec((wide, g * hd), lambda bi, i: (0, 0))
    yspec = pl.BlockSpec((None, tc, g * hd), lambda bi, i: (bi, i, 0))
    ya, yb = pl.pallas_call(
        _cmp_proj_kernel,
        name="cmp_proj",
        grid=(b, n // tc),
        in_specs=[pl.BlockSpec((None, tc, wide), lambda bi, i: (bi, i, 0)), vec, vec, mat, mat],
        out_specs=[yspec, yspec],
        out_shape=[jax.ShapeDtypeStruct((b, n, g * hd), F32)] * 2,
        compiler_params=_params("parallel", "parallel"),
    )(x, p_lo, p_hi, mat_a, mat_b)
    full = pl.BlockSpec((None, n, g * hd), lambda bi: (bi, 0, 0))
    return pl.pallas_call(
        _cmp_mix_kernel,
        name="cmp_mix",
        grid=(b,),
        in_specs=[full, full, pl.BlockSpec((g * hd, g * hd), lambda bi: (0, 0))],
        out_specs=pl.BlockSpec((None, g, n, hd), lambda bi: (bi, 0, 0, 0)),
        out_shape=jax.ShapeDtypeStruct((b, g, n, hd), BF16),
        compiler_params=_params("parallel"),
    )(ya, yb, w2bd)


def _nsa_cw_kernel(q_ref, kc_ref, vc_ref, kw_ref, vw_ref, gates_ref, amat_ref, ocw_ref, idx_ref):
    qb = Q_BLOCK
    rows = NSA_HPG * qb
    t0 = pl.program_id(2) * qb
    q = q_ref[...].reshape(rows, NSA_HD)
    n_cmp = kc_ref.shape[0]
    n_sel = amat_ref.shape[1]
    t_row = t0 + lax.broadcasted_iota(jnp.int32, (rows, 1), 0) % qb

    s = _dot_nt(q, kc_ref[...])
    cmp_end = lax.broadcasted_iota(jnp.int32, (1, n_cmp), 1) * CMP_STRIDE + (CMP_BLOCK - 1)
    p = _softmax_rows(jnp.where(cmp_end <= t_row, s, NEG_INF))
    o_c = _dot(p.astype(BF16), vc_ref[...])
    p_grp = p[0:qb] + p[qb:2 * qb] + p[2 * qb:3 * qb] + p[3 * qb:4 * qb]
    imp = _dot_split(p_grp, amat_ref[...])

    t_tok = t0 + lax.broadcasted_iota(jnp.int32, (qb, 1), 0)
    cur = (t_tok // SEL_BLOCK).astype(F32)
    blk = lax.broadcasted_iota(jnp.int32, (1, n_sel), 1).astype(F32)
    forced = (blk == 0.0) | (blk == cur - 1.0)
    score = jnp.where(forced, jnp.inf, jnp.where(blk < cur, imp, NEG_INF))
    score = jnp.where(blk == cur, NEG_INF, score)
    slot = lax.broadcasted_iota(jnp.int32, (qb, N_SELECT), 1)
    sel = jnp.where(slot == 0, cur, 0.0)
    for n in range(1, N_SELECT):
        m = jnp.max(score, axis=-1, keepdims=True)
        pick = jnp.min(jnp.where(score == m, blk, float(n_sel)), axis=-1, keepdims=True)
        score = jnp.where(blk == pick, NEG_INF, score)
        sel = jnp.where(slot == n, pick, sel)
    idx_ref[...] = sel.astype(jnp.int32)

    w_len = WINDOW + qb
    w_start = pl.multiple_of(jnp.maximum(t0 - WINDOW, 0), qb)
    s_w = _dot_nt(q, kw_ref[pl.ds(w_start, w_len), :])
    kpos = w_start + lax.broadcasted_iota(jnp.int32, (1, w_len), 1)
    mask_w = (kpos <= t_row) & (kpos > t_row - WINDOW)
    p_w = _softmax_rows(jnp.where(mask_w, s_w, NEG_INF))
    o_w = _dot(p_w.astype(BF16), vw_ref[pl.ds(w_start, w_len), :])

    gates = gates_ref[...]
    outs = []
    for h in range(NSA_HPG):
        g_c = gates[:, 3 * h:3 * h + 1]
        g_w = gates[:, 3 * h + 2:3 * h + 3]
        outs.append(g_c * o_c[h * qb:(h + 1) * qb] + g_w * o_w[h * qb:(h + 1) * qb])
    ocw_ref[...] = jnp.concatenate(outs, axis=1)


def _nsa_cmp_win(q, k_cmp, v_cmp, k_w, v_w, gates, amat):
    b, _, s, hd = q.shape
    g = NSA_GROUPS
    n_cmp = k_cmp.shape[2]
    cmp_spec = pl.BlockSpec((None, None, n_cmp, hd), lambda bi, gi, j: (bi, gi, 0, 0))
    win_spec = pl.BlockSpec((None, None, s, hd), lambda bi, gi, j: (bi, gi, 0, 0))
    return pl.pallas_call(
        _nsa_cw_kernel,
        name="nsa_cw",
        grid=(b, g, s // Q_BLOCK),
        in_specs=[pl.BlockSpec((None, NSA_HPG, Q_BLOCK, hd), lambda bi, gi, j: (bi, gi, j, 0)),
                  cmp_spec, cmp_spec, win_spec, win_spec,
                  pl.BlockSpec((None, Q_BLOCK, LANES), lambda bi, gi, j: (bi, j, gi)),
                  pl.BlockSpec(amat.shape, lambda bi, gi, j: (0, 0))],
        out_specs=[pl.BlockSpec((None, Q_BLOCK, NSA_HPG * hd), lambda bi, gi, j: (bi, j, gi)),
                   pl.BlockSpec((None, None, Q_BLOCK, N_SELECT), lambda bi, gi, j: (bi, gi, j, 0))],
        out_shape=[jax.ShapeDtypeStruct((b, s, NSA_HEADS * hd), F32),
                   jax.ShapeDtypeStruct((b, g, s, N_SELECT), jnp.int32)],
        compiler_params=_params("parallel", "parallel", "arbitrary"),
    )(q, k_cmp, v_cmp, k_w, v_w, gates, amat)


def _roll_rows(x, shift):
    return jnp.concatenate([pltpu.roll(x[r:r + 8], shift, 0) for r in range(0, x.shape[0], 8)], axis=0)


def _nsa_sel_kernel(idx_ref, q2_ref, kv2_ref, o_ref, s_a, s_b, v_a, v_b, *, tq, pairs):
    half_blk = SEL_BLOCK // 2
    n_col = N_SELECT * half_blk
    t0 = pl.program_id(2) * tq
    row = lax.broadcasted_iota(jnp.int32, (16, n_col), 0)
    col = lax.broadcasted_iota(jnp.int32, (16, n_col), 1)
    pos0 = 2 * col + (row % 8) // NSA_HPG
    in_slot0 = col < half_blk
    second = (row >= 8).astype(jnp.int32)
    row8 = lax.broadcasted_iota(jnp.int32, (8, LANES), 0)

    def stage_scores(grp, s_buf, v_buf):
        for u in range(pairs):
            tp = grp * pairs + u
            q16 = q2_ref[pl.ds(pl.multiple_of(16 * tp, 16), 16), :]
            halves = []
            for w in range(2):
                blocks = []
                for n in range(N_SELECT):
                    off = pl.multiple_of(idx_ref[0, (2 * tp + w) * N_SELECT + n] * half_blk, half_blk)
                    blocks.append(kv2_ref[pl.ds(off, half_blk), :])
                keys = jnp.concatenate([blk[:, :LANES] for blk in blocks], axis=0)
                v_buf[(2 * u + w) * n_col:(2 * u + w + 1) * n_col, :] = jnp.concatenate(
                    [blk[:, LANES:] for blk in blocks], axis=0)
                halves.append(_dot_nt(q16, keys)[8 * w:8 * w + 8])
            s_buf[16 * u:16 * u + 16, :] = jnp.concatenate(halves, axis=0)

    def stage_output(grp, s_buf, v_buf):
        for u in range(pairs):
            tp = grp * pairs + u
            ta = t0 + 2 * tp
            n_valid = jnp.minimum(ta // SEL_BLOCK + 1, N_SELECT)
            ok = jnp.where(in_slot0, pos0 - (ta % SEL_BLOCK + second), col - n_valid * half_blk + 1) <= 0
            s = jnp.where(ok, s_buf[16 * u:16 * u + 16, :], NEG_INF)
            m = jnp.max(s, axis=-1, keepdims=True)
            m = jnp.maximum(m, _roll_rows(m, NSA_HPG))
            e = jnp.exp(s - m)
            l = jnp.sum(e, axis=-1, keepdims=True)
            l = l + _roll_rows(l, NSA_HPG)
            p = (e / l).astype(BF16)
            xa = _dot(p, v_buf[(2 * u) * n_col:(2 * u + 1) * n_col, :])
            xb = _dot(p, v_buf[(2 * u + 1) * n_col:(2 * u + 2) * n_col, :])
            x = jnp.concatenate([xa[:8], xb[8:]], axis=0)
            y = x + pltpu.roll(_roll_rows(x, NSA_HPG), NSA_HD, 1)
            out = jnp.where(row8 < NSA_HPG, y[:8], pltpu.roll(y[8:], NSA_HPG, 0))
            o_ref[pl.ds(pl.multiple_of(8 * tp, 8), 8), :] = out

    n_grp = tq // (2 * pairs)
    stage_scores(0, s_a, v_a)

    def body(k, carry):
        stage_scores(2 * k + 1, s_b, v_b)
        stage_output(2 * k, s_a, v_a)
        stage_scores(jnp.minimum(2 * k + 2, n_grp - 1), s_a, v_a)
        stage_output(2 * k + 1, s_b, v_b)
        return carry

    lax.fori_loop(0, n_grp // 2, body, 0)


def _nsa_selected(idx, q2, kv2, tq=256, pairs=8):
    b, g, s, _ = idx.shape
    tq = min(tq, s)
    nt = s // tq
    assert (tq // (2 * pairs)) % 2 == 0
    idx_flat = idx.reshape(b * g * nt, 1, tq * N_SELECT)
    n_col = N_SELECT * SEL_BLOCK // 2
    scratch = [pltpu.VMEM((pairs * 16, n_col), F32)] * 2 + [pltpu.VMEM((pairs * 2 * n_col, LANES), BF16)] * 2
    return pl.pallas_call(
        functools.partial(_nsa_sel_kernel, tq=tq, pairs=pairs),
        scratch_shapes=scratch,
        name="nsa_sel",
        grid=(b, g, nt),
        in_specs=[pl.BlockSpec((None, 1, tq * N_SELECT), lambda bi, gi, j: ((bi * g + gi) * nt + j, 0, 0),
                               memory_space=pltpu.SMEM),
                  pl.BlockSpec((None, None, tq * 8, LANES), lambda bi, gi, j: (bi, gi, j, 0)),
                  pl.BlockSpec((None, None, s // 2, 2 * LANES), lambda bi, gi, j: (bi, gi, 0, 0))],
        out_specs=pl.BlockSpec((None, None, tq * NSA_HPG, LANES), lambda bi, gi, j: (bi, gi, j, 0)),
        out_shape=jax.ShapeDtypeStruct((b, g, s * NSA_HPG, LANES), F32),
        compiler_params=_params("parallel", "parallel", "arbitrary"),
    )(idx_flat, q2, kv2)


def _nsa_out_kernel(x_ref, ocw_ref, os_ref, gates_ref, emat_ref, w_ref, gain_ref, o_ref):
    g_s = _dot_split(gates_ref[...], emat_ref[...])
    a = (ocw_ref[...] + g_s * os_ref[...]).astype(BF16)
    o_ref[...] = x_ref[...] + _rms(_dot(a, w_ref[...]), gain_ref[...])


def _nsa_out_proj(x, o_cw, o_s, gates, emat, w, gain, tm=1024):
    t, d = x.shape
    tm = min(tm, t)
    row = pl.BlockSpec((tm, d), lambda i: (i, 0))
    return pl.pallas_call(
        _nsa_out_kernel,
        name="nsa_out",
        grid=(t // tm,),
        in_specs=[row, row, row,
                  pl.BlockSpec((tm, gates.shape[1]), lambda i: (i, 0)),
                  pl.BlockSpec(emat.shape, lambda i: (0, 0)),
                  pl.BlockSpec(w.shape, lambda i: (0, 0)),
                  pl.BlockSpec((1, d), lambda i: (0, 0))],
        out_specs=row,
        out_shape=jax.ShapeDtypeStruct((t, d), F32),
        compiler_params=_params("parallel"),
    )(x, o_cw, o_s, gates, emat, w, gain)


def _proj_out_kernel(x_ref, a_ref, w_ref, gain_ref, o_ref):
    o_ref[...] = x_ref[...] + _rms(_dot(a_ref[...], w_ref[...]), gain_ref[...])


def _proj_out(x, a, w, gain, tm=1024):
    t, d = x.shape
    tm = min(tm, t)
    row = pl.BlockSpec((tm, d), lambda i: (i, 0))
    return pl.pallas_call(
        _proj_out_kernel,
        name="proj_out",
        grid=(t // tm,),
        in_specs=[row, pl.BlockSpec((tm, a.shape[1]), lambda i: (i, 0)),
                  pl.BlockSpec(w.shape, lambda i: (0, 0)),
                  pl.BlockSpec((1, d), lambda i: (0, 0))],
        out_specs=row,
        out_shape=jax.ShapeDtypeStruct((t, d), F32),
        compiler_params=_params("parallel"),
    )(x, a, w, gain)


def _nsa_mixer(x, gain_pre, gain_post, w_in, w_out, cmp_pos, cmp_w1, cmp_w2):
    b, s, d = x.shape
    g, hd, hpg = NSA_GROUPS, NSA_HD, NSA_HPG
    n_gate = 3 * hpg
    q_cols = NSA_HEADS * hd
    kv_end = q_cols + 6 * NSA_KV_COLS

    gate_w = w_in[:, kv_end:].reshape(d, g, n_gate)
    gate_w = jnp.pad(gate_w, ((0, 0), (0, 0), (0, LANES - n_gate))).reshape(d, g * LANES)
    w_all = jnp.concatenate([w_in[:, :kv_end], gate_w], axis=1).astype(BF16)

    half = hd // 2
    inv = ROPE_THETA ** (-jnp.arange(half, dtype=F32) / half)
    ang = jnp.arange(s).astype(F32)[:, None] * inv[None, :]
    cos, sin = jnp.cos(ang), jnp.sin(ang)
    cos_t = jnp.tile(cos, (1, LANES // half))
    sin_t = jnp.tile(jnp.concatenate([-sin, sin], axis=1), (1, LANES // hd))

    q, k_c, v_c, k_s, v_s, k_w, v_w, gates = _nsa_in_proj(x, gain_pre, w_all, cos_t, sin_t)
    k_cmp = _compress(k_c, cmp_pos[0], cmp_w1[0], cmp_w2[0])
    v_cmp = _compress(v_c, cmp_pos[1], cmp_w1[1], cmp_w2[1])

    n_cmp_rows = s // CMP_STRIDE
    n_sel = s // SEL_BLOCK
    ratio = SEL_BLOCK // CMP_STRIDE
    c_id = jnp.arange(n_cmp_rows)[:, None]
    n_id = jnp.arange(n_sel)[None, :]
    lo = ratio * n_id - (CMP_BLOCK // CMP_STRIDE - 1)
    amat = ((c_id >= lo) & (c_id <= ratio * n_id + ratio - 1) & (c_id < n_cmp_rows - 1)).astype(BF16)

    o_cw, idx = _nsa_cmp_win(q, k_cmp, v_cmp, k_w, v_w, gates, amat)

    qt = q.reshape(b, g, hpg, s, hd).transpose(0, 1, 3, 2, 4)
    zero = jnp.zeros_like(qt)
    q2 = jnp.stack([jnp.concatenate([qt, zero], axis=-1), jnp.concatenate([zero, qt], axis=-1)], axis=3)
    q2 = q2.reshape(b, g, s * 2 * hpg, 2 * hd)
    kv2 = jnp.concatenate([k_s.reshape(b, g, s // 2, 2 * hd), v_s.reshape(b, g, s // 2, 2 * hd)], axis=-1)
    o_s = _nsa_selected(idx, q2, kv2)
    o_s = o_s[..., :hd].reshape(b, g, s, hpg * hd).transpose(0, 2, 1, 3).reshape(b * s, q_cols)

    r_id = jnp.arange(g * LANES)[:, None]
    col = jnp.arange(q_cols)[None, :]
    emat = (r_id == (col // (hpg * hd)) * LANES + 3 * ((col // hd) % hpg) + 1).astype(BF16)

    out = _nsa_out_proj(x.reshape(b * s, d), o_cw.reshape(b * s, q_cols), o_s,
                        gates.reshape(b * s, g * LANES), emat, w_out.astype(BF16), gain_post)
    return out.reshape(b, s, d)


def _hgrn_in_kernel(x_ref, gain_ref, w_ref, lb_ref, q_ref, k_ref, lf_ref, v_ref, gs_ref):
    xn = _rms(x_ref[...], gain_ref[...]).astype(BF16)
    n_chunk = D_MODEL // 256

    def chunk(c):
        return _dot(xn, w_ref[:, 256 * c:256 * (c + 1)])

    for c in range(n_chunk):
        cols = slice(256 * c, 256 * (c + 1))
        r = chunk(c)
        q_ref[:, cols] = r * jax.nn.sigmoid(r)
        f = chunk(n_chunk + c)
        lb = lb_ref[:, cols]
        log_sig = jnp.minimum(f, 0.0) - jnp.log1p(jnp.exp(-jnp.abs(f)))
        a = jnp.log(lb)
        bb = jnp.log1p(-lb) + log_sig
        log_f = jnp.maximum(a, bb) + jnp.log1p(jnp.exp(-jnp.abs(a - bb)))
        lf_ref[:, cols] = log_f
        k_ref[:, cols] = 1.0 - jnp.exp(log_f)
        v_ref[:, cols] = chunk(2 * n_chunk + c)
        r = chunk(3 * n_chunk + c)
        gs_ref[:, cols] = r * jax.nn.sigmoid(r)


def _hgrn_in_proj(x, gain, w, lb, tm=512):
    t, d = x.shape
    tm = min(tm, t)
    row = pl.BlockSpec((tm, d), lambda i: (i, 0))
    vec = pl.BlockSpec((1, d), lambda i: (0, 0))
    return pl.pallas_call(
        _hgrn_in_kernel,
        name="hgrn_in",
        grid=(t // tm,),
        in_specs=[row, vec, pl.BlockSpec(w.shape, lambda i: (0, 0)), vec],
        out_specs=[row] * 5,
        out_shape=[jax.ShapeDtypeStruct((t, d), F32)] * 5,
        compiler_params=_params("parallel"),
    )(x, gain, w, lb)


def _hgrn_scan_kernel(q_ref, k_ref, lf_ref, v_ref, gs_ref, gain_ref, o_ref, state_ref, *, rows):
    sub = HGRN_SUB

    @pl.when(pl.program_id(2) == 0)
    def _():
        state_ref[...] = jnp.zeros_like(state_ref)

    lf = lf_ref[...]
    r_id = lax.broadcasted_iota(jnp.int32, (rows, rows), 0)
    c_id = lax.broadcasted_iota(jnp.int32, (rows, rows), 1)
    same = (r_id // sub) == (c_id // sub)
    tri = jnp.where(same & (c_id <= r_id), 1.0, 0.0)
    box = jnp.where(same, 1.0, 0.0)
    cum = jnp.dot(tri, lf, preferred_element_type=F32, precision=lax.Precision.HIGHEST)
    tot = jnp.dot(box, lf, preferred_element_type=F32, precision=lax.Precision.HIGHEST)
    q = q_ref[...]
    k = k_ref[...]
    v = v_ref[...]
    q_dec = (q * jnp.exp(cum)).astype(BF16)
    k_dec = (k * jnp.exp(tot - cum)).astype(BF16)
    dec = jnp.exp(tot)
    ones = jnp.ones((HGRN_DK, LANES), BF16)
    row_id = lax.broadcasted_iota(jnp.int32, (sub, HGRN_DK), 0)

    state = state_ref[...]
    outs = []
    for i in range(rows // sub):
        sl = slice(sub * i, sub * (i + 1))
        qi, ki, vi, ci = q[sl], k[sl], v[sl], cum[sl]
        terms = []
        for j in range(sub):
            w = jnp.exp(jnp.where(row_id >= j, ci - ci[j:j + 1], NEG_INF))
            terms.append(qi * w * ki[j:j + 1])
        att = _dot(jnp.concatenate(terms, axis=0).astype(BF16), ones)
        o_i = _dot_nt(q_dec[sl], state.astype(BF16))
        for j in range(sub):
            o_i = o_i + att[sub * j:sub * (j + 1)] * vi[j:j + 1]
        outs.append(o_i)
        state = state * dec[sub * i:sub * i + 1] + _dot_tn(vi.astype(BF16), k_dec[sl])
    state_ref[...] = state

    o = jnp.concatenate(outs, axis=0)
    o_ref[...] = (_rms(o, gain_ref[...]) * gs_ref[...]).astype(o_ref.dtype)


def _hgrn_scan(q, k, lf, v, gs, gain, rows=128):
    b, s, d = q.shape
    rows = min(rows, s)
    blk = pl.BlockSpec((None, rows, LANES), lambda bi, h, l: (bi, l, h))
    return pl.pallas_call(
        functools.partial(_hgrn_scan_kernel, rows=rows),
        name="hgrn_scan",
        grid=(b, d // LANES, s // rows),
        in_specs=[blk] * 5 + [pl.BlockSpec((1, LANES), lambda bi, h, l: (0, 0))],
        out_specs=blk,
        out_shape=jax.ShapeDtypeStruct((b, s, d), BF16),
        scratch_shapes=[pltpu.VMEM((LANES, HGRN_DK), F32)],
        compiler_params=_params("parallel", "parallel", "arbitrary"),
    )(q, k, lf, v, gs, gain)


def _hgrn_mixer(x, gain_pre, gain_post, w_in, w_out, norm_gain, lower_bound):
    b, s, d = x.shape
    x2 = x.reshape(b * s, d)
    q, k, lf, v, gs = _hgrn_in_proj(x2, gain_pre, w_in.astype(BF16), lower_bound.reshape(1, d))
    shape = (b, s, d)
    o = _hgrn_scan(q.reshape(shape), k.reshape(shape), lf.reshape(shape), v.reshape(shape),
                   gs.reshape(shape), norm_gain.reshape(1, LANES))
    return _proj_out(x2, o.reshape(b * s, d), w_out.astype(BF16), gain_post).reshape(shape)


def kernel(x, p, norm_gains, ffn_w_in, ffn_w_out, ple_w_in, ple_w_gate, nsa_w_in, nsa_w_out, nsa_cmp_pos,
           nsa_cmp_w1, nsa_cmp_w2, hgrn_w_in, hgrn_w_out, hgrn_norm, hgrn_lb_logits):
    b, s, d = x.shape
    depth = norm_gains.shape[0]
    lb_sm = jax.nn.softmax(hgrn_lb_logits.astype(F32), axis=0)
    lower_bounds = jnp.cumsum(lb_sm, axis=0) - lb_sm[0]
    for layer in range(depth):
        ng = norm_gains[layer].reshape(8, 1, d)
        x2 = x.reshape(b * s, d)
        x2 = _ffn_block(x2, ng[0], ng[1], ffn_w_in[layer, 0].astype(BF16), ffn_w_out[layer, 0].astype(BF16))
        x = x2.reshape(b, s, d)
        j = layer // 2
        if layer % 2 == 0:
            x = _nsa_mixer(x, ng[2], ng[3], nsa_w_in[j], nsa_w_out[j], nsa_cmp_pos[j], nsa_cmp_w1[j],
                           nsa_cmp_w2[j])
        else:
            x = _hgrn_mixer(x, ng[2], ng[3], hgrn_w_in[j], hgrn_w_out[j], hgrn_norm[j], lower_bounds[layer])
        x2 = x.reshape(b * s, d)
        x2 = _ffn_block(x2, ng[4], ng[5], ffn_w_in[layer, 1].astype(BF16), ffn_w_out[layer, 1].astype(BF16))
        x2 = _ple_block(x2, p[layer].reshape(b * s, -1), ng[6], ng[7], ple_w_gate[layer].astype(BF16),
                        ple_w_in[layer].astype(BF16))
        x = x2.reshape(b, s, d)
    return x
```

```python
import functools

import jax
import jax.numpy as jnp
from jax import lax
from jax.experimental import pallas as pl
from jax.experimental.pallas import tpu as pltpu

F32 = jnp.float32
BF16 = jnp.bfloat16
NEG_INF = float("-inf")

D_MODEL = 1024
NORM_EPS = 1e-6
ROPE_THETA = 10000.0

NSA_HEADS = 16
NSA_GROUPS = 4
NSA_HPG = NSA_HEADS // NSA_GROUPS
NSA_HD = 64
NSA_KV_COLS = NSA_GROUPS * NSA_HD
CMP_BLOCK = 32
CMP_STRIDE = 16
SEL_BLOCK = 64
N_SELECT = 16
WINDOW = 512
Q_BLOCK = 128

HGRN_HEADS = 8
HGRN_DK = 128
HGRN_SUB = 16

LANES = 128
VMEM_LIMIT = 56 * 1024 * 1024


def _params(*sem):
    return pltpu.CompilerParams(dimension_semantics=sem, vmem_limit_bytes=VMEM_LIMIT)


def _rms(x, gain):
    return x * lax.rsqrt(jnp.mean(x * x, axis=-1, keepdims=True) + NORM_EPS) * gain


def _dot(a, b):
    return jnp.dot(a, b, preferred_element_type=F32)


def _dot_nt(a, b):
    return lax.dot_general(a, b, (((1,), (1,)), ((), ())), preferred_element_type=F32)


def _dot_tn(a, b):
    return lax.dot_general(a, b, (((0,), (0,)), ((), ())), preferred_element_type=F32)


def _dot_split(x, w):
    hi = x.astype(BF16)
    lo = (x - hi.astype(F32)).astype(BF16)
    return _dot(hi, w) + _dot(lo, w)


def _softmax_rows(s):
    m = jnp.max(s, axis=-1, keepdims=True)
    m = jnp.where(m == NEG_INF, 0.0, m)
    e = jnp.exp(s - m)
    return e / jnp.maximum(jnp.sum(e, axis=-1, keepdims=True), 1e-30)


def _ffn_kernel(x_ref, g0_ref, g1_ref, wg_ref, wu_ref, wo_ref, o_ref, xn_ref, acc_ref):
    j = pl.program_id(1)

    @pl.when(j == 0)
    def _():
        xn_ref[...] = _rms(x_ref[...], g0_ref[...]).astype(BF16)
        acc_ref[...] = jnp.zeros_like(acc_ref)

    xn = xn_ref[...]
    g = _dot(xn, wg_ref[...])
    u = _dot(xn, wu_ref[...])
    h = (g * jax.nn.sigmoid(g) * u).astype(BF16)
    acc_ref[...] += _dot(h, wo_ref[...])

    @pl.when(j == pl.num_programs(1) - 1)
    def _():
        o_ref[...] = x_ref[...] + 0.5 * _rms(acc_ref[...], g1_ref[...])


def _ffn_block(x, g_pre, g_post, w_in, w_out, tm=1024, tf=256):
    t, d = x.shape
    dff = w_out.shape[0]
    tm = min(tm, t)
    assert t % tm == 0
    nf = dff // tf
    row = pl.BlockSpec((tm, d), lambda i, j: (i, 0))
    vec = pl.BlockSpec((1, d), lambda i, j: (0, 0))
    return pl.pallas_call(
        _ffn_kernel,
        name="ffn",
        grid=(t // tm, nf),
        in_specs=[row, vec, vec,
                  pl.BlockSpec((d, tf), lambda i, j: (0, j)),
                  pl.BlockSpec((d, tf), lambda i, j: (0, j + nf)),
                  pl.BlockSpec((tf, d), lambda i, j: (j, 0))],
        out_specs=row,
        out_shape=jax.ShapeDtypeStruct((t, d), F32),
        scratch_shapes=[pltpu.VMEM((tm, d), BF16), pltpu.VMEM((tm, d), F32)],
        compiler_params=_params("parallel", "arbitrary"),
    )(x, g_pre, g_post, w_in, w_in, w_out)


def _ple_kernel(x_ref, p_ref, g0_ref, g1_ref, wg_ref, wp_ref, o_ref):
    x = x_ref[...]
    xn = _rms(x, g0_ref[...]).astype(BF16)
    gate = jax.nn.sigmoid(_dot(xn, wg_ref[...]))
    e = _dot(p_ref[...].astype(BF16), wp_ref[...])
    o_ref[...] = x + _rms(e * gate, g1_ref[...])


def _ple_block(x, p, g_pre, g_post, w_gate, w_p, tm=1024):
    t, d = x.shape
    pd = p.shape[1]
    tm = min(tm, t)
    assert t % tm == 0
    row = pl.BlockSpec((tm, d), lambda i: (i, 0))
    vec = pl.BlockSpec((1, d), lambda i: (0, 0))
    return pl.pallas_call(
        _ple_kernel,
        name="ple",
        grid=(t // tm,),
        in_specs=[row, pl.BlockSpec((tm, pd), lambda i: (i, 0)), vec, vec,
                  pl.BlockSpec((d, d), lambda i: (0, 0)),
                  pl.BlockSpec((pd, d), lambda i: (0, 0))],
        out_specs=row,
        out_shape=jax.ShapeDtypeStruct((t, d), F32),
        compiler_params=_params("parallel"),
    )(x, p, g_pre, g_post, w_gate, w_p)


def _rope_tile(xb, cos_t, sin_t, first_half):
    fwd = pltpu.roll(xb, LANES - NSA_HD // 2, 1)
    bwd = pltpu.roll(xb, NSA_HD // 2, 1)
    return xb * cos_t + jnp.where(first_half, fwd, bwd) * sin_t


def _nsa_in_kernel(x_ref, gain_ref, w_ref, cos_ref, sin_ref,
                   q_ref, kc_ref, vc_ref, ks_ref, vs_ref, kw_ref, vw_ref, gates_ref):
    xn = _rms(x_ref[...], gain_ref[...]).astype(BF16)
    cos_t = cos_ref[...]
    sin_t = sin_ref[...]
    tm = cos_t.shape[0]
    lane = lax.broadcasted_iota(jnp.int32, (tm, LANES), 1)
    first_half = (lane % NSA_HD) < (NSA_HD // 2)

    def chunk(c):
        return _dot(xn, w_ref[:, 256 * c:256 * (c + 1)])

    def roped(r):
        return [_rope_tile(r[:, LANES * h:LANES * (h + 1)], cos_t, sin_t, first_half) for h in range(2)]

    def store_heads(ref, base, tiles, scale):
        for h, tile in enumerate(tiles):
            ref[base + 2 * h] = (tile[:, :NSA_HD] * scale).astype(ref.dtype)
            ref[base + 2 * h + 1] = (tile[:, NSA_HD:] * scale).astype(ref.dtype)

    for c in range(4):
        store_heads(q_ref, 4 * c, roped(chunk(c)), NSA_HD ** -0.5)
    kc = roped(chunk(4))
    kc_ref[...] = jnp.concatenate(kc, axis=1)
    vc_ref[...] = chunk(5)
    store_heads(ks_ref, 0, roped(chunk(6)), 1.0)
    r = chunk(7)
    store_heads(vs_ref, 0, [r[:, :LANES], r[:, LANES:]], 1.0)
    store_heads(kw_ref, 0, roped(chunk(8)), 1.0)
    r = chunk(9)
    store_heads(vw_ref, 0, [r[:, :LANES], r[:, LANES:]], 1.0)
    gates_ref[:, :256] = jax.nn.sigmoid(chunk(10))
    gates_ref[:, 256:] = jax.nn.sigmoid(chunk(11))


def _nsa_in_proj(x, gain, w, cos_t, sin_t, tm=512):
    b, s, d = x.shape
    tm = min(tm, s)
    assert s % tm == 0
    g, hd = NSA_GROUPS, NSA_HD
    kv_heads = pl.BlockSpec((None, g, tm, hd), lambda bi, i: (bi, 0, i, 0))
    kv_flat = pl.BlockSpec((None, tm, g * hd), lambda bi, i: (bi, i, 0))
    tab = pl.BlockSpec((tm, LANES), lambda bi, i: (i, 0))
    return pl.pallas_call(
        _nsa_in_kernel,
        name="nsa_in",
        grid=(b, s // tm),
        in_specs=[pl.BlockSpec((None, tm, d), lambda bi, i: (bi, i, 0)),
                  pl.BlockSpec((1, d), lambda bi, i: (0, 0)),
                  pl.BlockSpec(w.shape, lambda bi, i: (0, 0)),
                  tab, tab],
        out_specs=[pl.BlockSpec((None, NSA_HEADS, tm, hd), lambda bi, i: (bi, 0, i, 0)),
                   kv_flat, kv_flat, kv_heads, kv_heads, kv_heads, kv_heads,
                   pl.BlockSpec((None, tm, 4 * LANES), lambda bi, i: (bi, i, 0))],
        out_shape=[jax.ShapeDtypeStruct((b, NSA_HEADS, s, hd), BF16),
                   jax.ShapeDtypeStruct((b, s, g * hd), F32),
                   jax.ShapeDtypeStruct((b, s, g * hd), F32),
                   jax.ShapeDtypeStruct((b, g, s, hd), BF16),
                   jax.ShapeDtypeStruct((b, g, s, hd), BF16),
                   jax.ShapeDtypeStruct((b, g, s, hd), BF16),
                   jax.ShapeDtypeStruct((b, g, s, hd), BF16),
                   jax.ShapeDtypeStruct((b, s, 4 * LANES), F32)],
        compiler_params=_params("parallel", "parallel"),
    )(x, gain, w, cos_t, sin_t)


def _cmp_proj_kernel(x_ref, plo_ref, phi_ref, a_ref, b_ref, ya_ref, yb_ref):
    x = x_ref[...]
    ya_ref[...] = _dot((x + plo_ref[...]).astype(BF16), a_ref[...])
    yb_ref[...] = _dot((x + phi_ref[...]).astype(BF16), b_ref[...])


def _cmp_mix_kernel(ya_ref, yb_ref, w2_ref, o_ref):
    n = ya_ref.shape[0]
    z = ya_ref[...] + pltpu.roll(yb_ref[...], n - 1, 0)
    h = (z * jax.nn.sigmoid(z)).astype(BF16)
    out = _dot(h, w2_ref[...])
    for g in range(NSA_GROUPS):
        o_ref[g] = out[:, NSA_HD * g:NSA_HD * (g + 1)].astype(o_ref.dtype)


def _compress(kv, pos, w1, w2, tc=256):
    b, s, _ = kv.shape
    g, hd = NSA_GROUPS, NSA_HD
    n = s // CMP_STRIDE
    wide = CMP_STRIDE * g * hd
    tc = min(tc, n)
    x = kv.reshape(b, n, wide)
    eye = jnp.eye(g, dtype=F32)
    w1r = w1.reshape(CMP_BLOCK, hd, hd)
    mat_a = jnp.einsum("pde,gh->pgdhe", w1r[:CMP_STRIDE], eye).reshape(wide, g * hd).astype(BF16)
    mat_b = jnp.einsum("pde,gh->pgdhe", w1r[CMP_STRIDE:], eye).reshape(wide, g * hd).astype(BF16)
    p_lo = jnp.broadcast_to(pos[:CMP_STRIDE, None, :], (CMP_STRIDE, g, hd)).reshape(1, wide)
    p_hi = jnp.broadcast_to(pos[CMP_STRIDE:, None, :], (CMP_STRIDE, g, hd)).reshape(1, wide)
    w2bd = jnp.einsum("de,gh->gdhe", w2, eye).reshape(g * hd, g * hd).astype(BF16)
    vec = pl.BlockSpec((1, wide), lambda bi, i: (0, 0))
    mat = pl.BlockSpec((wide, g * hd), lambda bi, i: (0, 0))
    yspec = pl.BlockSpec((None, tc, g * hd), lambda bi, i: (bi, i, 0))
    ya, yb = pl.pallas_call(
        _cmp_proj_kernel,
        name="cmp_proj",
        grid=(b, n // tc),
        in_specs=[pl.BlockSpec((None, tc, wide), lambda bi, i: (bi, i, 0)), vec, vec, mat, mat],
        out_specs=[yspec, yspec],
        out_shape=[jax.ShapeDtypeStruct((b, n, g * hd), F32)] * 2,
        compiler_params=_params("parallel", "parallel"),
    )(x, p_lo, p_hi, mat_a, mat_b)
    full = pl.BlockSpec((None, n, g * hd), lambda bi: (bi, 0, 0))
    return pl.pallas_call(
        _cmp_mix_kernel,
        name="cmp_mix",
        grid=(b,),
        in_specs=[full, full, pl.BlockSpec((g * hd, g * hd), lambda bi: (0, 0))],
        out_specs=pl.BlockSpec((None, g, n, hd), lambda bi: (bi, 0, 0, 0)),
        out_shape=jax.ShapeDtypeStruct((b, g, n, hd), BF16),
        compiler_params=_params("parallel"),
    )(ya, yb, w2bd)


def _nsa_cw_kernel(q_ref, kc_ref, vc_ref, kw_ref, vw_ref, gates_ref, amat_ref, ocw_ref, idx_ref):
    qb = Q_BLOCK
    rows = NSA_HPG * qb
    t0 = pl.program_id(2) * qb
    q = q_ref[...].reshape(rows, NSA_HD)
    n_cmp = kc_ref.shape[0]
    n_sel = amat_ref.shape[0]
    t_row = t0 + lax.broadcasted_iota(jnp.int32, (rows, 1), 0) % qb

    s = _dot_nt(q, kc_ref[...])
    cmp_end = lax.broadcasted_iota(jnp.int32, (1, n_cmp), 1) * CMP_STRIDE + (CMP_BLOCK - 1)
    p = _softmax_rows(jnp.where(cmp_end <= t_row, s, NEG_INF))
    o_c = _dot(p.astype(BF16), vc_ref[...])
    p_grp = p[0:qb] + p[qb:2 * qb] + p[2 * qb:3 * qb] + p[3 * qb:4 * qb]
    p_hi = p_grp.astype(BF16)
    p_lo = (p_grp - p_hi.astype(F32)).astype(BF16)
    imp = _dot_nt(amat_ref[...], p_hi) + _dot_nt(amat_ref[...], p_lo)

    t_tok = t0 + lax.broadcasted_iota(jnp.int32, (1, qb), 1)
    cur = (t_tok // SEL_BLOCK).astype(F32)
    blk = lax.broadcasted_iota(jnp.int32, (n_sel, 1), 0).astype(F32)
    forced = (blk == 0.0) | (blk == cur - 1.0)
    score = jnp.where(forced, jnp.inf, jnp.where(blk < cur, imp, NEG_INF))
    score = jnp.where(blk == cur, NEG_INF, score)
    slot = lax.broadcasted_iota(jnp.int32, (N_SELECT, qb), 0)
    sel = jnp.where(slot == 0, cur, 0.0)
    for n in range(1, N_SELECT):
        m = jnp.max(score, axis=0, keepdims=True)
        pick = jnp.min(jnp.where(score == m, blk, float(n_sel)), axis=0, keepdims=True)
        score = jnp.where(blk == pick, NEG_INF, score)
        sel = jnp.where(slot == n, pick, sel)
    idx_ref[...] = sel.astype(jnp.int32)

    w_len = WINDOW + qb
    w_start = pl.multiple_of(jnp.maximum(t0 - WINDOW, 0), qb)
    s_w = _dot_nt(q, kw_ref[pl.ds(w_start, w_len), :])
    kpos = w_start + lax.broadcasted_iota(jnp.int32, (1, w_len), 1)
    mask_w = (kpos <= t_row) & (kpos > t_row - WINDOW)
    p_w = _softmax_rows(jnp.where(mask_w, s_w, NEG_INF))
    o_w = _dot(p_w.astype(BF16), vw_ref[pl.ds(w_start, w_len), :])

    gates = gates_ref[...]
    outs = []
    for h in range(NSA_HPG):
        g_c = gates[:, 3 * h:3 * h + 1]
        g_w = gates[:, 3 * h + 2:3 * h + 3]
        outs.append(g_c * o_c[h * qb:(h + 1) * qb] + g_w * o_w[h * qb:(h + 1) * qb])
    ocw_ref[...] = jnp.concatenate(outs, axis=1)


def _nsa_cmp_win(q, k_cmp, v_cmp, k_w, v_w, gates, amat):
    b, _, s, hd = q.shape
    g = NSA_GROUPS
    n_cmp = k_cmp.shape[2]
    cmp_spec = pl.BlockSpec((None, None, n_cmp, hd), lambda bi, gi, j: (bi, gi, 0, 0))
    win_spec = pl.BlockSpec((None, None, s, hd), lambda bi, gi, j: (bi, gi, 0, 0))
    return pl.pallas_call(
        _nsa_cw_kernel,
        name="nsa_cw",
        grid=(b, g, s // Q_BLOCK),
        in_specs=[pl.BlockSpec((None, NSA_HPG, Q_BLOCK, hd), lambda bi, gi, j: (bi, gi, j, 0)),
                  cmp_spec, cmp_spec, win_spec, win_spec,
                  pl.BlockSpec((None, Q_BLOCK, LANES), lambda bi, gi, j: (bi, j, gi)),
                  pl.BlockSpec(amat.shape, lambda bi, gi, j: (0, 0))],
        out_specs=[pl.BlockSpec((None, Q_BLOCK, NSA_HPG * hd), lambda bi, gi, j: (bi, j, gi)),
                   pl.BlockSpec((None, None, None, N_SELECT, Q_BLOCK), lambda bi, gi, j: (bi, gi, j, 0, 0))],
        out_shape=[jax.ShapeDtypeStruct((b, s, NSA_HEADS * hd), F32),
                   jax.ShapeDtypeStruct((b, g, s // Q_BLOCK, N_SELECT, Q_BLOCK), jnp.int32)],
        compiler_params=_params("parallel", "parallel", "arbitrary"),
    )(q, k_cmp, v_cmp, k_w, v_w, gates, amat)


def _roll_rows(x, shift):
    return jnp.concatenate([pltpu.roll(x[r:r + 8], shift, 0) for r in range(0, x.shape[0], 8)], axis=0)


def _nsa_sel_kernel(idx_ref, q2_ref, kv2_ref, o_ref, s_a, s_b, v_a, v_b, *, tq, pairs):
    half_blk = SEL_BLOCK // 2
    n_col = N_SELECT * half_blk
    t0 = pl.program_id(2) * tq
    row = lax.broadcasted_iota(jnp.int32, (16, n_col), 0)
    col = lax.broadcasted_iota(jnp.int32, (16, n_col), 1)
    pos0 = 2 * col + (row % 8) // NSA_HPG
    in_slot0 = col < half_blk
    second = (row >= 8).astype(jnp.int32)
    row8 = lax.broadcasted_iota(jnp.int32, (8, LANES), 0)

    def stage_scores(grp, s_buf, v_buf):
        for u in range(pairs):
            tp = grp * pairs + u
            q16 = q2_ref[pl.ds(pl.multiple_of(16 * tp, 16), 16), :]
            halves = []
            for w in range(2):
                tok = 2 * tp + w
                base = (tok // Q_BLOCK) * (N_SELECT * Q_BLOCK) + tok % Q_BLOCK
                blocks = []
                for n in range(N_SELECT):
                    off = pl.multiple_of(idx_ref[0, base + n * Q_BLOCK] * half_blk, half_blk)
                    blocks.append(kv2_ref[pl.ds(off, half_blk), :])
                keys = jnp.concatenate([blk[:, :LANES] for blk in blocks], axis=0)
                v_buf[(2 * u + w) * n_col:(2 * u + w + 1) * n_col, :] = jnp.concatenate(
                    [blk[:, LANES:] for blk in blocks], axis=0)
                halves.append(_dot_nt(q16, keys)[8 * w:8 * w + 8])
            s_buf[16 * u:16 * u + 16, :] = jnp.concatenate(halves, axis=0)

    def stage_output(grp, s_buf, v_buf):
        for u in range(pairs):
            tp = grp * pairs + u
            ta = t0 + 2 * tp
            n_valid = jnp.minimum(ta // SEL_BLOCK + 1, N_SELECT)
            ok = jnp.where(in_slot0, pos0 - (ta % SEL_BLOCK + second), col - n_valid * half_blk + 1) <= 0
            s = jnp.where(ok, s_buf[16 * u:16 * u + 16, :], NEG_INF)
            m = jnp.max(s, axis=-1, keepdims=True)
            m = jnp.maximum(m, _roll_rows(m, NSA_HPG))
            e = jnp.exp(s - m)
            l = jnp.sum(e, axis=-1, keepdims=True)
            l = l + _roll_rows(l, NSA_HPG)
            p = (e / l).astype(BF16)
            xa = _dot(p, v_buf[(2 * u) * n_col:(2 * u + 1) * n_col, :])
            xb = _dot(p, v_buf[(2 * u + 1) * n_col:(2 * u + 2) * n_col, :])
            x = jnp.concatenate([xa[:8], xb[8:]], axis=0)
            y = x + pltpu.roll(_roll_rows(x, NSA_HPG), NSA_HD, 1)
            out = jnp.where(row8 < NSA_HPG, y[:8], pltpu.roll(y[8:], NSA_HPG, 0))
            o_ref[pl.ds(pl.multiple_of(8 * tp, 8), 8), :] = out

    n_grp = tq // (2 * pairs)
    stage_scores(0, s_a, v_a)

    def body(k, carry):
        stage_scores(2 * k + 1, s_b, v_b)
        stage_output(2 * k, s_a, v_a)
        stage_scores(jnp.minimum(2 * k + 2, n_grp - 1), s_a, v_a)
        stage_output(2 * k + 1, s_b, v_b)
        return carry

    lax.fori_loop(0, n_grp // 2, body, 0)


def _nsa_selected(idx, q2, kv2, tq=256, pairs=8):
    b, g = idx.shape[:2]
    s = kv2.shape[2] * 2
    tq = min(tq, s)
    nt = s // tq
    assert (tq // (2 * pairs)) % 2 == 0 and tq % Q_BLOCK == 0
    idx_flat = idx.reshape(b * g * nt, 1, tq * N_SELECT)
    n_col = N_SELECT * SEL_BLOCK // 2
    scratch = [pltpu.VMEM((pairs * 16, n_col), F32)] * 2 + [pltpu.VMEM((pairs * 2 * n_col, LANES), BF16)] * 2
    return pl.pallas_call(
        functools.partial(_nsa_sel_kernel, tq=tq, pairs=pairs),
        scratch_shapes=scratch,
        name="nsa_sel",
        grid=(b, g, nt),
        in_specs=[pl.BlockSpec((None, 1, tq * N_SELECT), lambda bi, gi, j: ((bi * g + gi) * nt + j, 0, 0),
                               memory_space=pltpu.SMEM),
                  pl.BlockSpec((None, None, tq * 8, LANES), lambda bi, gi, j: (bi, gi, j, 0)),
                  pl.BlockSpec((None, None, s // 2, 2 * LANES), lambda bi, gi, j: (bi, gi, 0, 0))],
        out_specs=pl.BlockSpec((None, None, tq * NSA_HPG, LANES), lambda bi, gi, j: (bi, gi, j, 0)),
        out_shape=jax.ShapeDtypeStruct((b, g, s * NSA_HPG, LANES), F32),
        compiler_params=_params("parallel", "parallel", "arbitrary"),
    )(idx_flat, q2, kv2)


def _nsa_out_kernel(x_ref, ocw_ref, os_ref, gates_ref, emat_ref, w_ref, gain_ref, o_ref):
    g_s = _dot_split(gates_ref[...], emat_ref[...])
    a = (ocw_ref[...] + g_s * os_ref[...]).astype(BF16)
    o_ref[...] = x_ref[...] + _rms(_dot(a, w_ref[...]), gain_ref[...])


def _nsa_out_proj(x, o_cw, o_s, gates, emat, w, gain, tm=1024):
    t, d = x.shape
    tm = min(tm, t)
    assert t % tm == 0
    row = pl.BlockSpec((tm, d), lambda i: (i, 0))
    return pl.pallas_call(
        _nsa_out_kernel,
        name="nsa_out",
        grid=(t // tm,),
        in_specs=[row, row, row,
                  pl.BlockSpec((tm, gates.shape[1]), lambda i: (i, 0)),
                  pl.BlockSpec(emat.shape, lambda i: (0, 0)),
                  pl.BlockSpec(w.shape, lambda i: (0, 0)),
                  pl.BlockSpec((1, d), lambda i: (0, 0))],
        out_specs=row,
        out_shape=jax.ShapeDtypeStruct((t, d), F32),
        compiler_params=_params("parallel"),
    )(x, o_cw, o_s, gates, emat, w, gain)


def _proj_out_kernel(x_ref, a_ref, w_ref, gain_ref, o_ref):
    o_ref[...] = x_ref[...] + _rms(_dot(a_ref[...], w_ref[...]), gain_ref[...])


def _proj_out(x, a, w, gain, tm=1024):
    t, d = x.shape
    tm = min(tm, t)
    assert t % tm == 0
    row = pl.BlockSpec((tm, d), lambda i: (i, 0))
    return pl.pallas_call(
        _proj_out_kernel,
        name="proj_out",
        grid=(t // tm,),
        in_specs=[row, pl.BlockSpec((tm, a.shape[1]), lambda i: (i, 0)),
                  pl.BlockSpec(w.shape, lambda i: (0, 0)),
                  pl.BlockSpec((1, d), lambda i: (0, 0))],
        out_specs=row,
        out_shape=jax.ShapeDtypeStruct((t, d), F32),
        compiler_params=_params("parallel"),
    )(x, a, w, gain)


def _nsa_mixer(x, gain_pre, gain_post, w_in, w_out, cmp_pos, cmp_w1, cmp_w2):
    b, s, d = x.shape
    g, hd, hpg = NSA_GROUPS, NSA_HD, NSA_HPG
    n_gate = 3 * hpg
    q_cols = NSA_HEADS * hd
    kv_end = q_cols + 6 * NSA_KV_COLS

    gate_w = w_in[:, kv_end:].reshape(d, g, n_gate)
    gate_w = jnp.pad(gate_w, ((0, 0), (0, 0), (0, LANES - n_gate))).reshape(d, g * LANES)
    w_all = jnp.concatenate([w_in[:, :kv_end], gate_w], axis=1).astype(BF16)

    half = hd // 2
    inv = ROPE_THETA ** (-jnp.arange(half, dtype=F32) / half)
    ang = jnp.arange(s).astype(F32)[:, None] * inv[None, :]
    cos, sin = jnp.cos(ang), jnp.sin(ang)
    cos_t = jnp.tile(cos, (1, LANES // half))
    sin_t = jnp.tile(jnp.concatenate([-sin, sin], axis=1), (1, LANES // hd))

    q, k_c, v_c, k_s, v_s, k_w, v_w, gates = _nsa_in_proj(x, gain_pre, w_all, cos_t, sin_t)
    k_cmp = _compress(k_c, cmp_pos[0], cmp_w1[0], cmp_w2[0])
    v_cmp = _compress(v_c, cmp_pos[1], cmp_w1[1], cmp_w2[1])

    n_cmp_rows = s // CMP_STRIDE
    n_sel = s // SEL_BLOCK
    ratio = SEL_BLOCK // CMP_STRIDE
    c_id = jnp.arange(n_cmp_rows)[:, None]
    n_id = jnp.arange(n_sel)[None, :]
    lo = ratio * n_id - (CMP_BLOCK // CMP_STRIDE - 1)
    amat = ((c_id >= lo) & (c_id <= ratio * n_id + ratio - 1) & (c_id < n_cmp_rows - 1)).astype(BF16)

    o_cw, idx = _nsa_cmp_win(q, k_cmp, v_cmp, k_w, v_w, gates, amat.T)

    qt = q.reshape(b, g, hpg, s, hd).transpose(0, 1, 3, 2, 4)
    zero = jnp.zeros_like(qt)
    q2 = jnp.stack([jnp.concatenate([qt, zero], axis=-1), jnp.concatenate([zero, qt], axis=-1)], axis=3)
    q2 = q2.reshape(b, g, s * 2 * hpg, 2 * hd)
    kv2 = jnp.concatenate([k_s.reshape(b, g, s // 2, 2 * hd), v_s.reshape(b, g, s // 2, 2 * hd)], axis=-1)
    o_s = _nsa_selected(idx, q2, kv2)
    o_s = o_s[..., :hd].reshape(b, g, s, hpg * hd).transpose(0, 2, 1, 3).reshape(b * s, q_cols)

    r_id = jnp.arange(g * LANES)[:, None]
    col = jnp.arange(q_cols)[None, :]
    emat = (r_id == (col // (hpg * hd)) * LANES + 3 * ((col // hd) % hpg) + 1).astype(BF16)

    out = _nsa_out_proj(x.reshape(b * s, d), o_cw.reshape(b * s, q_cols), o_s,
                        gates.reshape(b * s, g * LANES), emat, w_out.astype(BF16), gain_post)
    return out.reshape(b, s, d)


def _hgrn_in_kernel(x_ref, gain_ref, w_ref, lb_ref, q_ref, k_ref, lf_ref, v_ref, gs_ref):
    xn = _rms(x_ref[...], gain_ref[...]).astype(BF16)
    n_chunk = D_MODEL // 256

    def chunk(c):
        return _dot(xn, w_ref[:, 256 * c:256 * (c + 1)])

    for c in range(n_chunk):
        cols = slice(256 * c, 256 * (c + 1))
        r = chunk(c)
        q_ref[:, cols] = r * jax.nn.sigmoid(r)
        f = chunk(n_chunk + c)
        lb = lb_ref[:, cols]
        log_sig = jnp.minimum(f, 0.0) - jnp.log1p(jnp.exp(-jnp.abs(f)))
        a = jnp.log(lb)
        bb = jnp.log1p(-lb) + log_sig
        log_f = jnp.maximum(a, bb) + jnp.log1p(jnp.exp(-jnp.abs(a - bb)))
        lf_ref[:, cols] = log_f
        k_ref[:, cols] = 1.0 - jnp.exp(log_f)
        v_ref[:, cols] = chunk(2 * n_chunk + c)
        r = chunk(3 * n_chunk + c)
        gs_ref[:, cols] = r * jax.nn.sigmoid(r)


def _hgrn_in_proj(x, gain, w, lb, tm=512):
    t, d = x.shape
    tm = min(tm, t)
    assert t % tm == 0
    row = pl.BlockSpec((tm, d), lambda i: (i, 0))
    vec = pl.BlockSpec((1, d), lambda i: (0, 0))
    return pl.pallas_call(
        _hgrn_in_kernel,
        name="hgrn_in",
        grid=(t // tm,),
        in_specs=[row, vec, pl.BlockSpec(w.shape, lambda i: (0, 0)), vec],
        out_specs=[row] * 5,
        out_shape=[jax.ShapeDtypeStruct((t, d), F32)] * 5,
        compiler_params=_params("parallel"),
    )(x, gain, w, lb)


def _hgrn_scan_kernel(q_ref, k_ref, lf_ref, v_ref, gs_ref, gain_ref, o_ref, state_ref, *, rows):
    sub = HGRN_SUB

    @pl.when(pl.program_id(2) == 0)
    def _():
        state_ref[...] = jnp.zeros_like(state_ref)

    lf = lf_ref[...]
    r_id = lax.broadcasted_iota(jnp.int32, (rows, rows), 0)
    c_id = lax.broadcasted_iota(jnp.int32, (rows, rows), 1)
    same = (r_id // sub) == (c_id // sub)
    tri = jnp.where(same & (c_id <= r_id), 1.0, 0.0)
    cum = jnp.dot(tri, lf, preferred_element_type=F32, precision=lax.Precision.HIGHEST)
    q = q_ref[...]
    k = k_ref[...]
    v = v_ref[...]
    q_dec = (q * jnp.exp(cum)).astype(BF16)
    ones = jnp.ones((HGRN_DK, LANES), BF16)
    row_id = lax.broadcasted_iota(jnp.int32, (sub, HGRN_DK), 0)

    intra, update, decay = [], [], []
    for i in range(rows // sub):
        sl = slice(sub * i, sub * (i + 1))
        qi, ki, vi, ci = q[sl], k[sl], v[sl], cum[sl]
        total = ci[sub - 1:sub]
        terms = []
        for j in range(sub):
            w = jnp.exp(jnp.where(row_id >= j, ci - ci[j:j + 1], NEG_INF))
            terms.append(qi * w * ki[j:j + 1])
        att = _dot(jnp.concatenate(terms, axis=0).astype(BF16), ones)
        o_i = att[0:sub] * vi[0:1]
        for j in range(1, sub):
            o_i = o_i + att[sub * j:sub * (j + 1)] * vi[j:j + 1]
        intra.append(o_i)
        update.append(_dot_tn(vi.astype(BF16), (ki * jnp.exp(total - ci)).astype(BF16)))
        decay.append(jnp.exp(total))

    state = state_ref[...]
    outs = []
    for i in range(rows // sub):
        outs.append(intra[i] + _dot_nt(q_dec[sub * i:sub * (i + 1)], state.astype(BF16)))
        state = state * decay[i] + update[i]
    state_ref[...] = state

    o = jnp.concatenate(outs, axis=0)
    o_ref[...] = (_rms(o, gain_ref[...]) * gs_ref[...]).astype(o_ref.dtype)


def _hgrn_scan(q, k, lf, v, gs, gain, rows=128):
    b, s, d = q.shape
    rows = min(rows, s)
    blk = pl.BlockSpec((None, rows, LANES), lambda bi, h, l: (bi, l, h))
    return pl.pallas_call(
        functools.partial(_hgrn_scan_kernel, rows=rows),
        name="hgrn_scan",
        grid=(b, d // LANES, s // rows),
        in_specs=[blk] * 5 + [pl.BlockSpec((1, LANES), lambda bi, h, l: (0, 0))],
        out_specs=blk,
        out_shape=jax.ShapeDtypeStruct((b, s, d), BF16),
        scratch_shapes=[pltpu.VMEM((LANES, HGRN_DK), F32)],
        compiler_params=_params("parallel", "parallel", "arbitrary"),
    )(q, k, lf, v, gs, gain)


def _hgrn_mixer(x, gain_pre, gain_post, w_in, w_out, norm_gain, lower_bound):
    b, s, d = x.shape
    x2 = x.reshape(b * s, d)
    q, k, lf, v, gs = _hgrn_in_proj(x2, gain_pre, w_in.astype(BF16), lower_bound.reshape(1, d))
    shape = (b, s, d)
    o = _hgrn_scan(q.reshape(shape), k.reshape(shape), lf.reshape(shape), v.reshape(shape),
                   gs.reshape(shape), norm_gain.reshape(1, LANES))
    return _proj_out(x2, o.reshape(b * s, d), w_out.astype(BF16), gain_post).reshape(shape)


def kernel(x, p, norm_gains, ffn_w_in, ffn_w_out, ple_w_in, ple_w_gate, nsa_w_in, nsa_w_out, nsa_cmp_pos,
           nsa_cmp_w1, nsa_cmp_w2, hgrn_w_in, hgrn_w_out, hgrn_norm, hgrn_lb_logits):
    b, s, d = x.shape
    depth = norm_gains.shape[0]
    lb_sm = jax.nn.softmax(hgrn_lb_logits.astype(F32), axis=0)
    lower_bounds = jnp.cumsum(lb_sm, axis=0) - lb_sm[0]
    for layer in range(depth):
        ng = norm_gains[layer].reshape(8, 1, d)
        x2 = x.reshape(b * s, d)
        x2 = _ffn_block(x2, ng[0], ng[1], ffn_w_in[layer, 0].astype(BF16), ffn_w_out[layer, 0].astype(BF16))
        x = x2.reshape(b, s, d)
        j = layer // 2
        if layer % 2 == 0:
            x = _nsa_mixer(x, ng[2], ng[3], nsa_w_in[j], nsa_w_out[j], nsa_cmp_pos[j], nsa_cmp_w1[j],
                           nsa_cmp_w2[j])
        else:
            x = _hgrn_mixer(x, ng[2], ng[3], hgrn_w_in[j], hgrn_w_out[j], hgrn_norm[j], lower_bounds[layer])
        x2 = x.reshape(b * s, d)
        x2 = _ffn_block(x2, ng[4], ng[5], ffn_w_in[layer, 1].astype(BF16), ffn_w_out[layer, 1].astype(BF16))
        x2 = _ple_block(x2, p[layer].reshape(b * s, -1), ng[6], ng[7], ple_w_gate[layer].astype(BF16),
                        ple_w_in[layer].astype(BF16))
        x = x2.reshape(b, s, d)
    return x
```

```python
import functools

import jax
import jax.numpy as jnp
from jax import lax
from jax.experimental import pallas as pl
from jax.experimental.pallas import tpu as pltpu

F32 = jnp.float32
BF16 = jnp.bfloat16
NEG_INF = float("-inf")

D_MODEL = 1024
NORM_EPS = 1e-6
ROPE_THETA = 10000.0

NSA_HEADS = 16
NSA_GROUPS = 4
NSA_HPG = NSA_HEADS // NSA_GROUPS
NSA_HD = 64
NSA_KV_COLS = NSA_GROUPS * NSA_HD
CMP_BLOCK = 32
CMP_STRIDE = 16
SEL_BLOCK = 64
N_SELECT = 16
WINDOW = 512
Q_BLOCK = 128

HGRN_HEADS = 8
HGRN_DK = 128
HGRN_SUB = 16

LANES = 128
VMEM_LIMIT = 56 * 1024 * 1024


def _params(*sem):
    return pltpu.CompilerParams(dimension_semantics=sem, vmem_limit_bytes=VMEM_LIMIT)


def _rms(x, gain):
    return x * lax.rsqrt(jnp.mean(x * x, axis=-1, keepdims=True) + NORM_EPS) * gain


def _dot(a, b):
    return jnp.dot(a, b, preferred_element_type=F32)


def _dot_nt(a, b):
    return lax.dot_general(a, b, (((1,), (1,)), ((), ())), preferred_element_type=F32)


def _dot_tn(a, b):
    return lax.dot_general(a, b, (((0,), (0,)), ((), ())), preferred_element_type=F32)


def _dot_split(x, w):
    hi = x.astype(BF16)
    lo = (x - hi.astype(F32)).astype(BF16)
    return _dot(hi, w) + _dot(lo, w)


def _softmax_rows(s):
    m = jnp.max(s, axis=-1, keepdims=True)
    m = jnp.where(m == NEG_INF, 0.0, m)
    e = jnp.exp(s - m)
    return e / jnp.maximum(jnp.sum(e, axis=-1, keepdims=True), 1e-30)


def _ffn_kernel(x_ref, g0_ref, g1_ref, wg_ref, wu_ref, wo_ref, o_ref, xn_ref, acc_ref):
    j = pl.program_id(1)

    @pl.when(j == 0)
    def _():
        xn_ref[...] = _rms(x_ref[...], g0_ref[...]).astype(BF16)
        acc_ref[...] = jnp.zeros_like(acc_ref)

    xn = xn_ref[...]
    g = _dot(xn, wg_ref[...])
    u = _dot(xn, wu_ref[...])
    h = (g * jax.nn.sigmoid(g) * u).astype(BF16)
    acc_ref[...] += _dot(h, wo_ref[...])

    @pl.when(j == pl.num_programs(1) - 1)
    def _():
        o_ref[...] = x_ref[...] + 0.5 * _rms(acc_ref[...], g1_ref[...])


def _ffn_block(x, g_pre, g_post, w_in, w_out, tm=1024, tf=256):
    t, d = x.shape
    dff = w_out.shape[0]
    tm = min(tm, t)
    assert t % tm == 0
    nf = dff // tf
    row = pl.BlockSpec((tm, d), lambda i, j: (i, 0))
    vec = pl.BlockSpec((1, d), lambda i, j: (0, 0))
    return pl.pallas_call(
        _ffn_kernel,
        name="ffn",
        grid=(t // tm, nf),
        in_specs=[row, vec, vec,
                  pl.BlockSpec((d, tf), lambda i, j: (0, j)),
                  pl.BlockSpec((d, tf), lambda i, j: (0, j + nf)),
                  pl.BlockSpec((tf, d), lambda i, j: (j, 0))],
        out_specs=row,
        out_shape=jax.ShapeDtypeStruct((t, d), F32),
        scratch_shapes=[pltpu.VMEM((tm, d), BF16), pltpu.VMEM((tm, d), F32)],
        compiler_params=_params("parallel", "arbitrary"),
    )(x, g_pre, g_post, w_in, w_in, w_out)


def _ple_kernel(x_ref, p_ref, g0_ref, g1_ref, wg_ref, wp_ref, o_ref):
    x = x_ref[...]
    xn = _rms(x, g0_ref[...]).astype(BF16)
    gate = jax.nn.sigmoid(_dot(xn, wg_ref[...]))
    e = _dot(p_ref[...].astype(BF16), wp_ref[...])
    o_ref[...] = x + _rms(e * gate, g1_ref[...])


def _ple_block(x, p, g_pre, g_post, w_gate, w_p, tm=1024):
    t, d = x.shape
    pd = p.shape[1]
    tm = min(tm, t)
    assert t % tm == 0
    row = pl.BlockSpec((tm, d), lambda i: (i, 0))
    vec = pl.BlockSpec((1, d), lambda i: (0, 0))
    return pl.pallas_call(
        _ple_kernel,
        name="ple",
        grid=(t // tm,),
        in_specs=[row, pl.BlockSpec((tm, pd), lambda i: (i, 0)), vec, vec,
                  pl.BlockSpec((d, d), lambda i: (0, 0)),
                  pl.BlockSpec((pd, d), lambda i: (0, 0))],
        out_specs=row,
        out_shape=jax.ShapeDtypeStruct((t, d), F32),
        compiler_params=_params("parallel"),
    )(x, p, g_pre, g_post, w_gate, w_p)


def _rope_tile(xb, cos_t, sin_t, first_half):
    fwd = pltpu.roll(xb, LANES - NSA_HD // 2, 1)
    bwd = pltpu.roll(xb, NSA_HD // 2, 1)
    return xb * cos_t + jnp.where(first_half, fwd, bwd) * sin_t


def _nsa_in_kernel(x_ref, gain_ref, w_ref, cos_ref, sin_ref,
                   q_ref, kc_ref, vc_ref, ks_ref, vs_ref, kw_ref, vw_ref, gates_ref):
    xn = _rms(x_ref[...], gain_ref[...]).astype(BF16)
    cos_t = cos_ref[...]
    sin_t = sin_ref[...]
    tm = cos_t.shape[0]
    lane = lax.broadcasted_iota(jnp.int32, (tm, LANES), 1)
    first_half = (lane % NSA_HD) < (NSA_HD // 2)

    def chunk(c):
        return _dot(xn, w_ref[:, 256 * c:256 * (c + 1)])

    def roped(r):
        return [_rope_tile(r[:, LANES * h:LANES * (h + 1)], cos_t, sin_t, first_half) for h in range(2)]

    def store_heads(ref, base, tiles, scale):
        for h, tile in enumerate(tiles):
            ref[base + 2 * h] = (tile[:, :NSA_HD] * scale).astype(ref.dtype)
            ref[base + 2 * h + 1] = (tile[:, NSA_HD:] * scale).astype(ref.dtype)

    for c in range(4):
        store_heads(q_ref, 4 * c, roped(chunk(c)), NSA_HD ** -0.5)
    kc = roped(chunk(4))
    kc_ref[...] = jnp.concatenate(kc, axis=1)
    vc_ref[...] = chunk(5)
    store_heads(ks_ref, 0, roped(chunk(6)), 1.0)
    r = chunk(7)
    store_heads(vs_ref, 0, [r[:, :LANES], r[:, LANES:]], 1.0)
    store_heads(kw_ref, 0, roped(chunk(8)), 1.0)
    r = chunk(9)
    store_heads(vw_ref, 0, [r[:, :LANES], r[:, LANES:]], 1.0)
    gates_ref[:, :256] = jax.nn.sigmoid(chunk(10))
    gates_ref[:, 256:] = jax.nn.sigmoid(chunk(11))


def _nsa_in_proj(x, gain, w, cos_t, sin_t, tm=512):
    b, s, d = x.shape
    tm = min(tm, s)
    assert s % tm == 0
    g, hd = NSA_GROUPS, NSA_HD
    kv_heads = pl.BlockSpec((None, g, tm, hd), lambda bi, i: (bi, 0, i, 0))
    kv_flat = pl.BlockSpec((None, tm, g * hd), lambda bi, i: (bi, i, 0))
    tab = pl.BlockSpec((tm, LANES), lambda bi, i: (i, 0))
    return pl.pallas_call(
        _nsa_in_kernel,
        name="nsa_in",
        grid=(b, s // tm),
        in_specs=[pl.BlockSpec((None, tm, d), lambda bi, i: (bi, i, 0)),
                  pl.BlockSpec((1, d), lambda bi, i: (0, 0)),
                  pl.BlockSpec(w.shape, lambda bi, i: (0, 0)),
                  tab, tab],
        out_specs=[pl.BlockSpec((None, NSA_HEADS, tm, hd), lambda bi, i: (bi, 0, i, 0)),
                   kv_flat, kv_flat, kv_heads, kv_heads, kv_heads, kv_heads,
                   pl.BlockSpec((None, tm, 4 * LANES), lambda bi, i: (bi, i, 0))],
        out_shape=[jax.ShapeDtypeStruct((b, NSA_HEADS, s, hd), BF16),
                   jax.ShapeDtypeStruct((b, s, g * hd), F32),
                   jax.ShapeDtypeStruct((b, s, g * hd), F32),
                   jax.ShapeDtypeStruct((b, g, s, hd), BF16),
                   jax.ShapeDtypeStruct((b, g, s, hd), BF16),
                   jax.ShapeDtypeStruct((b, g, s, hd), BF16),
                   jax.ShapeDtypeStruct((b, g, s, hd), BF16),
                   jax.ShapeDtypeStruct((b, s, 4 * LANES), F32)],
        compiler_params=_params("parallel", "parallel"),
    )(x, gain, w, cos_t, sin_t)


def _cmp_proj_kernel(x_ref, plo_ref, phi_ref, a_ref, b_ref, ya_ref, yb_ref):
    x = x_ref[...]
    ya_ref[...] = _dot((x + plo_ref[...]).astype(BF16), a_ref[...])
    yb_ref[...] = _dot((x + phi_ref[...]).astype(BF16), b_ref[...])


def _cmp_mix_kernel(ya_ref, yb_ref, w2_ref, o_ref):
    n = ya_ref.shape[0]
    z = ya_ref[...] + pltpu.roll(yb_ref[...], n - 1, 0)
    h = (z * jax.nn.sigmoid(z)).astype(BF16)
    out = _dot(h, w2_ref[...])
    for g in range(NSA_GROUPS):
        o_ref[g] = out[:, NSA_HD * g:NSA_HD * (g + 1)].astype(o_ref.dtype)


def _compress(kv, pos, w1, w2, tc=256):
    b, s, _ = kv.shape
    g, hd = NSA_GROUPS, NSA_HD
    n = s // CMP_STRIDE
    wide = CMP_STRIDE * g * hd
    tc = min(tc, n)
    x = kv.reshape(b, n, wide)
    eye = jnp.eye(g, dtype=F32)
    w1r = w1.reshape(CMP_BLOCK, hd, hd)
    mat_a = jnp.einsum("pde,gh->pgdhe", w1r[:CMP_STRIDE], eye).reshape(wide, g * hd).astype(BF16)
    mat_b = jnp.einsum("pde,gh->pgdhe", w1r[CMP_STRIDE:], eye).reshape(wide, g * hd).astype(BF16)
    p_lo = jnp.broadcast_to(pos[:CMP_STRIDE, None, :], (CMP_STRIDE, g, hd)).reshape(1, wide)
    p_hi = jnp.broadcast_to(pos[CMP_STRIDE:, None, :], (CMP_STRIDE, g, hd)).reshape(1, wide)
    w2bd = jnp.einsum("de,gh->gdhe", w2, eye).reshape(g * hd, g * hd).astype(BF16)
    vec = pl.BlockSpec((1, wide), lambda bi, i: (0, 0))
    mat = pl.BlockSpec((wide, g * hd), lambda bi, i: (0, 0))
    yspec = pl.BlockSpec((None, tc, g * hd), lambda bi, i: (bi, i, 0))
    ya, yb = pl.pallas_call(
        _cmp_proj_kernel,
        name="cmp_proj",
        grid=(b, n // tc),
        in_specs=[pl.BlockSpec((None, tc, wide), lambda bi, i: (bi, i, 0)), vec, vec, mat, mat],
        out_specs=[yspec, yspec],
        out_shape=[jax.ShapeDtypeStruct((b, n, g * hd), F32)] * 2,
        compiler_params=_params("parallel", "parallel"),
    )(x, p_lo, p_hi, mat_a, mat_b)
    full = pl.BlockSpec((None, n, g * hd), lambda bi: (bi, 0, 0))
    return pl.pallas_call(
        _cmp_mix_kernel,
        name="cmp_mix",
        grid=(b,),
        in_specs=[full, full, pl.BlockSpec((g * hd, g * hd), lambda bi: (0, 0))],
        out_specs=pl.BlockSpec((None, g, n, hd), lambda bi: (bi, 0, 0, 0)),
        out_shape=jax.ShapeDtypeStruct((b, g, n, hd), BF16),
        compiler_params=_params("parallel"),
    )(ya, yb, w2bd)


def _nsa_cw_kernel(q_ref, kc_ref, vc_ref, kw_ref, vw_ref, gates_ref, amat_ref, ocw_ref, idx_ref, *, chunk):
    t0 = pl.program_id(2) * Q_BLOCK
    n_chunk = kc_ref.shape[0] // chunk
    last_cmp = (t0 + Q_BLOCK - CMP_BLOCK) // CMP_STRIDE
    need = jnp.minimum(last_cmp // chunk + 1, n_chunk)
    for nc in range(1, n_chunk + 1):
        pl.when(need == nc)(functools.partial(
            _nsa_cw_tile, q_ref, kc_ref, vc_ref, kw_ref, vw_ref, gates_ref, amat_ref, ocw_ref, idx_ref,
            n_cmp=nc * chunk, n_sel=nc * chunk * CMP_STRIDE // SEL_BLOCK))


def _nsa_cw_tile(q_ref, kc_ref, vc_ref, kw_ref, vw_ref, gates_ref, amat_ref, ocw_ref, idx_ref, *, n_cmp, n_sel):
    qb = Q_BLOCK
    rows = NSA_HPG * qb
    t0 = pl.program_id(2) * qb
    q = q_ref[...].reshape(rows, NSA_HD)
    t_row = t0 + lax.broadcasted_iota(jnp.int32, (rows, 1), 0) % qb

    s = _dot_nt(q, kc_ref[0:n_cmp, :])
    cmp_end = lax.broadcasted_iota(jnp.int32, (1, n_cmp), 1) * CMP_STRIDE + (CMP_BLOCK - 1)
    p = _softmax_rows(jnp.where(cmp_end <= t_row, s, NEG_INF))
    o_c = _dot(p.astype(BF16), vc_ref[0:n_cmp, :])
    p_grp = p[0:qb] + p[qb:2 * qb] + p[2 * qb:3 * qb] + p[3 * qb:4 * qb]
    p_hi = p_grp.astype(BF16)
    p_lo = (p_grp - p_hi.astype(F32)).astype(BF16)
    amat = amat_ref[0:n_sel, 0:n_cmp]
    imp = _dot_nt(amat, p_hi) + _dot_nt(amat, p_lo)

    t_tok = t0 + lax.broadcasted_iota(jnp.int32, (1, qb), 1)
    cur = (t_tok // SEL_BLOCK).astype(F32)
    blk = lax.broadcasted_iota(jnp.int32, (n_sel, 1), 0).astype(F32)
    forced = (blk == 0.0) | (blk == cur - 1.0)
    score = jnp.where(forced, jnp.inf, jnp.where(blk < cur, imp, NEG_INF))
    score = jnp.where(blk == cur, NEG_INF, score)
    slot = lax.broadcasted_iota(jnp.int32, (N_SELECT, qb), 0)
    sel = jnp.where(slot == 0, cur, 0.0)
    for n in range(1, N_SELECT):
        m = jnp.max(score, axis=0, keepdims=True)
        pick = jnp.min(jnp.where(score == m, blk, float(n_sel)), axis=0, keepdims=True)
        score = jnp.where(blk == pick, NEG_INF, score)
        sel = jnp.where(slot == n, pick, sel)
    idx_ref[...] = sel.astype(jnp.int32)

    w_len = WINDOW + qb
    w_start = pl.multiple_of(jnp.maximum(t0 - WINDOW, 0), qb)
    s_w = _dot_nt(q, kw_ref[pl.ds(w_start, w_len), :])
    kpos = w_start + lax.broadcasted_iota(jnp.int32, (1, w_len), 1)
    mask_w = (kpos <= t_row) & (kpos > t_row - WINDOW)
    p_w = _softmax_rows(jnp.where(mask_w, s_w, NEG_INF))
    o_w = _dot(p_w.astype(BF16), vw_ref[pl.ds(w_start, w_len), :])

    gates = gates_ref[...]
    outs = []
    for h in range(NSA_HPG):
        g_c = gates[:, 3 * h:3 * h + 1]
        g_w = gates[:, 3 * h + 2:3 * h + 3]
        outs.append(g_c * o_c[h * qb:(h + 1) * qb] + g_w * o_w[h * qb:(h + 1) * qb])
    ocw_ref[...] = jnp.concatenate(outs, axis=1)


def _nsa_cmp_win(q, k_cmp, v_cmp, k_w, v_w, gates, amat, chunk=2 * LANES):
    b, _, s, hd = q.shape
    g = NSA_GROUPS
    n_cmp = k_cmp.shape[2]
    cmp_spec = pl.BlockSpec((None, None, n_cmp, hd), lambda bi, gi, j: (bi, gi, 0, 0))
    win_spec = pl.BlockSpec((None, None, s, hd), lambda bi, gi, j: (bi, gi, 0, 0))
    chunk = min(n_cmp, chunk)
    assert n_cmp % chunk == 0
    return pl.pallas_call(
        functools.partial(_nsa_cw_kernel, chunk=chunk),
        name="nsa_cw",
        grid=(b, g, s // Q_BLOCK),
        in_specs=[pl.BlockSpec((None, NSA_HPG, Q_BLOCK, hd), lambda bi, gi, j: (bi, gi, j, 0)),
                  cmp_spec, cmp_spec, win_spec, win_spec,
                  pl.BlockSpec((None, Q_BLOCK, LANES), lambda bi, gi, j: (bi, j, gi)),
                  pl.BlockSpec(amat.shape, lambda bi, gi, j: (0, 0))],
        out_specs=[pl.BlockSpec((None, Q_BLOCK, NSA_HPG * hd), lambda bi, gi, j: (bi, j, gi)),
                   pl.BlockSpec((None, None, None, N_SELECT, Q_BLOCK), lambda bi, gi, j: (bi, gi, j, 0, 0))],
        out_shape=[jax.ShapeDtypeStruct((b, s, NSA_HEADS * hd), F32),
                   jax.ShapeDtypeStruct((b, g, s // Q_BLOCK, N_SELECT, Q_BLOCK), jnp.int32)],
        compiler_params=_params("parallel", "parallel", "arbitrary"),
    )(q, k_cmp, v_cmp, k_w, v_w, gates, amat)


def _roll_rows(x, shift):
    return jnp.concatenate([pltpu.roll(x[r:r + 8], shift, 0) for r in range(0, x.shape[0], 8)], axis=0)


def _nsa_sel_kernel(idx_ref, q2_ref, kv2_ref, o_ref, s_a, s_b, v_a, v_b, *, tq, pairs):
    half_blk = SEL_BLOCK // 2
    n_col = N_SELECT * half_blk
    t0 = pl.program_id(2) * tq
    row = lax.broadcasted_iota(jnp.int32, (16, n_col), 0)
    col = lax.broadcasted_iota(jnp.int32, (16, n_col), 1)
    pos0 = 2 * col + (row % 8) // NSA_HPG
    in_slot0 = col < half_blk
    second = (row >= 8).astype(jnp.int32)
    row8 = lax.broadcasted_iota(jnp.int32, (8, LANES), 0)

    def stage_scores(grp, s_buf, v_buf):
        for u in range(pairs):
            tp = grp * pairs + u
            q16 = q2_ref[pl.ds(pl.multiple_of(16 * tp, 16), 16), :]
            halves = []
            for w in range(2):
                tok = 2 * tp + w
                base = (tok // Q_BLOCK) * (N_SELECT * Q_BLOCK) + tok % Q_BLOCK
                blocks = []
                for n in range(N_SELECT):
                    off = pl.multiple_of(idx_ref[0, base + n * Q_BLOCK] * half_blk, half_blk)
                    blocks.append(kv2_ref[pl.ds(off, half_blk), :])
                keys = jnp.concatenate([blk[:, :LANES] for blk in blocks], axis=0)
                v_buf[(2 * u + w) * n_col:(2 * u + w + 1) * n_col, :] = jnp.concatenate(
                    [blk[:, LANES:] for blk in blocks], axis=0)
                halves.append(_dot_nt(q16, keys)[8 * w:8 * w + 8])
            s_buf[16 * u:16 * u + 16, :] = jnp.concatenate(halves, axis=0)

    def stage_output(grp, s_buf, v_buf):
        for u in range(pairs):
            tp = grp * pairs + u
            ta = t0 + 2 * tp
            n_valid = jnp.minimum(ta // SEL_BLOCK + 1, N_SELECT)
            ok = jnp.where(in_slot0, pos0 - (ta % SEL_BLOCK + second), col - n_valid * half_blk + 1) <= 0
            s = jnp.where(ok, s_buf[16 * u:16 * u + 16, :], NEG_INF)
            m = jnp.max(s, axis=-1, keepdims=True)
            m = jnp.maximum(m, _roll_rows(m, NSA_HPG))
            e = jnp.exp(s - m)
            l = jnp.sum(e, axis=-1, keepdims=True)
            l = l + _roll_rows(l, NSA_HPG)
            p = (e / l).astype(BF16)
            xa = _dot(p, v_buf[(2 * u) * n_col:(2 * u + 1) * n_col, :])
            xb = _dot(p, v_buf[(2 * u + 1) * n_col:(2 * u + 2) * n_col, :])
            x = jnp.concatenate([xa[:8], xb[8:]], axis=0)
            y = x + pltpu.roll(_roll_rows(x, NSA_HPG), NSA_HD, 1)
            out = jnp.where(row8 < NSA_HPG, y[:8], pltpu.roll(y[8:], NSA_HPG, 0))
            o_ref[pl.ds(pl.multiple_of(8 * tp, 8), 8), :] = out

    n_grp = tq // (2 * pairs)
    stage_scores(0, s_a, v_a)

    def body(k, carry):
        stage_scores(2 * k + 1, s_b, v_b)
        stage_output(2 * k, s_a, v_a)
        stage_scores(jnp.minimum(2 * k + 2, n_grp - 1), s_a, v_a)
        stage_output(2 * k + 1, s_b, v_b)
        return carry

    lax.fori_loop(0, n_grp // 2, body, 0)


def _nsa_selected(idx, q2, kv2, tq=256, pairs=8):
    b, g = idx.shape[:2]
    s = kv2.shape[2] * 2
    tq = min(tq, s)
    nt = s // tq
    assert (tq // (2 * pairs)) % 2 == 0 and tq % Q_BLOCK == 0
    idx_flat = idx.reshape(b * g * nt, 1, tq * N_SELECT)
    n_col = N_SELECT * SEL_BLOCK // 2
    scratch = [pltpu.VMEM((pairs * 16, n_col), F32)] * 2 + [pltpu.VMEM((pairs * 2 * n_col, LANES), BF16)] * 2
    return pl.pallas_call(
        functools.partial(_nsa_sel_kernel, tq=tq, pairs=pairs),
        scratch_shapes=scratch,
        name="nsa_sel",
        grid=(b, g, nt),
        in_specs=[pl.BlockSpec((None, 1, tq * N_SELECT), lambda bi, gi, j: ((bi * g + gi) * nt + j, 0, 0),
                               memory_space=pltpu.SMEM),
                  pl.BlockSpec((None, None, tq * 8, LANES), lambda bi, gi, j: (bi, gi, j, 0)),
                  pl.BlockSpec((None, None, s // 2, 2 * LANES), lambda bi, gi, j: (bi, gi, 0, 0))],
        out_specs=pl.BlockSpec((None, None, tq * NSA_HPG, LANES), lambda bi, gi, j: (bi, gi, j, 0)),
        out_shape=jax.ShapeDtypeStruct((b, g, s * NSA_HPG, LANES), F32),
        compiler_params=_params("parallel", "parallel", "arbitrary"),
    )(idx_flat, q2, kv2)


def _nsa_out_kernel(x_ref, ocw_ref, os_ref, gates_ref, emat_ref, w_ref, gain_ref, o_ref):
    g_s = _dot_split(gates_ref[...], emat_ref[...])
    a = (ocw_ref[...] + g_s * os_ref[...]).astype(BF16)
    o_ref[...] = x_ref[...] + _rms(_dot(a, w_ref[...]), gain_ref[...])


def _nsa_out_proj(x, o_cw, o_s, gates, emat, w, gain, tm=1024):
    t, d = x.shape
    tm = min(tm, t)
    assert t % tm == 0
    row = pl.BlockSpec((tm, d), lambda i: (i, 0))
    return pl.pallas_call(
        _nsa_out_kernel,
        name="nsa_out",
        grid=(t // tm,),
        in_specs=[row, row, row,
                  pl.BlockSpec((tm, gates.shape[1]), lambda i: (i, 0)),
                  pl.BlockSpec(emat.shape, lambda i: (0, 0)),
                  pl.BlockSpec(w.shape, lambda i: (0, 0)),
                  pl.BlockSpec((1, d), lambda i: (0, 0))],
        out_specs=row,
        out_shape=jax.ShapeDtypeStruct((t, d), F32),
        compiler_params=_params("parallel"),
    )(x, o_cw, o_s, gates, emat, w, gain)


def _proj_out_kernel(x_ref, a_ref, w_ref, gain_ref, o_ref):
    o_ref[...] = x_ref[...] + _rms(_dot(a_ref[...], w_ref[...]), gain_ref[...])


def _proj_out(x, a, w, gain, tm=1024):
    t, d = x.shape
    tm = min(tm, t)
    assert t % tm == 0
    row = pl.BlockSpec((tm, d), lambda i: (i, 0))
    return pl.pallas_call(
        _proj_out_kernel,
        name="proj_out",
        grid=(t // tm,),
        in_specs=[row, pl.BlockSpec((tm, a.shape[1]), lambda i: (i, 0)),
                  pl.BlockSpec(w.shape, lambda i: (0, 0)),
                  pl.BlockSpec((1, d), lambda i: (0, 0))],
        out_specs=row,
        out_shape=jax.ShapeDtypeStruct((t, d), F32),
        compiler_params=_params("parallel"),
    )(x, a, w, gain)


def _nsa_mixer(x, gain_pre, gain_post, w_in, w_out, cmp_pos, cmp_w1, cmp_w2):
    b, s, d = x.shape
    g, hd, hpg = NSA_GROUPS, NSA_HD, NSA_HPG
    n_gate = 3 * hpg
    q_cols = NSA_HEADS * hd
    kv_end = q_cols + 6 * NSA_KV_COLS

    gate_w = w_in[:, kv_end:].reshape(d, g, n_gate)
    gate_w = jnp.pad(gate_w, ((0, 0), (0, 0), (0, LANES - n_gate))).reshape(d, g * LANES)
    w_all = jnp.concatenate([w_in[:, :kv_end], gate_w], axis=1).astype(BF16)

    half = hd // 2
    inv = ROPE_THETA ** (-jnp.arange(half, dtype=F32) / half)
    ang = jnp.arange(s).astype(F32)[:, None] * inv[None, :]
    cos, sin = jnp.cos(ang), jnp.sin(ang)
    cos_t = jnp.tile(cos, (1, LANES // half))
    sin_t = jnp.tile(jnp.concatenate([-sin, sin], axis=1), (1, LANES // hd))

    q, k_c, v_c, k_s, v_s, k_w, v_w, gates = _nsa_in_proj(x, gain_pre, w_all, cos_t, sin_t)
    k_cmp = _compress(k_c, cmp_pos[0], cmp_w1[0], cmp_w2[0])
    v_cmp = _compress(v_c, cmp_pos[1], cmp_w1[1], cmp_w2[1])

    n_cmp_rows = s // CMP_STRIDE
    n_sel = s // SEL_BLOCK
    ratio = SEL_BLOCK // CMP_STRIDE
    c_id = jnp.arange(n_cmp_rows)[:, None]
    n_id = jnp.arange(n_sel)[None, :]
    lo = ratio * n_id - (CMP_BLOCK // CMP_STRIDE - 1)
    amat = ((c_id >= lo) & (c_id <= ratio * n_id + ratio - 1) & (c_id < n_cmp_rows - 1)).astype(BF16)

    o_cw, idx = _nsa_cmp_win(q, k_cmp, v_cmp, k_w, v_w, gates, amat.T)

    qt = q.reshape(b, g, hpg, s, hd).transpose(0, 1, 3, 2, 4)
    zero = jnp.zeros_like(qt)
    q2 = jnp.stack([jnp.concatenate([qt, zero], axis=-1), jnp.concatenate([zero, qt], axis=-1)], axis=3)
    q2 = q2.reshape(b, g, s * 2 * hpg, 2 * hd)
    kv2 = jnp.concatenate([k_s.reshape(b, g, s // 2, 2 * hd), v_s.reshape(b, g, s // 2, 2 * hd)], axis=-1)
    o_s = _nsa_selected(idx, q2, kv2)
    o_s = o_s[..., :hd].reshape(b, g, s, hpg * hd).transpose(0, 2, 1, 3).reshape(b * s, q_cols)

    r_id = jnp.arange(g * LANES)[:, None]
    col = jnp.arange(q_cols)[None, :]
    emat = (r_id == (col // (hpg * hd)) * LANES + 3 * ((col // hd) % hpg) + 1).astype(BF16)

    out = _nsa_out_proj(x.reshape(b * s, d), o_cw.reshape(b * s, q_cols), o_s,
                        gates.reshape(b * s, g * LANES), emat, w_out.astype(BF16), gain_post)
    return out.reshape(b, s, d)


def _hgrn_in_kernel(x_ref, gain_ref, w_ref, lb_ref, q_ref, k_ref, lf_ref, v_ref, gs_ref):
    xn = _rms(x_ref[...], gain_ref[...]).astype(BF16)
    n_chunk = D_MODEL // 256

    def chunk(c):
        return _dot(xn, w_ref[:, 256 * c:256 * (c + 1)])

    for c in range(n_chunk):
        cols = slice(256 * c, 256 * (c + 1))
        r = chunk(c)
        q_ref[:, cols] = r * jax.nn.sigmoid(r)
        f = chunk(n_chunk + c)
        lb = lb_ref[:, cols]
        log_sig = jnp.minimum(f, 0.0) - jnp.log1p(jnp.exp(-jnp.abs(f)))
        a = jnp.log(lb)
        bb = jnp.log1p(-lb) + log_sig
        log_f = jnp.maximum(a, bb) + jnp.log1p(jnp.exp(-jnp.abs(a - bb)))
        lf_ref[:, cols] = log_f
        k_ref[:, cols] = 1.0 - jnp.exp(log_f)
        v_ref[:, cols] = chunk(2 * n_chunk + c)
        r = chunk(3 * n_chunk + c)
        gs_ref[:, cols] = r * jax.nn.sigmoid(r)


def _hgrn_in_proj(x, gain, w, lb, tm=512):
    t, d = x.shape
    tm = min(tm, t)
    assert t % tm == 0
    row = pl.BlockSpec((tm, d), lambda i: (i, 0))
    vec = pl.BlockSpec((1, d), lambda i: (0, 0))
    return pl.pallas_call(
        _hgrn_in_kernel,
        name="hgrn_in",
        grid=(t // tm,),
        in_specs=[row, vec, pl.BlockSpec(w.shape, lambda i: (0, 0)), vec],
        out_specs=[row] * 5,
        out_shape=[jax.ShapeDtypeStruct((t, d), F32)] * 5,
        compiler_params=_params("parallel"),
    )(x, gain, w, lb)


def _hgrn_scan_kernel(q_ref, k_ref, lf_ref, v_ref, gs_ref, gain_ref, o_ref, state_ref, *, rows, heads):
    sub = HGRN_SUB
    n_sub = rows // sub

    @pl.when(pl.program_id(2) == 0)
    def _():
        state_ref[...] = jnp.zeros_like(state_ref)

    r_id = lax.broadcasted_iota(jnp.int32, (rows, rows), 0)
    c_id = lax.broadcasted_iota(jnp.int32, (rows, rows), 1)
    tri = jnp.where(((r_id // sub) == (c_id // sub)) & (c_id <= r_id), 1.0, 0.0)
    row_id = lax.broadcasted_iota(jnp.int32, (sub, HGRN_DK), 0)

    for h in range(heads):
        lanes = slice(LANES * h, LANES * (h + 1))
        cum = jnp.dot(tri, lf_ref[:, lanes], preferred_element_type=F32, precision=lax.Precision.HIGHEST)
        q = q_ref[:, lanes]
        k = k_ref[:, lanes]
        v = v_ref[:, lanes]
        q_dec = (q * jnp.exp(cum)).astype(BF16)

        intra, update, decay = [], [], []
        for i in range(n_sub):
            sl = slice(sub * i, sub * (i + 1))
            qi, ki, vi, ci = q[sl], k[sl], v[sl], cum[sl]
            total = ci[sub - 1:sub]
            o_i = None
            for j in range(sub):
                w = jnp.exp(jnp.where(row_id >= j, ci - ci[j:j + 1], NEG_INF))
                att = jnp.sum(qi * w * ki[j:j + 1], axis=-1, keepdims=True)
                o_i = att * vi[j:j + 1] if o_i is None else o_i + att * vi[j:j + 1]
            intra.append(o_i)
            update.append(_dot_tn(vi.astype(BF16), (ki * jnp.exp(total - ci)).astype(BF16)))
            decay.append(jnp.exp(total))

        state = state_ref[h]
        outs = []
        for i in range(n_sub):
            outs.append(intra[i] + _dot_nt(q_dec[sub * i:sub * (i + 1)], state.astype(BF16)))
            state = state * decay[i] + update[i]
        state_ref[h] = state

        o = jnp.concatenate(outs, axis=0)
        o_ref[:, lanes] = (_rms(o, gain_ref[...]) * gs_ref[:, lanes]).astype(o_ref.dtype)


def _hgrn_scan(q, k, lf, v, gs, gain, rows=128, heads=4):
    b, s, d = q.shape
    rows = min(rows, s)
    blk = pl.BlockSpec((None, rows, heads * LANES), lambda bi, h, l: (bi, l, h))
    return pl.pallas_call(
        functools.partial(_hgrn_scan_kernel, rows=rows, heads=heads),
        name="hgrn_scan",
        grid=(b, d // (heads * LANES), s // rows),
        in_specs=[blk] * 5 + [pl.BlockSpec((1, LANES), lambda bi, h, l: (0, 0))],
        out_specs=blk,
        out_shape=jax.ShapeDtypeStruct((b, s, d), BF16),
        scratch_shapes=[pltpu.VMEM((heads, LANES, HGRN_DK), F32)],
        compiler_params=_params("parallel", "parallel", "arbitrary"),
    )(q, k, lf, v, gs, gain)


def _hgrn_mixer(x, gain_pre, gain_post, w_in, w_out, norm_gain, lower_bound):
    b, s, d = x.shape
    x2 = x.reshape(b * s, d)
    q, k, lf, v, gs = _hgrn_in_proj(x2, gain_pre, w_in.astype(BF16), lower_bound.reshape(1, d))
    shape = (b, s, d)
    o = _hgrn_scan(q.reshape(shape), k.reshape(shape), lf.reshape(shape), v.reshape(shape),
                   gs.reshape(shape), norm_gain.reshape(1, LANES))
    return _proj_out(x2, o.reshape(b * s, d), w_out.astype(BF16), gain_post).reshape(shape)


def kernel(x, p, norm_gains, ffn_w_in, ffn_w_out, ple_w_in, ple_w_gate, nsa_w_in, nsa_w_out, nsa_cmp_pos,
           nsa_cmp_w1, nsa_cmp_w2, hgrn_w_in, hgrn_w_out, hgrn_norm, hgrn_lb_logits):
    b, s, d = x.shape
    depth = norm_gains.shape[0]
    lb_sm = jax.nn.softmax(hgrn_lb_logits.astype(F32), axis=0)
    lower_bounds = jnp.cumsum(lb_sm, axis=0) - lb_sm[0]
    for layer in range(depth):
        ng = norm_gains[layer].reshape(8, 1, d)
        x2 = x.reshape(b * s, d)
        x2 = _ffn_block(x2, ng[0], ng[1], ffn_w_in[layer, 0].astype(BF16), ffn_w_out[layer, 0].astype(BF16))
        x = x2.reshape(b, s, d)
        j = layer // 2
        if layer % 2 == 0:
            x = _nsa_mixer(x, ng[2], ng[3], nsa_w_in[j], nsa_w_out[j], nsa_cmp_pos[j], nsa_cmp_w1[j],
                           nsa_cmp_w2[j])
        else:
            x = _hgrn_mixer(x, ng[2], ng[3], hgrn_w_in[j], hgrn_w_out[j], hgrn_norm[j], lower_bounds[layer])
        x2 = x.reshape(b * s, d)
        x2 = _ffn_block(x2, ng[4], ng[5], ffn_w_in[layer, 1].astype(BF16), ffn_w_out[layer, 1].astype(BF16))
        x2 = _ple_block(x2, p[layer].reshape(b * s, -1), ng[6], ng[7], ple_w_gate[layer].astype(BF16),
                        ple_w_in[layer].astype(BF16))
        x = x2.reshape(b, s, d)
    return x
```

```python
import functools

import jax
import jax.numpy as jnp
from jax import lax
from jax.experimental import pallas as pl
from jax.experimental.pallas import tpu as pltpu

F32 = jnp.float32
BF16 = jnp.bfloat16
NEG_INF = float("-inf")

D_MODEL = 1024
NORM_EPS = 1e-6
ROPE_THETA = 10000.0

NSA_HEADS = 16
NSA_GROUPS = 4
NSA_HPG = NSA_HEADS // NSA_GROUPS
NSA_HD = 64
NSA_KV_COLS = NSA_GROUPS * NSA_HD
CMP_BLOCK = 32
CMP_STRIDE = 16
SEL_BLOCK = 64
N_SELECT = 16
WINDOW = 512
Q_BLOCK = 128

HGRN_HEADS = 8
HGRN_DK = 128
HGRN_SUB = 16

LANES = 128
VMEM_LIMIT = 56 * 1024 * 1024


def _params(*sem):
    return pltpu.CompilerParams(dimension_semantics=sem, vmem_limit_bytes=VMEM_LIMIT)


def _rms(x, gain):
    return x * lax.rsqrt(jnp.mean(x * x, axis=-1, keepdims=True) + NORM_EPS) * gain


def _dot(a, b):
    return jnp.dot(a, b, preferred_element_type=F32)


def _dot_nt(a, b):
    return lax.dot_general(a, b, (((1,), (1,)), ((), ())), preferred_element_type=F32)


def _dot_tn(a, b):
    return lax.dot_general(a, b, (((0,), (0,)), ((), ())), preferred_element_type=F32)


def _dot_split(x, w):
    hi = x.astype(BF16)
    lo = (x - hi.astype(F32)).astype(BF16)
    return _dot(hi, w) + _dot(lo, w)


def _softmax_rows(s):
    m = jnp.max(s, axis=-1, keepdims=True)
    m = jnp.where(m == NEG_INF, 0.0, m)
    e = jnp.exp(s - m)
    return e / jnp.maximum(jnp.sum(e, axis=-1, keepdims=True), 1e-30)


def _ffn_kernel(x_ref, g0_ref, g1_ref, wg_ref, wu_ref, wo_ref, o_ref, xn_ref, acc_ref):
    j = pl.program_id(1)

    @pl.when(j == 0)
    def _():
        xn_ref[...] = _rms(x_ref[...], g0_ref[...]).astype(BF16)
        acc_ref[...] = jnp.zeros_like(acc_ref)

    xn = xn_ref[...]
    g = _dot(xn, wg_ref[...])
    u = _dot(xn, wu_ref[...])
    h = (g * jax.nn.sigmoid(g) * u).astype(BF16)
    acc_ref[...] += _dot(h, wo_ref[...])

    @pl.when(j == pl.num_programs(1) - 1)
    def _():
        o_ref[...] = x_ref[...] + 0.5 * _rms(acc_ref[...], g1_ref[...])


def _ffn_block(x, g_pre, g_post, w_in, w_out, tm=1024, tf=256):
    t, d = x.shape
    dff = w_out.shape[0]
    tm = min(tm, t)
    assert t % tm == 0
    nf = dff // tf
    row = pl.BlockSpec((tm, d), lambda i, j: (i, 0))
    vec = pl.BlockSpec((1, d), lambda i, j: (0, 0))
    return pl.pallas_call(
        _ffn_kernel,
        name="ffn",
        grid=(t // tm, nf),
        in_specs=[row, vec, vec,
                  pl.BlockSpec((d, tf), lambda i, j: (0, j)),
                  pl.BlockSpec((d, tf), lambda i, j: (0, j + nf)),
                  pl.BlockSpec((tf, d), lambda i, j: (j, 0))],
        out_specs=row,
        out_shape=jax.ShapeDtypeStruct((t, d), F32),
        scratch_shapes=[pltpu.VMEM((tm, d), BF16), pltpu.VMEM((tm, d), F32)],
        compiler_params=_params("parallel", "arbitrary"),
    )(x, g_pre, g_post, w_in, w_in, w_out)


def _ple_kernel(x_ref, p_ref, g0_ref, g1_ref, wg_ref, wp_ref, o_ref):
    x = x_ref[...]
    xn = _rms(x, g0_ref[...]).astype(BF16)
    gate = jax.nn.sigmoid(_dot(xn, wg_ref[...]))
    e = _dot(p_ref[...].astype(BF16), wp_ref[...])
    o_ref[...] = x + _rms(e * gate, g1_ref[...])


def _ple_block(x, p, g_pre, g_post, w_gate, w_p, tm=1024):
    t, d = x.shape
    pd = p.shape[1]
    tm = min(tm, t)
    assert t % tm == 0
    row = pl.BlockSpec((tm, d), lambda i: (i, 0))
    vec = pl.BlockSpec((1, d), lambda i: (0, 0))
    return pl.pallas_call(
        _ple_kernel,
        name="ple",
        grid=(t // tm,),
        in_specs=[row, pl.BlockSpec((tm, pd), lambda i: (i, 0)), vec, vec,
                  pl.BlockSpec((d, d), lambda i: (0, 0)),
                  pl.BlockSpec((pd, d), lambda i: (0, 0))],
        out_specs=row,
        out_shape=jax.ShapeDtypeStruct((t, d), F32),
        compiler_params=_params("parallel"),
    )(x, p, g_pre, g_post, w_gate, w_p)


def _rope_tile(xb, cos_t, sin_t, first_half):
    fwd = pltpu.roll(xb, LANES - NSA_HD // 2, 1)
    bwd = pltpu.roll(xb, NSA_HD // 2, 1)
    return xb * cos_t + jnp.where(first_half, fwd, bwd) * sin_t


def _nsa_in_kernel(x_ref, gain_ref, w_ref, cos_ref, sin_ref,
                   q_ref, kc_ref, vc_ref, ks_ref, vs_ref, kw_ref, vw_ref, gates_ref):
    xn = _rms(x_ref[...], gain_ref[...]).astype(BF16)
    cos_t = cos_ref[...]
    sin_t = sin_ref[...]
    tm = cos_t.shape[0]
    lane = lax.broadcasted_iota(jnp.int32, (tm, LANES), 1)
    first_half = (lane % NSA_HD) < (NSA_HD // 2)

    def chunk(c):
        return _dot(xn, w_ref[:, 256 * c:256 * (c + 1)])

    def roped(r):
        return [_rope_tile(r[:, LANES * h:LANES * (h + 1)], cos_t, sin_t, first_half) for h in range(2)]

    def store_heads(ref, base, tiles, scale):
        for h, tile in enumerate(tiles):
            ref[base + 2 * h] = (tile[:, :NSA_HD] * scale).astype(ref.dtype)
            ref[base + 2 * h + 1] = (tile[:, NSA_HD:] * scale).astype(ref.dtype)

    for c in range(4):
        store_heads(q_ref, 4 * c, roped(chunk(c)), NSA_HD ** -0.5)
    kc = roped(chunk(4))
    kc_ref[...] = jnp.concatenate(kc, axis=1)
    vc_ref[...] = chunk(5)
    store_heads(ks_ref, 0, roped(chunk(6)), 1.0)
    r = chunk(7)
    store_heads(vs_ref, 0, [r[:, :LANES], r[:, LANES:]], 1.0)
    store_heads(kw_ref, 0, roped(chunk(8)), 1.0)
    r = chunk(9)
    store_heads(vw_ref, 0, [r[:, :LANES], r[:, LANES:]], 1.0)
    gates_ref[:, :256] = jax.nn.sigmoid(chunk(10))
    gates_ref[:, 256:] = jax.nn.sigmoid(chunk(11))


def _nsa_in_proj(x, gain, w, cos_t, sin_t, tm=512):
    b, s, d = x.shape
    tm = min(tm, s)
    assert s % tm == 0
    g, hd = NSA_GROUPS, NSA_HD
    kv_heads = pl.BlockSpec((None, g, tm, hd), lambda bi, i: (bi, 0, i, 0))
    kv_flat = pl.BlockSpec((None, tm, g * hd), lambda bi, i: (bi, i, 0))
    tab = pl.BlockSpec((tm, LANES), lambda bi, i: (i, 0))
    return pl.pallas_call(
        _nsa_in_kernel,
        name="nsa_in",
        grid=(b, s // tm),
        in_specs=[pl.BlockSpec((None, tm, d), lambda bi, i: (bi, i, 0)),
                  pl.BlockSpec((1, d), lambda bi, i: (0, 0)),
                  pl.BlockSpec(w.shape, lambda bi, i: (0, 0)),
                  tab, tab],
        out_specs=[pl.BlockSpec((None, NSA_HEADS, tm, hd), lambda bi, i: (bi, 0, i, 0)),
                   kv_flat, kv_flat, kv_heads, kv_heads, kv_heads, kv_heads,
                   pl.BlockSpec((None, tm, 4 * LANES), lambda bi, i: (bi, i, 0))],
        out_shape=[jax.ShapeDtypeStruct((b, NSA_HEADS, s, hd), BF16),
                   jax.ShapeDtypeStruct((b, s, g * hd), F32),
                   jax.ShapeDtypeStruct((b, s, g * hd), F32),
                   jax.ShapeDtypeStruct((b, g, s, hd), BF16),
                   jax.ShapeDtypeStruct((b, g, s, hd), BF16),
                   jax.ShapeDtypeStruct((b, g, s, hd), BF16),
                   jax.ShapeDtypeStruct((b, g, s, hd), BF16),
                   jax.ShapeDtypeStruct((b, s, 4 * LANES), F32)],
        compiler_params=_params("parallel", "parallel"),
    )(x, gain, w, cos_t, sin_t)


def _cmp_proj_kernel(x_ref, plo_ref, phi_ref, a_ref, b_ref, ya_ref, yb_ref):
    x = x_ref[...]
    ya_ref[...] = _dot((x + plo_ref[...]).astype(BF16), a_ref[...])
    yb_ref[...] = _dot((x + phi_ref[...]).astype(BF16), b_ref[...])


def _cmp_mix_kernel(ya_ref, yb_ref, w2_ref, o_ref):
    n = ya_ref.shape[0]
    z = ya_ref[...] + pltpu.roll(yb_ref[...], n - 1, 0)
    h = (z * jax.nn.sigmoid(z)).astype(BF16)
    out = _dot(h, w2_ref[...])
    for g in range(NSA_GROUPS):
        o_ref[g] = out[:, NSA_HD * g:NSA_HD * (g + 1)].astype(o_ref.dtype)


def _compress(kv, pos, w1, w2, tc=256):
    b, s, _ = kv.shape
    g, hd = NSA_GROUPS, NSA_HD
    n = s // CMP_STRIDE
    wide = CMP_STRIDE * g * hd
    tc = min(tc, n)
    x = kv.reshape(b, n, wide)
    eye = jnp.eye(g, dtype=F32)
    w1r = w1.reshape(CMP_BLOCK, hd, hd)
    mat_a = jnp.einsum("pde,gh->pgdhe", w1r[:CMP_STRIDE], eye).reshape(wide, g * hd).astype(BF16)
    mat_b = jnp.einsum("pde,gh->pgdhe", w1r[CMP_STRIDE:], eye).reshape(wide, g * hd).astype(BF16)
    p_lo = jnp.broadcast_to(pos[:CMP_STRIDE, None, :], (CMP_STRIDE, g, hd)).reshape(1, wide)
    p_hi = jnp.broadcast_to(pos[CMP_STRIDE:, None, :], (CMP_STRIDE, g, hd)).reshape(1, wide)
    w2bd = jnp.einsum("de,gh->gdhe", w2, eye).reshape(g * hd, g * hd).astype(BF16)
    vec = pl.BlockSpec((1, wide), lambda bi, i: (0, 0))
    mat = pl.BlockSpec((wide, g * hd), lambda bi, i: (0, 0))
    yspec = pl.BlockSpec((None, tc, g * hd), lambda bi, i: (bi, i, 0))
    ya, yb = pl.pallas_call(
        _cmp_proj_kernel,
        name="cmp_proj",
        grid=(b, n // tc),
        in_specs=[pl.BlockSpec((None, tc, wide), lambda bi, i: (bi, i, 0)), vec, vec, mat, mat],
        out_specs=[yspec, yspec],
        out_shape=[jax.ShapeDtypeStruct((b, n, g * hd), F32)] * 2,
        compiler_params=_params("parallel", "parallel"),
    )(x, p_lo, p_hi, mat_a, mat_b)
    full = pl.BlockSpec((None, n, g * hd), lambda bi: (bi, 0, 0))
    return pl.pallas_call(
        _cmp_mix_kernel,
        name="cmp_mix",
        grid=(b,),
        in_specs=[full, full, pl.BlockSpec((g * hd, g * hd), lambda bi: (0, 0))],
        out_specs=pl.BlockSpec((None, g, n, hd), lambda bi: (bi, 0, 0, 0)),
        out_shape=jax.ShapeDtypeStruct((b, g, n, hd), BF16),
        compiler_params=_params("parallel"),
    )(ya, yb, w2bd)


def _nsa_cw_kernel(q_ref, kc_ref, vc_ref, kw_ref, vw_ref, gates_ref, amat_ref, ocw_ref, idx_ref, *, chunk):
    t0 = pl.program_id(2) * Q_BLOCK
    n_chunk = kc_ref.shape[0] // chunk
    last_cmp = (t0 + Q_BLOCK - CMP_BLOCK) // CMP_STRIDE
    need = jnp.minimum(last_cmp // chunk + 1, n_chunk)
    for nc in range(1, n_chunk + 1):
        pl.when(need == nc)(functools.partial(
            _nsa_cw_tile, q_ref, kc_ref, vc_ref, kw_ref, vw_ref, gates_ref, amat_ref, ocw_ref, idx_ref,
            n_cmp=nc * chunk, n_sel=nc * chunk * CMP_STRIDE // SEL_BLOCK))


def _nsa_cw_tile(q_ref, kc_ref, vc_ref, kw_ref, vw_ref, gates_ref, amat_ref, ocw_ref, idx_ref, *, n_cmp, n_sel):
    qb = Q_BLOCK
    rows = NSA_HPG * qb
    t0 = pl.program_id(2) * qb
    q = q_ref[...].reshape(rows, NSA_HD)
    t_row = t0 + lax.broadcasted_iota(jnp.int32, (rows, 1), 0) % qb

    s = _dot_nt(q, kc_ref[0:n_cmp, :])
    cmp_end = lax.broadcasted_iota(jnp.int32, (1, n_cmp), 1) * CMP_STRIDE + (CMP_BLOCK - 1)
    p = _softmax_rows(jnp.where(cmp_end <= t_row, s, NEG_INF))
    o_c = _dot(p.astype(BF16), vc_ref[0:n_cmp, :])
    p_grp = p[0:qb] + p[qb:2 * qb] + p[2 * qb:3 * qb] + p[3 * qb:4 * qb]
    p_hi = p_grp.astype(BF16)
    p_lo = (p_grp - p_hi.astype(F32)).astype(BF16)
    amat = amat_ref[0:n_sel, 0:n_cmp]
    imp = _dot_nt(amat, p_hi) + _dot_nt(amat, p_lo)

    t_tok = t0 + lax.broadcasted_iota(jnp.int32, (1, qb), 1)
    cur = (t_tok // SEL_BLOCK).astype(F32)
    blk = lax.broadcasted_iota(jnp.int32, (n_sel, 1), 0).astype(F32)
    forced = (blk == 0.0) | (blk == cur - 1.0)
    score = jnp.where(forced, jnp.inf, jnp.where(blk < cur, imp, NEG_INF))
    score = jnp.where(blk == cur, NEG_INF, score)
    slot = lax.broadcasted_iota(jnp.int32, (N_SELECT, qb), 0)
    sel = jnp.where(slot == 0, cur, 0.0)
    for n in range(1, N_SELECT):
        m = jnp.max(score, axis=0, keepdims=True)
        pick = jnp.min(jnp.where(score == m, blk, float(n_sel)), axis=0, keepdims=True)
        score = jnp.where(blk == pick, NEG_INF, score)
        sel = jnp.where(slot == n, pick, sel)
    idx_ref[...] = sel.astype(jnp.int32)

    w_len = WINDOW + qb
    w_start = pl.multiple_of(jnp.maximum(t0 - WINDOW, 0), qb)
    s_w = _dot_nt(q, kw_ref[pl.ds(w_start, w_len), :])
    kpos = w_start + lax.broadcasted_iota(jnp.int32, (1, w_len), 1)
    mask_w = (kpos <= t_row) & (kpos > t_row - WINDOW)
    p_w = _softmax_rows(jnp.where(mask_w, s_w, NEG_INF))
    o_w = _dot(p_w.astype(BF16), vw_ref[pl.ds(w_start, w_len), :])

    gates = gates_ref[...]
    outs = []
    for h in range(NSA_HPG):
        g_c = gates[:, 3 * h:3 * h + 1]
        g_w = gates[:, 3 * h + 2:3 * h + 3]
        outs.append(g_c * o_c[h * qb:(h + 1) * qb] + g_w * o_w[h * qb:(h + 1) * qb])
    ocw_ref[...] = jnp.concatenate(outs, axis=1)


def _nsa_cmp_win(q, k_cmp, v_cmp, k_w, v_w, gates, amat, chunk=2 * LANES):
    b, _, s, hd = q.shape
    g = NSA_GROUPS
    n_cmp = k_cmp.shape[2]
    cmp_spec = pl.BlockSpec((None, None, n_cmp, hd), lambda bi, gi, j: (bi, gi, 0, 0))
    win_spec = pl.BlockSpec((None, None, s, hd), lambda bi, gi, j: (bi, gi, 0, 0))
    chunk = min(n_cmp, chunk)
    assert n_cmp % chunk == 0
    return pl.pallas_call(
        functools.partial(_nsa_cw_kernel, chunk=chunk),
        name="nsa_cw",
        grid=(b, g, s // Q_BLOCK),
        in_specs=[pl.BlockSpec((None, NSA_HPG, Q_BLOCK, hd), lambda bi, gi, j: (bi, gi, j, 0)),
                  cmp_spec, cmp_spec, win_spec, win_spec,
                  pl.BlockSpec((None, Q_BLOCK, LANES), lambda bi, gi, j: (bi, j, gi)),
                  pl.BlockSpec(amat.shape, lambda bi, gi, j: (0, 0))],
        out_specs=[pl.BlockSpec((None, Q_BLOCK, NSA_HPG * hd), lambda bi, gi, j: (bi, j, gi)),
                   pl.BlockSpec((None, None, None, N_SELECT, Q_BLOCK), lambda bi, gi, j: (bi, gi, j, 0, 0))],
        out_shape=[jax.ShapeDtypeStruct((b, s, NSA_HEADS * hd), F32),
                   jax.ShapeDtypeStruct((b, g, s // Q_BLOCK, N_SELECT, Q_BLOCK), jnp.int32)],
        compiler_params=_params("parallel", "parallel", "arbitrary"),
    )(q, k_cmp, v_cmp, k_w, v_w, gates, amat)


def _roll_rows(x, shift):
    return jnp.concatenate([pltpu.roll(x[r:r + 8], shift, 0) for r in range(0, x.shape[0], 8)], axis=0)


def _nsa_sel_kernel(idx_ref, q2_ref, kv2_ref, o_ref, s_a, s_b, v_a, v_b, *, tq, pairs):
    half_blk = SEL_BLOCK // 2
    n_col = N_SELECT * half_blk
    t0 = pl.program_id(2) * tq
    row = lax.broadcasted_iota(jnp.int32, (16, n_col), 0)
    col = lax.broadcasted_iota(jnp.int32, (16, n_col), 1)
    pos0 = 2 * col + (row % 8) // NSA_HPG
    in_slot0 = col < half_blk
    second = (row >= 8).astype(jnp.int32)
    row8 = lax.broadcasted_iota(jnp.int32, (8, LANES), 0)

    def stage_scores(grp, s_buf, v_buf):
        for u in range(pairs):
            tp = grp * pairs + u
            q16 = q2_ref[pl.ds(pl.multiple_of(16 * tp, 16), 16), :]
            halves = []
            for w in range(2):
                tok = 2 * tp + w
                base = (tok // Q_BLOCK) * (N_SELECT * Q_BLOCK) + tok % Q_BLOCK
                blocks = []
                for n in range(N_SELECT):
                    off = pl.multiple_of(idx_ref[0, base + n * Q_BLOCK] * half_blk, half_blk)
                    blocks.append(kv2_ref[pl.ds(off, half_blk), :])
                keys = jnp.concatenate([blk[:, :LANES] for blk in blocks], axis=0)
                v_buf[(2 * u + w) * n_col:(2 * u + w + 1) * n_col, :] = jnp.concatenate(
                    [blk[:, LANES:] for blk in blocks], axis=0)
                halves.append(_dot_nt(q16, keys)[8 * w:8 * w + 8])
            s_buf[16 * u:16 * u + 16, :] = jnp.concatenate(halves, axis=0)

    def stage_output(grp, s_buf, v_buf):
        for u in range(pairs):
            tp = grp * pairs + u
            ta = t0 + 2 * tp
            n_valid = jnp.minimum(ta // SEL_BLOCK + 1, N_SELECT)
            ok = jnp.where(in_slot0, pos0 - (ta % SEL_BLOCK + second), col - n_valid * half_blk + 1) <= 0
            s = jnp.where(ok, s_buf[16 * u:16 * u + 16, :], NEG_INF)
            m = jnp.max(s, axis=-1, keepdims=True)
            m = jnp.maximum(m, _roll_rows(m, NSA_HPG))
            e = jnp.exp(s - m)
            l = jnp.sum(e, axis=-1, keepdims=True)
            l = l + _roll_rows(l, NSA_HPG)
            p = (e / l).astype(BF16)
            xa = _dot(p, v_buf[(2 * u) * n_col:(2 * u + 1) * n_col, :])
            xb = _dot(p, v_buf[(2 * u + 1) * n_col:(2 * u + 2) * n_col, :])
            x = jnp.concatenate([xa[:8], xb[8:]], axis=0)
            y = x + pltpu.roll(_roll_rows(x, NSA_HPG), NSA_HD, 1)
            out = jnp.where(row8 < NSA_HPG, y[:8], pltpu.roll(y[8:], NSA_HPG, 0))
            o_ref[pl.ds(pl.multiple_of(8 * tp, 8), 8), :] = out

    n_grp = tq // (2 * pairs)
    stage_scores(0, s_a, v_a)

    def body(k, carry):
        stage_scores(2 * k + 1, s_b, v_b)
        stage_output(2 * k, s_a, v_a)
        stage_scores(2 * k + 2, s_a, v_a)
        stage_output(2 * k + 1, s_b, v_b)
        return carry

    lax.fori_loop(0, n_grp // 2 - 1, body, 0)
    stage_scores(n_grp - 1, s_b, v_b)
    stage_output(n_grp - 2, s_a, v_a)
    stage_output(n_grp - 1, s_b, v_b)


def _nsa_selected(idx, q2, kv2, tq=512, pairs=8):
    b, g = idx.shape[:2]
    s = kv2.shape[2] * 2
    tq = min(tq, s)
    nt = s // tq
    assert (tq // (2 * pairs)) % 2 == 0 and tq % Q_BLOCK == 0
    idx_flat = idx.reshape(b * g * nt, 1, tq * N_SELECT)
    n_col = N_SELECT * SEL_BLOCK // 2
    scratch = [pltpu.VMEM((pairs * 16, n_col), F32)] * 2 + [pltpu.VMEM((pairs * 2 * n_col, LANES), BF16)] * 2
    return pl.pallas_call(
        functools.partial(_nsa_sel_kernel, tq=tq, pairs=pairs),
        scratch_shapes=scratch,
        name="nsa_sel",
        grid=(b, g, nt),
        in_specs=[pl.BlockSpec((None, 1, tq * N_SELECT), lambda bi, gi, j: ((bi * g + gi) * nt + j, 0, 0),
                               memory_space=pltpu.SMEM),
                  pl.BlockSpec((None, None, tq * 8, LANES), lambda bi, gi, j: (bi, gi, j, 0)),
                  pl.BlockSpec((None, None, s // 2, 2 * LANES), lambda bi, gi, j: (bi, gi, 0, 0))],
        out_specs=pl.BlockSpec((None, None, tq * NSA_HPG, LANES), lambda bi, gi, j: (bi, gi, j, 0)),
        out_shape=jax.ShapeDtypeStruct((b, g, s * NSA_HPG, LANES), F32),
        compiler_params=_params("parallel", "parallel", "arbitrary"),
    )(idx_flat, q2, kv2)


def _nsa_out_kernel(x_ref, ocw_ref, os0_ref, os1_ref, os2_ref, os3_ref, gates_ref, emat_ref, w_ref, gain_ref,
                    o_ref):
    tm = x_ref.shape[0]
    left = lax.broadcasted_iota(jnp.int32, (tm, LANES), 1) < NSA_HD
    tiles = []
    for os_ref in (os0_ref, os1_ref, os2_ref, os3_ref):
        heads = [os_ref[pl.ds(h, tm, stride=NSA_HPG), :] for h in range(NSA_HPG)]
        for h in range(0, NSA_HPG, 2):
            tiles.append(jnp.where(left, heads[h], pltpu.roll(heads[h + 1], NSA_HD, 1)))
    o_s = jnp.concatenate(tiles, axis=1)
    g_s = _dot_split(gates_ref[...], emat_ref[...])
    a = (ocw_ref[...] + g_s * o_s).astype(BF16)
    o_ref[...] = x_ref[...] + _rms(_dot(a, w_ref[...]), gain_ref[...])


def _nsa_out_proj(x, o_cw, o_s, gates, emat, w, gain, tm=512):
    b, s, d = x.shape
    tm = min(tm, s)
    assert s % tm == 0
    row = pl.BlockSpec((None, tm, d), lambda bi, i: (bi, i, 0))
    sel = [pl.BlockSpec((None, None, tm * NSA_HPG, LANES), functools.partial(lambda bi, i, gi: (bi, gi, i, 0), gi=gi))
           for gi in range(NSA_GROUPS)]
    return pl.pallas_call(
        _nsa_out_kernel,
        name="nsa_out",
        grid=(b, s // tm),
        in_specs=[row, row, *sel,
                  pl.BlockSpec((None, tm, gates.shape[2]), lambda bi, i: (bi, i, 0)),
                  pl.BlockSpec(emat.shape, lambda bi, i: (0, 0)),
                  pl.BlockSpec(w.shape, lambda bi, i: (0, 0)),
                  pl.BlockSpec((1, d), lambda bi, i: (0, 0))],
        out_specs=row,
        out_shape=jax.ShapeDtypeStruct((b, s, d), F32),
        compiler_params=_params("parallel", "parallel"),
    )(x, o_cw, o_s, o_s, o_s, o_s, gates, emat, w, gain)


def _proj_out_kernel(x_ref, a_ref, w_ref, gain_ref, o_ref):
    o_ref[...] = x_ref[...] + _rms(_dot(a_ref[...], w_ref[...]), gain_ref[...])


def _proj_out(x, a, w, gain, tm=1024):
    t, d = x.shape
    tm = min(tm, t)
    assert t % tm == 0
    row = pl.BlockSpec((tm, d), lambda i: (i, 0))
    return pl.pallas_call(
        _proj_out_kernel,
        name="proj_out",
        grid=(t // tm,),
        in_specs=[row, pl.BlockSpec((tm, a.shape[1]), lambda i: (i, 0)),
                  pl.BlockSpec(w.shape, lambda i: (0, 0)),
                  pl.BlockSpec((1, d), lambda i: (0, 0))],
        out_specs=row,
        out_shape=jax.ShapeDtypeStruct((t, d), F32),
        compiler_params=_params("parallel"),
    )(x, a, w, gain)


def _nsa_mixer(x, gain_pre, gain_post, w_in, w_out, cmp_pos, cmp_w1, cmp_w2):
    b, s, d = x.shape
    g, hd, hpg = NSA_GROUPS, NSA_HD, NSA_HPG
    n_gate = 3 * hpg
    q_cols = NSA_HEADS * hd
    kv_end = q_cols + 6 * NSA_KV_COLS

    gate_w = w_in[:, kv_end:].reshape(d, g, n_gate)
    gate_w = jnp.pad(gate_w, ((0, 0), (0, 0), (0, LANES - n_gate))).reshape(d, g * LANES)
    w_all = jnp.concatenate([w_in[:, :kv_end], gate_w], axis=1).astype(BF16)

    half = hd // 2
    inv = ROPE_THETA ** (-jnp.arange(half, dtype=F32) / half)
    ang = jnp.arange(s).astype(F32)[:, None] * inv[None, :]
    cos, sin = jnp.cos(ang), jnp.sin(ang)
    cos_t = jnp.tile(cos, (1, LANES // half))
    sin_t = jnp.tile(jnp.concatenate([-sin, sin], axis=1), (1, LANES // hd))

    q, k_c, v_c, k_s, v_s, k_w, v_w, gates = _nsa_in_proj(x, gain_pre, w_all, cos_t, sin_t)
    k_cmp = _compress(k_c, cmp_pos[0], cmp_w1[0], cmp_w2[0])
    v_cmp = _compress(v_c, cmp_pos[1], cmp_w1[1], cmp_w2[1])

    n_cmp_rows = s // CMP_STRIDE
    n_sel = s // SEL_BLOCK
    ratio = SEL_BLOCK // CMP_STRIDE
    c_id = jnp.arange(n_cmp_rows)[:, None]
    n_id = jnp.arange(n_sel)[None, :]
    lo = ratio * n_id - (CMP_BLOCK // CMP_STRIDE - 1)
    amat = ((c_id >= lo) & (c_id <= ratio * n_id + ratio - 1) & (c_id < n_cmp_rows - 1)).astype(BF16)

    o_cw, idx = _nsa_cmp_win(q, k_cmp, v_cmp, k_w, v_w, gates, amat.T)

    qt = q.reshape(b, g, hpg, s, hd).transpose(0, 1, 3, 2, 4)
    zero = jnp.zeros_like(qt)
    q2 = jnp.stack([jnp.concatenate([qt, zero], axis=-1), jnp.concatenate([zero, qt], axis=-1)], axis=3)
    q2 = q2.reshape(b, g, s * 2 * hpg, 2 * hd)
    kv2 = jnp.concatenate([k_s.reshape(b, g, s // 2, 2 * hd), v_s.reshape(b, g, s // 2, 2 * hd)], axis=-1)
    o_s = _nsa_selected(idx, q2, kv2)

    r_id = jnp.arange(g * LANES)[:, None]
    col = jnp.arange(q_cols)[None, :]
    emat = (r_id == (col // (hpg * hd)) * LANES + 3 * ((col // hd) % hpg) + 1).astype(BF16)

    return _nsa_out_proj(x, o_cw, o_s, gates, emat, w_out.astype(BF16), gain_post)


def _hgrn_in_kernel(x_ref, gain_ref, w_ref, lb_ref, q_ref, k_ref, lf_ref, v_ref, gs_ref):
    xn = _rms(x_ref[...], gain_ref[...]).astype(BF16)
    n_chunk = D_MODEL // 256

    def chunk(c):
        return _dot(xn, w_ref[:, 256 * c:256 * (c + 1)])

    for c in range(n_chunk):
        cols = slice(256 * c, 256 * (c + 1))
        r = chunk(c)
        q_ref[:, cols] = r * jax.nn.sigmoid(r)
        f = chunk(n_chunk + c)
        lb = lb_ref[:, cols]
        log_sig = jnp.minimum(f, 0.0) - jnp.log1p(jnp.exp(-jnp.abs(f)))
        a = jnp.log(lb)
        bb = jnp.log1p(-lb) + log_sig
        log_f = jnp.maximum(a, bb) + jnp.log1p(jnp.exp(-jnp.abs(a - bb)))
        lf_ref[:, cols] = log_f
        k_ref[:, cols] = 1.0 - jnp.exp(log_f)
        v_ref[:, cols] = chunk(2 * n_chunk + c)
        r = chunk(3 * n_chunk + c)
        gs_ref[:, cols] = r * jax.nn.sigmoid(r)


def _hgrn_in_proj(x, gain, w, lb, tm=512):
    t, d = x.shape
    tm = min(tm, t)
    assert t % tm == 0
    row = pl.BlockSpec((tm, d), lambda i: (i, 0))
    vec = pl.BlockSpec((1, d), lambda i: (0, 0))
    return pl.pallas_call(
        _hgrn_in_kernel,
        name="hgrn_in",
        grid=(t // tm,),
        in_specs=[row, vec, pl.BlockSpec(w.shape, lambda i: (0, 0)), vec],
        out_specs=[row] * 5,
        out_shape=[jax.ShapeDtypeStruct((t, d), F32)] * 5,
        compiler_params=_params("parallel"),
    )(x, gain, w, lb)


def _hgrn_scan_kernel(q_ref, k_ref, lf_ref, v_ref, gs_ref, gain_ref, o_ref, state_ref, *, rows, heads):
    sub = HGRN_SUB
    n_sub = rows // sub

    @pl.when(pl.program_id(2) == 0)
    def _():
        state_ref[...] = jnp.zeros_like(state_ref)

    r_id = lax.broadcasted_iota(jnp.int32, (rows, rows), 0)
    c_id = lax.broadcasted_iota(jnp.int32, (rows, rows), 1)
    tri = jnp.where(((r_id // sub) == (c_id // sub)) & (c_id <= r_id), 1.0, 0.0)
    row_id = lax.broadcasted_iota(jnp.int32, (sub, HGRN_DK), 0)

    for h in range(heads):
        lanes = slice(LANES * h, LANES * (h + 1))
        cum = jnp.dot(tri, lf_ref[:, lanes], preferred_element_type=F32, precision=lax.Precision.HIGHEST)
        q = q_ref[:, lanes]
        k = k_ref[:, lanes]
        v = v_ref[:, lanes]
        q_dec = (q * jnp.exp(cum)).astype(BF16)

        intra, update, decay = [], [], []
        for i in range(n_sub):
            sl = slice(sub * i, sub * (i + 1))
            qi, ki, vi, ci = q[sl], k[sl], v[sl], cum[sl]
            total = ci[sub - 1:sub]
            o_i = None
            for j in range(sub):
                w = jnp.exp(jnp.where(row_id >= j, ci - ci[j:j + 1], NEG_INF))
                att = jnp.sum(qi * w * ki[j:j + 1], axis=-1, keepdims=True)
                o_i = att * vi[j:j + 1] if o_i is None else o_i + att * vi[j:j + 1]
            intra.append(o_i)
            update.append(_dot_tn(vi.astype(BF16), (ki * jnp.exp(total - ci)).astype(BF16)))
            decay.append(jnp.exp(total))

        state = state_ref[h]
        outs = []
        for i in range(n_sub):
            outs.append(intra[i] + _dot_nt(q_dec[sub * i:sub * (i + 1)], state.astype(BF16)))
            state = state * decay[i] + update[i]
        state_ref[h] = state

        o = jnp.concatenate(outs, axis=0)
        o_ref[:, lanes] = (_rms(o, gain_ref[...]) * gs_ref[:, lanes]).astype(o_ref.dtype)


def _hgrn_scan(q, k, lf, v, gs, gain, rows=128, heads=4):
    b, s, d = q.shape
    rows = min(rows, s)
    blk = pl.BlockSpec((None, rows, heads * LANES), lambda bi, h, l: (bi, l, h))
    return pl.pallas_call(
        functools.partial(_hgrn_scan_kernel, rows=rows, heads=heads),
        name="hgrn_scan",
        grid=(b, d // (heads * LANES), s // rows),
        in_specs=[blk] * 5 + [pl.BlockSpec((1, LANES), lambda bi, h, l: (0, 0))],
        out_specs=blk,
        out_shape=jax.ShapeDtypeStruct((b, s, d), BF16),
        scratch_shapes=[pltpu.VMEM((heads, LANES, HGRN_DK), F32)],
        compiler_params=_params("parallel", "parallel", "arbitrary"),
    )(q, k, lf, v, gs, gain)


def _hgrn_mixer(x, gain_pre, gain_post, w_in, w_out, norm_gain, lower_bound):
    b, s, d = x.shape
    x2 = x.reshape(b * s, d)
    q, k, lf, v, gs = _hgrn_in_proj(x2, gain_pre, w_in.astype(BF16), lower_bound.reshape(1, d))
    shape = (b, s, d)
    o = _hgrn_scan(q.reshape(shape), k.reshape(shape), lf.reshape(shape), v.reshape(shape),
                   gs.reshape(shape), norm_gain.reshape(1, LANES))
    return _proj_out(x2, o.reshape(b * s, d), w_out.astype(BF16), gain_post).reshape(shape)


def kernel(x, p, norm_gains, ffn_w_in, ffn_w_out, ple_w_in, ple_w_gate, nsa_w_in, nsa_w_out, nsa_cmp_pos,
           nsa_cmp_w1, nsa_cmp_w2, hgrn_w_in, hgrn_w_out, hgrn_norm, hgrn_lb_logits):
    b, s, d = x.shape
    depth = norm_gains.shape[0]
    lb_sm = jax.nn.softmax(hgrn_lb_logits.astype(F32), axis=0)
    lower_bounds = jnp.cumsum(lb_sm, axis=0) - lb_sm[0]
    for layer in range(depth):
        ng = norm_gains[layer].reshape(8, 1, d)
        x2 = x.reshape(b * s, d)
        x2 = _ffn_block(x2, ng[0], ng[1], ffn_w_in[layer, 0].astype(BF16), ffn_w_out[layer, 0].astype(BF16))
        x = x2.reshape(b, s, d)
        j = layer // 2
        if layer % 2 == 0:
            x = _nsa_mixer(x, ng[2], ng[3], nsa_w_in[j], nsa_w_out[j], nsa_cmp_pos[j], nsa_cmp_w1[j],
                           nsa_cmp_w2[j])
        else:
            x = _hgrn_mixer(x, ng[2], ng[3], hgrn_w_in[j], hgrn_w_out[j], hgrn_norm[j], lower_bounds[layer])
        x2 = x.reshape(b * s, d)
        x2 = _ffn_block(x2, ng[4], ng[5], ffn_w_in[layer, 1].astype(BF16), ffn_w_out[layer, 1].astype(BF16))
        x2 = _ple_block(x2, p[layer].reshape(b * s, -1), ng[6], ng[7], ple_w_gate[layer].astype(BF16),
                        ple_w_in[layer].astype(BF16))
        x = x2.reshape(b, s, d)
    return x
```

```python
import functools

import jax
import jax.numpy as jnp
from jax import lax
from jax.experimental import pallas as pl
from jax.experimental.pallas import tpu as pltpu

F32 = jnp.float32
BF16 = jnp.bfloat16
NEG_INF = float("-inf")

D_MODEL = 1024
NORM_EPS = 1e-6
ROPE_THETA = 10000.0

NSA_HEADS = 16
NSA_GROUPS = 4
NSA_HPG = NSA_HEADS // NSA_GROUPS
NSA_HD = 64
NSA_KV_COLS = NSA_GROUPS * NSA_HD
CMP_BLOCK = 32
CMP_STRIDE = 16
SEL_BLOCK = 64
N_SELECT = 16
WINDOW = 512
Q_BLOCK = 128

HGRN_HEADS = 8
HGRN_DK = 128
HGRN_SUB = 16

LANES = 128
VMEM_LIMIT = 56 * 1024 * 1024


def _params(*sem):
    return pltpu.CompilerParams(dimension_semantics=sem, vmem_limit_bytes=VMEM_LIMIT)


def _rms(x, gain):
    return x * lax.rsqrt(jnp.mean(x * x, axis=-1, keepdims=True) + NORM_EPS) * gain


def _dot(a, b):
    return jnp.dot(a, b, preferred_element_type=F32)


def _dot_nt(a, b):
    return lax.dot_general(a, b, (((1,), (1,)), ((), ())), preferred_element_type=F32)


def _dot_tn(a, b):
    return lax.dot_general(a, b, (((0,), (0,)), ((), ())), preferred_element_type=F32)


def _dot_split(x, w):
    hi = x.astype(BF16)
    lo = (x - hi.astype(F32)).astype(BF16)
    return _dot(hi, w) + _dot(lo, w)


def _softmax_rows(s):
    m = jnp.max(s, axis=-1, keepdims=True)
    m = jnp.where(m == NEG_INF, 0.0, m)
    e = jnp.exp(s - m)
    return e / jnp.maximum(jnp.sum(e, axis=-1, keepdims=True), 1e-30)


def _ffn_kernel(x_ref, g0_ref, g1_ref, wg_ref, wu_ref, wo_ref, o_ref, xn_ref, acc_ref):
    j = pl.program_id(1)

    @pl.when(j == 0)
    def _():
        xn_ref[...] = _rms(x_ref[...], g0_ref[...]).astype(BF16)
        acc_ref[...] = jnp.zeros_like(acc_ref)

    xn = xn_ref[...]
    g = _dot(xn, wg_ref[...])
    u = _dot(xn, wu_ref[...])
    h = (g * jax.nn.sigmoid(g) * u).astype(BF16)
    acc_ref[...] += _dot(h, wo_ref[...])

    @pl.when(j == pl.num_programs(1) - 1)
    def _():
        o_ref[...] = x_ref[...] + 0.5 * _rms(acc_ref[...], g1_ref[...])


def _ffn_block(x, g_pre, g_post, w_in, w_out, tm=1024, tf=1408):
    t, d = x.shape
    dff = w_out.shape[0]
    tm = min(tm, t)
    assert t % tm == 0
    assert dff % tf == 0
    nf = dff // tf
    row = pl.BlockSpec((tm, d), lambda i, j: (i, 0))
    vec = pl.BlockSpec((1, d), lambda i, j: (0, 0))
    return pl.pallas_call(
        _ffn_kernel,
        name="ffn",
        grid=(t // tm, nf),
        in_specs=[row, vec, vec,
                  pl.BlockSpec((d, tf), lambda i, j: (0, j)),
                  pl.BlockSpec((d, tf), lambda i, j: (0, j + nf)),
                  pl.BlockSpec((tf, d), lambda i, j: (j, 0))],
        out_specs=row,
        out_shape=jax.ShapeDtypeStruct((t, d), F32),
        scratch_shapes=[pltpu.VMEM((tm, d), BF16), pltpu.VMEM((tm, d), F32)],
        compiler_params=_params("parallel", "arbitrary"),
    )(x, g_pre, g_post, w_in, w_in, w_out)


def _ple_kernel(x_ref, p_ref, g0_ref, g1_ref, wg_ref, wp_ref, o_ref):
    x = x_ref[...]
    xn = _rms(x, g0_ref[...]).astype(BF16)
    gate = jax.nn.sigmoid(_dot(xn, wg_ref[...]))
    e = _dot(p_ref[...].astype(BF16), wp_ref[...])
    o_ref[...] = x + _rms(e * gate, g1_ref[...])


def _ple_block(x, p, g_pre, g_post, w_gate, w_p, tm=1024):
    t, d = x.shape
    pd = p.shape[1]
    tm = min(tm, t)
    assert t % tm == 0
    row = pl.BlockSpec((tm, d), lambda i: (i, 0))
    vec = pl.BlockSpec((1, d), lambda i: (0, 0))
    return pl.pallas_call(
        _ple_kernel,
        name="ple",
        grid=(t // tm,),
        in_specs=[row, pl.BlockSpec((tm, pd), lambda i: (i, 0)), vec, vec,
                  pl.BlockSpec((d, d), lambda i: (0, 0)),
                  pl.BlockSpec((pd, d), lambda i: (0, 0))],
        out_specs=row,
        out_shape=jax.ShapeDtypeStruct((t, d), F32),
        compiler_params=_params("parallel"),
    )(x, p, g_pre, g_post, w_gate, w_p)


def _rope_tile(xb, cos_t, sin_t, first_half):
    fwd = pltpu.roll(xb, LANES - NSA_HD // 2, 1)
    bwd = pltpu.roll(xb, NSA_HD // 2, 1)
    return xb * cos_t + jnp.where(first_half, fwd, bwd) * sin_t


def _nsa_in_kernel(x_ref, gain_ref, w_ref, cos_ref, sin_ref,
                   q_ref, kc_ref, vc_ref, ks_ref, vs_ref, kw_ref, vw_ref, gates_ref):
    xn = _rms(x_ref[...], gain_ref[...]).astype(BF16)
    cos_t = cos_ref[...]
    sin_t = sin_ref[...]
    tm = cos_t.shape[0]
    lane = lax.broadcasted_iota(jnp.int32, (tm, LANES), 1)
    first_half = (lane % NSA_HD) < (NSA_HD // 2)

    def chunk(c):
        return _dot(xn, w_ref[:, 256 * c:256 * (c + 1)])

    def roped(r):
        return [_rope_tile(r[:, LANES * h:LANES * (h + 1)], cos_t, sin_t, first_half) for h in range(2)]

    def store_heads(ref, base, tiles, scale):
        for h, tile in enumerate(tiles):
            ref[base + 2 * h] = (tile[:, :NSA_HD] * scale).astype(ref.dtype)
            ref[base + 2 * h + 1] = (tile[:, NSA_HD:] * scale).astype(ref.dtype)

    for c in range(4):
        store_heads(q_ref, 4 * c, roped(chunk(c)), NSA_HD ** -0.5)
    kc = roped(chunk(4))
    kc_ref[...] = jnp.concatenate(kc, axis=1)
    vc_ref[...] = chunk(5)
    store_heads(ks_ref, 0, roped(chunk(6)), 1.0)
    r = chunk(7)
    store_heads(vs_ref, 0, [r[:, :LANES], r[:, LANES:]], 1.0)
    store_heads(kw_ref, 0, roped(chunk(8)), 1.0)
    r = chunk(9)
    store_heads(vw_ref, 0, [r[:, :LANES], r[:, LANES:]], 1.0)
    gates_ref[:, :256] = jax.nn.sigmoid(chunk(10))
    gates_ref[:, 256:] = jax.nn.sigmoid(chunk(11))


def _nsa_in_proj(x, gain, w, cos_t, sin_t, tm=512):
    b, s, d = x.shape
    tm = min(tm, s)
    assert s % tm == 0
    g, hd = NSA_GROUPS, NSA_HD
    kv_heads = pl.BlockSpec((None, g, tm, hd), lambda bi, i: (bi, 0, i, 0))
    kv_flat = pl.BlockSpec((None, tm, g * hd), lambda bi, i: (bi, i, 0))
    tab = pl.BlockSpec((tm, LANES), lambda bi, i: (i, 0))
    return pl.pallas_call(
        _nsa_in_kernel,
        name="nsa_in",
        grid=(b, s // tm),
        in_specs=[pl.BlockSpec((None, tm, d), lambda bi, i: (bi, i, 0)),
                  pl.BlockSpec((1, d), lambda bi, i: (0, 0)),
                  pl.BlockSpec(w.shape, lambda bi, i: (0, 0)),
                  tab, tab],
        out_specs=[pl.BlockSpec((None, NSA_HEADS, tm, hd), lambda bi, i: (bi, 0, i, 0)),
                   kv_flat, kv_flat, kv_heads, kv_heads, kv_heads, kv_heads,
                   pl.BlockSpec((None, tm, 4 * LANES), lambda bi, i: (bi, i, 0))],
        out_shape=[jax.ShapeDtypeStruct((b, NSA_HEADS, s, hd), BF16),
                   jax.ShapeDtypeStruct((b, s, g * hd), F32),
                   jax.ShapeDtypeStruct((b, s, g * hd), F32),
                   jax.ShapeDtypeStruct((b, g, s, hd), BF16),
                   jax.ShapeDtypeStruct((b, g, s, hd), BF16),
                   jax.ShapeDtypeStruct((b, g, s, hd), BF16),
                   jax.ShapeDtypeStruct((b, g, s, hd), BF16),
                   jax.ShapeDtypeStruct((b, s, 4 * LANES), F32)],
        compiler_params=_params("parallel", "parallel"),
    )(x, gain, w, cos_t, sin_t)


def _cmp_proj_kernel(x_ref, plo_ref, phi_ref, a_ref, b_ref, ya_ref, yb_ref):
    x = x_ref[...]
    ya_ref[...] = _dot((x + plo_ref[...]).astype(BF16), a_ref[...])
    yb_ref[...] = _dot((x + phi_ref[...]).astype(BF16), b_ref[...])


def _cmp_mix_kernel(ya_ref, yb_ref, w2_ref, o_ref):
    n = ya_ref.shape[0]
    z = ya_ref[...] + pltpu.roll(yb_ref[...], n - 1, 0)
    h = (z * jax.nn.sigmoid(z)).astype(BF16)
    out = _dot(h, w2_ref[...])
    for g in range(NSA_GROUPS):
        o_ref[g] = out[:, NSA_HD * g:NSA_HD * (g + 1)].astype(o_ref.dtype)


def _compress(kv, pos, w1, w2, tc=256):
    b, s, _ = kv.shape
    g, hd = NSA_GROUPS, NSA_HD
    n = s // CMP_STRIDE
    wide = CMP_STRIDE * g * hd
    tc = min(tc, n)
    x = kv.reshape(b, n, wide)
    eye = jnp.eye(g, dtype=F32)
    w1r = w1.reshape(CMP_BLOCK, hd, hd)
    mat_a = jnp.einsum("pde,gh->pgdhe", w1r[:CMP_STRIDE], eye).reshape(wide, g * hd).astype(BF16)
    mat_b = jnp.einsum("pde,gh->pgdhe", w1r[CMP_STRIDE:], eye).reshape(wide, g * hd).astype(BF16)
    p_lo = jnp.broadcast_to(pos[:CMP_STRIDE, None, :], (CMP_STRIDE, g, hd)).reshape(1, wide)
    p_hi = jnp.broadcast_to(pos[CMP_STRIDE:, None, :], (CMP_STRIDE, g, hd)).reshape(1, wide)
    w2bd = jnp.einsum("de,gh->gdhe", w2, eye).reshape(g * hd, g * hd).astype(BF16)
    vec = pl.BlockSpec((1, wide), lambda bi, i: (0, 0))
    mat = pl.BlockSpec((wide, g * hd), lambda bi, i: (0, 0))
    yspec = pl.BlockSpec((None, tc, g * hd), lambda bi, i: (bi, i, 0))
    ya, yb = pl.pallas_call(
        _cmp_proj_kernel,
        name="cmp_proj",
        grid=(b, n // tc),
        in_specs=[pl.BlockSpec((None, tc, wide), lambda bi, i: (bi, i, 0)), vec, vec, mat, mat],
        out_specs=[yspec, yspec],
        out_shape=[jax.ShapeDtypeStruct((b, n, g * hd), F32)] * 2,
        compiler_params=_params("parallel", "parallel"),
    )(x, p_lo, p_hi, mat_a, mat_b)
    full = pl.BlockSpec((None, n, g * hd), lambda bi: (bi, 0, 0))
    return pl.pallas_call(
        _cmp_mix_kernel,
        name="cmp_mix",
        grid=(b,),
        in_specs=[full, full, pl.BlockSpec((g * hd, g * hd), lambda bi: (0, 0))],
        out_specs=pl.BlockSpec((None, g, n, hd), lambda bi: (bi, 0, 0, 0)),
        out_shape=jax.ShapeDtypeStruct((b, g, n, hd), BF16),
        compiler_params=_params("parallel"),
    )(ya, yb, w2bd)


def _nsa_cw_kernel(q_ref, kc_ref, vc_ref, kw_ref, vw_ref, gates_ref, amat_ref, ocw_ref, idx_ref, *, chunk):
    t0 = pl.program_id(2) * Q_BLOCK
    n_chunk = kc_ref.shape[0] // chunk
    last_cmp = (t0 + Q_BLOCK - CMP_BLOCK) // CMP_STRIDE
    need = jnp.minimum(last_cmp // chunk + 1, n_chunk)
    for nc in range(1, n_chunk + 1):
        pl.when(need == nc)(functools.partial(
            _nsa_cw_tile, q_ref, kc_ref, vc_ref, kw_ref, vw_ref, gates_ref, amat_ref, ocw_ref, idx_ref,
            n_cmp=nc * chunk, n_sel=nc * chunk * CMP_STRIDE // SEL_BLOCK))


def _nsa_cw_tile(q_ref, kc_ref, vc_ref, kw_ref, vw_ref, gates_ref, amat_ref, ocw_ref, idx_ref, *, n_cmp, n_sel):
    qb = Q_BLOCK
    rows = NSA_HPG * qb
    t0 = pl.program_id(2) * qb
    q = q_ref[...].reshape(rows, NSA_HD)
    t_row = t0 + lax.broadcasted_iota(jnp.int32, (rows, 1), 0) % qb

    s = _dot_nt(q, kc_ref[0:n_cmp, :])
    cmp_end = lax.broadcasted_iota(jnp.int32, (1, n_cmp), 1) * CMP_STRIDE + (CMP_BLOCK - 1)
    p = _softmax_rows(jnp.where(cmp_end <= t_row, s, NEG_INF))
    o_c = _dot(p.astype(BF16), vc_ref[0:n_cmp, :])
    p_grp = p[0:qb] + p[qb:2 * qb] + p[2 * qb:3 * qb] + p[3 * qb:4 * qb]
    p_hi = p_grp.astype(BF16)
    p_lo = (p_grp - p_hi.astype(F32)).astype(BF16)
    amat = amat_ref[0:n_sel, 0:n_cmp]
    imp = _dot_nt(amat, p_hi) + _dot_nt(amat, p_lo)

    t_tok = t0 + lax.broadcasted_iota(jnp.int32, (1, qb), 1)
    cur = (t_tok // SEL_BLOCK).astype(F32)
    blk = lax.broadcasted_iota(jnp.int32, (n_sel, 1), 0).astype(F32)
    forced = (blk == 0.0) | (blk == cur - 1.0)
    score = jnp.where(forced, jnp.inf, jnp.where(blk < cur, imp, NEG_INF))
    score = jnp.where(blk == cur, NEG_INF, score)
    slot = lax.broadcasted_iota(jnp.int32, (N_SELECT, qb), 0)
    sel = jnp.where(slot == 0, cur, 0.0)
    for n in range(1, N_SELECT):
        m = jnp.max(score, axis=0, keepdims=True)
        pick = jnp.min(jnp.where(score == m, blk, float(n_sel)), axis=0, keepdims=True)
        score = jnp.where(blk == pick, NEG_INF, score)
        sel = jnp.where(slot == n, pick, sel)
    idx_ref[...] = sel.astype(jnp.int32)

    w_len = WINDOW + qb
    w_start = pl.multiple_of(jnp.maximum(t0 - WINDOW, 0), qb)
    s_w = _dot_nt(q, kw_ref[pl.ds(w_start, w_len), :])
    kpos = w_start + lax.broadcasted_iota(jnp.int32, (1, w_len), 1)
    mask_w = (kpos <= t_row) & (kpos > t_row - WINDOW)
    p_w = _softmax_rows(jnp.where(mask_w, s_w, NEG_INF))
    o_w = _dot(p_w.astype(BF16), vw_ref[pl.ds(w_start, w_len), :])

    gates = gates_ref[...]
    outs = []
    for h in range(NSA_HPG):
        g_c = gates[:, 3 * h:3 * h + 1]
        g_w = gates[:, 3 * h + 2:3 * h + 3]
        outs.append(g_c * o_c[h * qb:(h + 1) * qb] + g_w * o_w[h * qb:(h + 1) * qb])
    ocw_ref[...] = jnp.concatenate(outs, axis=1)


def _nsa_cmp_win(q, k_cmp, v_cmp, k_w, v_w, gates, amat, chunk=2 * LANES):
    b, _, s, hd = q.shape
    g = NSA_GROUPS
    n_cmp = k_cmp.shape[2]
    cmp_spec = pl.BlockSpec((None, None, n_cmp, hd), lambda bi, gi, j: (bi, gi, 0, 0))
    win_spec = pl.BlockSpec((None, None, s, hd), lambda bi, gi, j: (bi, gi, 0, 0))
    chunk = min(n_cmp, chunk)
    assert n_cmp % chunk == 0
    return pl.pallas_call(
        functools.partial(_nsa_cw_kernel, chunk=chunk),
        name="nsa_cw",
        grid=(b, g, s // Q_BLOCK),
        in_specs=[pl.BlockSpec((None, NSA_HPG, Q_BLOCK, hd), lambda bi, gi, j: (bi, gi, j, 0)),
                  cmp_spec, cmp_spec, win_spec, win_spec,
                  pl.BlockSpec((None, Q_BLOCK, LANES), lambda bi, gi, j: (bi, j, gi)),
                  pl.BlockSpec(amat.shape, lambda bi, gi, j: (0, 0))],
        out_specs=[pl.BlockSpec((None, Q_BLOCK, NSA_HPG * hd), lambda bi, gi, j: (bi, j, gi)),
                   pl.BlockSpec((None, None, None, N_SELECT, Q_BLOCK), lambda bi, gi, j: (bi, gi, j, 0, 0))],
        out_shape=[jax.ShapeDtypeStruct((b, s, NSA_HEADS * hd), F32),
                   jax.ShapeDtypeStruct((b, g, s // Q_BLOCK, N_SELECT, Q_BLOCK), jnp.int32)],
        compiler_params=_params("parallel", "parallel", "arbitrary"),
    )(q, k_cmp, v_cmp, k_w, v_w, gates, amat)


def _roll_rows(x, shift):
    return jnp.concatenate([pltpu.roll(x[r:r + 8], shift, 0) for r in range(0, x.shape[0], 8)], axis=0)


def _nsa_sel_kernel(idx_ref, q_ref, kv2_ref, o_ref, s_a, s_b, v_a, v_b, *, tq, pairs):
    half_blk = SEL_BLOCK // 2
    n_col = N_SELECT * half_blk
    t0 = pl.program_id(2) * tq
    row = lax.broadcasted_iota(jnp.int32, (16, n_col), 0)
    col = lax.broadcasted_iota(jnp.int32, (16, n_col), 1)
    pos0 = 2 * col + (row % 8) // NSA_HPG
    in_slot0 = col < half_blk
    second = (row >= 8).astype(jnp.int32)
    row8 = lax.broadcasted_iota(jnp.int32, (8, LANES), 0)

    def stage_scores(grp, s_buf, v_buf):
        for u in range(pairs):
            tp = grp * pairs + u
            if u % 2 == 0:
                q_four = q_ref[pl.ds(pl.multiple_of(8 * tp, 16), 16), :].astype(F32)
            q8 = q_four[8 * (u % 2):8 * (u % 2) + 8]
            wide = jnp.concatenate([q8, jnp.zeros_like(q8)], axis=1)
            shifted = pltpu.roll(wide, NSA_HD, 1)
            q16 = jnp.concatenate(
                [jnp.where(row8 < NSA_HPG, wide, pltpu.roll(shifted, NSA_HPG, 0)),
                 jnp.where(row8 < NSA_HPG, pltpu.roll(wide, NSA_HPG, 0), shifted)], axis=0).astype(BF16)
            halves = []
            for w in range(2):
                tok = 2 * tp + w
                base = (tok // Q_BLOCK) * (N_SELECT * Q_BLOCK) + tok % Q_BLOCK
                blocks = []
                for n in range(N_SELECT):
                    off = pl.multiple_of(idx_ref[0, base + n * Q_BLOCK] * half_blk, half_blk)
                    blocks.append(kv2_ref[pl.ds(off, half_blk), :])
                keys = jnp.concatenate([blk[:, :LANES] for blk in blocks], axis=0)
                v_buf[(2 * u + w) * n_col:(2 * u + w + 1) * n_col, :] = jnp.concatenate(
                    [blk[:, LANES:] for blk in blocks], axis=0)
                halves.append(_dot_nt(q16, keys)[8 * w:8 * w + 8])
            s_buf[16 * u:16 * u + 16, :] = jnp.concatenate(halves, axis=0)

    def stage_output(grp, s_buf, v_buf):
        for u in range(pairs):
            tp = grp * pairs + u
            ta = t0 + 2 * tp
            n_valid = jnp.minimum(ta // SEL_BLOCK + 1, N_SELECT)
            ok = jnp.where(in_slot0, pos0 - (ta % SEL_BLOCK + second), col - n_valid * half_blk + 1) <= 0
            s = jnp.where(ok, s_buf[16 * u:16 * u + 16, :], NEG_INF)
            m = jnp.max(s, axis=-1, keepdims=True)
            m = jnp.maximum(m, _roll_rows(m, NSA_HPG))
            e = jnp.exp(s - m)
            l = jnp.sum(e, axis=-1, keepdims=True)
            l = l + _roll_rows(l, NSA_HPG)
            p = (e / l).astype(BF16)
            xa = _dot(p, v_buf[(2 * u) * n_col:(2 * u + 1) * n_col, :])
            xb = _dot(p, v_buf[(2 * u + 1) * n_col:(2 * u + 2) * n_col, :])
            x = jnp.concatenate([xa[:8], xb[8:]], axis=0)
            y = x + pltpu.roll(_roll_rows(x, NSA_HPG), NSA_HD, 1)
            out = jnp.where(row8 < NSA_HPG, y[:8], pltpu.roll(y[8:], NSA_HPG, 0))
            o_ref[pl.ds(pl.multiple_of(8 * tp, 8), 8), :] = out

    n_grp = tq // (2 * pairs)
    stage_scores(0, s_a, v_a)

    def body(k, carry):
        stage_scores(2 * k + 1, s_b, v_b)
        stage_output(2 * k, s_a, v_a)
        stage_scores(2 * k + 2, s_a, v_a)
        stage_output(2 * k + 1, s_b, v_b)
        return carry

    lax.fori_loop(0, n_grp // 2 - 1, body, 0)
    stage_scores(n_grp - 1, s_b, v_b)
    stage_output(n_grp - 2, s_a, v_a)
    stage_output(n_grp - 1, s_b, v_b)


def _nsa_selected(idx, q_rows, kv2, tq=512, pairs=8):
    b, g = idx.shape[:2]
    s = kv2.shape[2] * 2
    tq = min(tq, s)
    nt = s // tq
    assert (tq // (2 * pairs)) % 2 == 0 and tq % Q_BLOCK == 0
    idx_flat = idx.reshape(b * g * nt, 1, tq * N_SELECT)
    n_col = N_SELECT * SEL_BLOCK // 2
    scratch = [pltpu.VMEM((pairs * 16, n_col), F32)] * 2 + [pltpu.VMEM((pairs * 2 * n_col, LANES), BF16)] * 2
    return pl.pallas_call(
        functools.partial(_nsa_sel_kernel, tq=tq, pairs=pairs),
        scratch_shapes=scratch,
        name="nsa_sel",
        grid=(b, g, nt),
        in_specs=[pl.BlockSpec((None, 1, tq * N_SELECT), lambda bi, gi, j: ((bi * g + gi) * nt + j, 0, 0),
                               memory_space=pltpu.SMEM),
                  pl.BlockSpec((None, None, tq * NSA_HPG, NSA_HD), lambda bi, gi, j: (bi, gi, j, 0)),
                  pl.BlockSpec((None, None, s // 2, 2 * LANES), lambda bi, gi, j: (bi, gi, 0, 0))],
        out_specs=pl.BlockSpec((None, None, tq * NSA_HPG, LANES), lambda bi, gi, j: (bi, gi, j, 0)),
        out_shape=jax.ShapeDtypeStruct((b, g, s * NSA_HPG, LANES), F32),
        compiler_params=_params("parallel", "parallel", "arbitrary"),
    )(idx_flat, q_rows, kv2)


def _nsa_out_kernel(x_ref, ocw_ref, os0_ref, os1_ref, os2_ref, os3_ref, gates_ref, emat_ref, w_ref, gain_ref,
                    o_ref):
    tm = x_ref.shape[0]
    left = lax.broadcasted_iota(jnp.int32, (tm, LANES), 1) < NSA_HD
    tiles = []
    for os_ref in (os0_ref, os1_ref, os2_ref, os3_ref):
        heads = [os_ref[pl.ds(h, tm, stride=NSA_HPG), :] for h in range(NSA_HPG)]
        for h in range(0, NSA_HPG, 2):
            tiles.append(jnp.where(left, heads[h], pltpu.roll(heads[h + 1], NSA_HD, 1)))
    o_s = jnp.concatenate(tiles, axis=1)
    g_s = _dot_split(gates_ref[...], emat_ref[...])
    a = (ocw_ref[...] + g_s * o_s).astype(BF16)
    o_ref[...] = x_ref[...] + _rms(_dot(a, w_ref[...]), gain_ref[...])


def _nsa_out_proj(x, o_cw, o_s, gates, emat, w, gain, tm=512):
    b, s, d = x.shape
    tm = min(tm, s)
    assert s % tm == 0
    row = pl.BlockSpec((None, tm, d), lambda bi, i: (bi, i, 0))
    sel = [pl.BlockSpec((None, None, tm * NSA_HPG, LANES), functools.partial(lambda bi, i, gi: (bi, gi, i, 0), gi=gi))
           for gi in range(NSA_GROUPS)]
    return pl.pallas_call(
        _nsa_out_kernel,
        name="nsa_out",
        grid=(b, s // tm),
        in_specs=[row, row, *sel,
                  pl.BlockSpec((None, tm, gates.shape[2]), lambda bi, i: (bi, i, 0)),
                  pl.BlockSpec(emat.shape, lambda bi, i: (0, 0)),
                  pl.BlockSpec(w.shape, lambda bi, i: (0, 0)),
                  pl.BlockSpec((1, d), lambda bi, i: (0, 0))],
        out_specs=row,
        out_shape=jax.ShapeDtypeStruct((b, s, d), F32),
        compiler_params=_params("parallel", "parallel"),
    )(x, o_cw, o_s, o_s, o_s, o_s, gates, emat, w, gain)


def _proj_out_kernel(x_ref, a_ref, w_ref, gain_ref, o_ref):
    o_ref[...] = x_ref[...] + _rms(_dot(a_ref[...], w_ref[...]), gain_ref[...])


def _proj_out(x, a, w, gain, tm=1024):
    t, d = x.shape
    tm = min(tm, t)
    assert t % tm == 0
    row = pl.BlockSpec((tm, d), lambda i: (i, 0))
    return pl.pallas_call(
        _proj_out_kernel,
        name="proj_out",
        grid=(t // tm,),
        in_specs=[row, pl.BlockSpec((tm, a.shape[1]), lambda i: (i, 0)),
                  pl.BlockSpec(w.shape, lambda i: (0, 0)),
                  pl.BlockSpec((1, d), lambda i: (0, 0))],
        out_specs=row,
        out_shape=jax.ShapeDtypeStruct((t, d), F32),
        compiler_params=_params("parallel"),
    )(x, a, w, gain)


def _nsa_mixer(x, gain_pre, gain_post, w_in, w_out, cmp_pos, cmp_w1, cmp_w2):
    b, s, d = x.shape
    g, hd, hpg = NSA_GROUPS, NSA_HD, NSA_HPG
    n_gate = 3 * hpg
    q_cols = NSA_HEADS * hd
    kv_end = q_cols + 6 * NSA_KV_COLS

    gate_w = w_in[:, kv_end:].reshape(d, g, n_gate)
    gate_w = jnp.pad(gate_w, ((0, 0), (0, 0), (0, LANES - n_gate))).reshape(d, g * LANES)
    w_all = jnp.concatenate([w_in[:, :kv_end], gate_w], axis=1).astype(BF16)

    half = hd // 2
    inv = ROPE_THETA ** (-jnp.arange(half, dtype=F32) / half)
    ang = jnp.arange(s).astype(F32)[:, None] * inv[None, :]
    cos, sin = jnp.cos(ang), jnp.sin(ang)
    cos_t = jnp.tile(cos, (1, LANES // half))
    sin_t = jnp.tile(jnp.concatenate([-sin, sin], axis=1), (1, LANES // hd))

    q, k_c, v_c, k_s, v_s, k_w, v_w, gates = _nsa_in_proj(x, gain_pre, w_all, cos_t, sin_t)
    k_cmp = _compress(k_c, cmp_pos[0], cmp_w1[0], cmp_w2[0])
    v_cmp = _compress(v_c, cmp_pos[1], cmp_w1[1], cmp_w2[1])

    n_cmp_rows = s // CMP_STRIDE
    n_sel = s // SEL_BLOCK
    ratio = SEL_BLOCK // CMP_STRIDE
    c_id = jnp.arange(n_cmp_rows)[:, None]
    n_id = jnp.arange(n_sel)[None, :]
    lo = ratio * n_id - (CMP_BLOCK // CMP_STRIDE - 1)
    amat = ((c_id >= lo) & (c_id <= ratio * n_id + ratio - 1) & (c_id < n_cmp_rows - 1)).astype(BF16)

    o_cw, idx = _nsa_cmp_win(q, k_cmp, v_cmp, k_w, v_w, gates, amat.T)

    q_rows = q.reshape(b, g, hpg, s, hd).transpose(0, 1, 3, 2, 4).reshape(b, g, s * hpg, hd)
    kv2 = jnp.concatenate([k_s.reshape(b, g, s // 2, 2 * hd), v_s.reshape(b, g, s // 2, 2 * hd)], axis=-1)
    o_s = _nsa_selected(idx, q_rows, kv2)

    r_id = jnp.arange(g * LANES)[:, None]
    col = jnp.arange(q_cols)[None, :]
    emat = (r_id == (col // (hpg * hd)) * LANES + 3 * ((col // hd) % hpg) + 1).astype(BF16)

    return _nsa_out_proj(x, o_cw, o_s, gates, emat, w_out.astype(BF16), gain_post)


def _hgrn_in_kernel(x_ref, gain_ref, w_ref, lb_ref, q_ref, k_ref, lf_ref, v_ref, gs_ref):
    xn = _rms(x_ref[...], gain_ref[...]).astype(BF16)
    n_chunk = D_MODEL // 256

    def chunk(c):
        return _dot(xn, w_ref[:, 256 * c:256 * (c + 1)])

    for c in range(n_chunk):
        cols = slice(256 * c, 256 * (c + 1))
        r = chunk(c)
        q_ref[:, cols] = r * jax.nn.sigmoid(r)
        f = chunk(n_chunk + c)
        lb = lb_ref[:, cols]
        log_sig = jnp.minimum(f, 0.0) - jnp.log1p(jnp.exp(-jnp.abs(f)))
        a = jnp.log(lb)
        bb = jnp.log1p(-lb) + log_sig
        log_f = jnp.maximum(a, bb) + jnp.log1p(jnp.exp(-jnp.abs(a - bb)))
        lf_ref[:, cols] = log_f
        k_ref[:, cols] = 1.0 - jnp.exp(log_f)
        v_ref[:, cols] = chunk(2 * n_chunk + c)
        r = chunk(3 * n_chunk + c)
        gs_ref[:, cols] = r * jax.nn.sigmoid(r)


def _hgrn_in_proj(x, gain, w, lb, tm=512):
    t, d = x.shape
    tm = min(tm, t)
    assert t % tm == 0
    row = pl.BlockSpec((tm, d), lambda i: (i, 0))
    vec = pl.BlockSpec((1, d), lambda i: (0, 0))
    return pl.pallas_call(
        _hgrn_in_kernel,
        name="hgrn_in",
        grid=(t // tm,),
        in_specs=[row, vec, pl.BlockSpec(w.shape, lambda i: (0, 0)), vec],
        out_specs=[row] * 5,
        out_shape=[jax.ShapeDtypeStruct((t, d), F32)] * 5,
        compiler_params=_params("parallel"),
    )(x, gain, w, lb)


def _hgrn_scan_kernel(q_ref, k_ref, lf_ref, v_ref, gs_ref, gain_ref, o_ref, state_ref, *, rows, heads):
    sub = HGRN_SUB
    n_sub = rows // sub

    @pl.when(pl.program_id(2) == 0)
    def _():
        state_ref[...] = jnp.zeros_like(state_ref)

    r_id = lax.broadcasted_iota(jnp.int32, (rows, rows), 0)
    c_id = lax.broadcasted_iota(jnp.int32, (rows, rows), 1)
    tri = jnp.where(((r_id // sub) == (c_id // sub)) & (c_id <= r_id), 1.0, 0.0)
    row_id = lax.broadcasted_iota(jnp.int32, (sub, HGRN_DK), 0)

    for h in range(heads):
        lanes = slice(LANES * h, LANES * (h + 1))
        cum = jnp.dot(tri, lf_ref[:, lanes], preferred_element_type=F32, precision=lax.Precision.HIGHEST)
        q = q_ref[:, lanes]
        k = k_ref[:, lanes]
        v = v_ref[:, lanes]
        q_dec = (q * jnp.exp(cum)).astype(BF16)

        intra, update, decay = [], [], []
        for i in range(n_sub):
            sl = slice(sub * i, sub * (i + 1))
            qi, ki, vi, ci = q[sl], k[sl], v[sl], cum[sl]
            total = ci[sub - 1:sub]
            o_i = None
            for j in range(sub):
                w = jnp.exp(jnp.where(row_id >= j, ci - ci[j:j + 1], NEG_INF))
                att = jnp.sum(qi * w * ki[j:j + 1], axis=-1, keepdims=True)
                o_i = att * vi[j:j + 1] if o_i is None else o_i + att * vi[j:j + 1]
            intra.append(o_i)
            update.append(_dot_tn(vi.astype(BF16), (ki * jnp.exp(total - ci)).astype(BF16)))
            decay.append(jnp.exp(total))

        state = state_ref[h]
        outs = []
        for i in range(n_sub):
            outs.append(intra[i] + _dot_nt(q_dec[sub * i:sub * (i + 1)], state.astype(BF16)))
            state = state * decay[i] + update[i]
        state_ref[h] = state

        o = jnp.concatenate(outs, axis=0)
        o_ref[:, lanes] = (_rms(o, gain_ref[...]) * gs_ref[:, lanes]).astype(o_ref.dtype)


def _hgrn_scan(q, k, lf, v, gs, gain, rows=128, heads=4):
    b, s, d = q.shape
    rows = min(rows, s)
    blk = pl.BlockSpec((None, rows, heads * LANES), lambda bi, h, l: (bi, l, h))
    return pl.pallas_call(
        functools.partial(_hgrn_scan_kernel, rows=rows, heads=heads),
        name="hgrn_scan",
        grid=(b, d // (heads * LANES), s // rows),
        in_specs=[blk] * 5 + [pl.BlockSpec((1, LANES), lambda bi, h, l: (0, 0))],
        out_specs=blk,
        out_shape=jax.ShapeDtypeStruct((b, s, d), BF16),
        scratch_shapes=[pltpu.VMEM((heads, LANES, HGRN_DK), F32)],
        compiler_params=_params("parallel", "parallel", "arbitrary"),
    )(q, k, lf, v, gs, gain)


def _hgrn_mixer(x, gain_pre, gain_post, w_in, w_out, norm_gain, lower_bound):
    b, s, d = x.shape
    x2 = x.reshape(b * s, d)
    q, k, lf, v, gs = _hgrn_in_proj(x2, gain_pre, w_in.astype(BF16), lower_bound.reshape(1, d))
    shape = (b, s, d)
    o = _hgrn_scan(q.reshape(shape), k.reshape(shape), lf.reshape(shape), v.reshape(shape),
                   gs.reshape(shape), norm_gain.reshape(1, LANES))
    return _proj_out(x2, o.reshape(b * s, d), w_out.astype(BF16), gain_post).reshape(shape)


def kernel(x, p, norm_gains, ffn_w_in, ffn_w_out, ple_w_in, ple_w_gate, nsa_w_in, nsa_w_out, nsa_cmp_pos,
           nsa_cmp_w1, nsa_cmp_w2, hgrn_w_in, hgrn_w_out, hgrn_norm, hgrn_lb_logits):
    b, s, d = x.shape
    depth = norm_gains.shape[0]
    lb_sm = jax.nn.softmax(hgrn_lb_logits.astype(F32), axis=0)
    lower_bounds = jnp.cumsum(lb_sm, axis=0) - lb_sm[0]
    for layer in range(depth):
        ng = norm_gains[layer].reshape(8, 1, d)
        x2 = x.reshape(b * s, d)
        x2 = _ffn_block(x2, ng[0], ng[1], ffn_w_in[layer, 0].astype(BF16), ffn_w_out[layer, 0].astype(BF16))
        x = x2.reshape(b, s, d)
        j = layer // 2
        if layer % 2 == 0:
            x = _nsa_mixer(x, ng[2], ng[3], nsa_w_in[j], nsa_w_out[j], nsa_cmp_pos[j], nsa_cmp_w1[j],
                           nsa_cmp_w2[j])
        else:
            x = _hgrn_mixer(x, ng[2], ng[3], hgrn_w_in[j], hgrn_w_out[j], hgrn_norm[j], lower_bounds[layer])
        x2 = x.reshape(b * s, d)
        x2 = _ffn_block(x2, ng[4], ng[5], ffn_w_in[layer, 1].astype(BF16), ffn_w_out[layer, 1].astype(BF16))
        x2 = _ple_block(x2, p[layer].reshape(b * s, -1), ng[6], ng[7], ple_w_gate[layer].astype(BF16),
                        ple_w_in[layer].astype(BF16))
        x = x2.reshape(b, s, d)
    return x
```

```python
import functools

import jax
import jax.numpy as jnp
from jax import lax
from jax.experimental import pallas as pl
from jax.experimental.pallas import tpu as pltpu

F32 = jnp.float32
BF16 = jnp.bfloat16
NEG_INF = float("-inf")

D_MODEL = 1024
NORM_EPS = 1e-6
ROPE_THETA = 10000.0

NSA_HEADS = 16
NSA_GROUPS = 4
NSA_HPG = NSA_HEADS // NSA_GROUPS
NSA_HD = 64
NSA_KV_COLS = NSA_GROUPS * NSA_HD
CMP_BLOCK = 32
CMP_STRIDE = 16
SEL_BLOCK = 64
N_SELECT = 16
WINDOW = 512
Q_BLOCK = 128

HGRN_HEADS = 8
HGRN_DK = 128
HGRN_SUB = 16

LANES = 128
VMEM_LIMIT = 56 * 1024 * 1024


def _params(*sem):
    return pltpu.CompilerParams(dimension_semantics=sem, vmem_limit_bytes=VMEM_LIMIT)


def _rms(x, gain):
    return x * lax.rsqrt(jnp.mean(x * x, axis=-1, keepdims=True) + NORM_EPS) * gain


def _dot(a, b):
    return jnp.dot(a, b, preferred_element_type=F32)


def _dot_nt(a, b):
    return lax.dot_general(a, b, (((1,), (1,)), ((), ())), preferred_element_type=F32)


def _dot_tn(a, b):
    return lax.dot_general(a, b, (((0,), (0,)), ((), ())), preferred_element_type=F32)


def _dot_split(x, w):
    hi = x.astype(BF16)
    lo = (x - hi.astype(F32)).astype(BF16)
    return _dot(hi, w) + _dot(lo, w)


def _softmax_rows(s):
    m = jnp.max(s, axis=-1, keepdims=True)
    m = jnp.where(m == NEG_INF, 0.0, m)
    e = jnp.exp(s - m)
    return e / jnp.maximum(jnp.sum(e, axis=-1, keepdims=True), 1e-30)


def _ffn_kernel(x_ref, g0_ref, g1_ref, wi_ref, wo_ref, *rest, with_ple):
    x = x_ref[...]
    dff = wo_ref.shape[0]
    gu = _dot(_rms(x, g0_ref[...]).astype(BF16), wi_ref[...])
    g = gu[:, :dff]
    h = (g * jax.nn.sigmoid(g) * gu[:, dff:]).astype(BF16)
    x = x + 0.5 * _rms(_dot(h, wo_ref[...]), g1_ref[...])
    if with_ple:
        p_ref, g2_ref, g3_ref, wg_ref, wp_ref, o_ref = rest
        gate = jax.nn.sigmoid(_dot(_rms(x, g2_ref[...]).astype(BF16), wg_ref[...]))
        e = _dot(p_ref[...].astype(BF16), wp_ref[...])
        x = x + _rms(e * gate, g3_ref[...])
    else:
        o_ref, = rest
    o_ref[...] = x


def _ffn_block(x, g_pre, g_post, w_in, w_out, ple=None, tm=512):
    t, d = x.shape
    tm = min(tm, t)
    assert t % tm == 0
    row = pl.BlockSpec((tm, d), lambda i: (i, 0))
    vec = pl.BlockSpec((1, d), lambda i: (0, 0))

    def resident(w):
        return pl.BlockSpec(w.shape, lambda i: (0, 0), pipeline_mode=pl.Buffered(1))

    args = [x, g_pre, g_post, w_in, w_out]
    specs = [row, vec, vec, resident(w_in), resident(w_out)]
    if ple is not None:
        p, g2, g3, w_gate, w_p = ple
        args += [p, g2, g3, w_gate, w_p]
        specs += [pl.BlockSpec((tm, p.shape[1]), lambda i: (i, 0)), vec, vec, resident(w_gate), resident(w_p)]
    return pl.pallas_call(
        functools.partial(_ffn_kernel, with_ple=ple is not None),
        name="ffn_ple" if ple is not None else "ffn",
        grid=(t // tm,),
        in_specs=specs,
        out_specs=row,
        out_shape=jax.ShapeDtypeStruct((t, d), F32),
        compiler_params=_params("parallel"),
    )(*args)


def _rope_tile(xb, cos_t, sin_t, first_half):
    fwd = pltpu.roll(xb, LANES - NSA_HD // 2, 1)
    bwd = pltpu.roll(xb, NSA_HD // 2, 1)
    return xb * cos_t + jnp.where(first_half, fwd, bwd) * sin_t


def _nsa_in_kernel(x_ref, gain_ref, w_ref, cos_ref, sin_ref,
                   q_ref, kc_ref, vc_ref, ks_ref, vs_ref, kw_ref, vw_ref, gates_ref):
    xn = _rms(x_ref[...], gain_ref[...]).astype(BF16)
    cos_t = cos_ref[...]
    sin_t = sin_ref[...]
    tm = cos_t.shape[0]
    lane = lax.broadcasted_iota(jnp.int32, (tm, LANES), 1)
    first_half = (lane % NSA_HD) < (NSA_HD // 2)

    def chunk(c):
        return _dot(xn, w_ref[:, 256 * c:256 * (c + 1)])

    def roped(r):
        return [_rope_tile(r[:, LANES * h:LANES * (h + 1)], cos_t, sin_t, first_half) for h in range(2)]

    def store_heads(ref, base, tiles, scale):
        for h, tile in enumerate(tiles):
            ref[base + 2 * h] = (tile[:, :NSA_HD] * scale).astype(ref.dtype)
            ref[base + 2 * h + 1] = (tile[:, NSA_HD:] * scale).astype(ref.dtype)

    for c in range(4):
        store_heads(q_ref, 4 * c, roped(chunk(c)), NSA_HD ** -0.5)
    kc = roped(chunk(4))
    kc_ref[...] = jnp.concatenate(kc, axis=1)
    vc_ref[...] = chunk(5)
    store_heads(ks_ref, 0, roped(chunk(6)), 1.0)
    r = chunk(7)
    store_heads(vs_ref, 0, [r[:, :LANES], r[:, LANES:]], 1.0)
    store_heads(kw_ref, 0, roped(chunk(8)), 1.0)
    r = chunk(9)
    store_heads(vw_ref, 0, [r[:, :LANES], r[:, LANES:]], 1.0)
    gates_ref[:, :256] = jax.nn.sigmoid(chunk(10))
    gates_ref[:, 256:] = jax.nn.sigmoid(chunk(11))


def _nsa_in_proj(x, gain, w, cos_t, sin_t, tm=512):
    b, s, d = x.shape
    tm = min(tm, s)
    assert s % tm == 0
    g, hd = NSA_GROUPS, NSA_HD
    kv_heads = pl.BlockSpec((None, g, tm, hd), lambda bi, i: (bi, 0, i, 0))
    kv_flat = pl.BlockSpec((None, tm, g * hd), lambda bi, i: (bi, i, 0))
    tab = pl.BlockSpec((tm, LANES), lambda bi, i: (i, 0))
    return pl.pallas_call(
        _nsa_in_kernel,
        name="nsa_in",
        grid=(b, s // tm),
        in_specs=[pl.BlockSpec((None, tm, d), lambda bi, i: (bi, i, 0)),
                  pl.BlockSpec((1, d), lambda bi, i: (0, 0)),
                  pl.BlockSpec(w.shape, lambda bi, i: (0, 0)),
                  tab, tab],
        out_specs=[pl.BlockSpec((None, NSA_HEADS, tm, hd), lambda bi, i: (bi, 0, i, 0)),
                   kv_flat, kv_flat, kv_heads, kv_heads, kv_heads, kv_heads,
                   pl.BlockSpec((None, tm, 4 * LANES), lambda bi, i: (bi, i, 0))],
        out_shape=[jax.ShapeDtypeStruct((b, NSA_HEADS, s, hd), BF16),
                   jax.ShapeDtypeStruct((b, s, g * hd), F32),
                   jax.ShapeDtypeStruct((b, s, g * hd), F32),
                   jax.ShapeDtypeStruct((b, g, s, hd), BF16),
                   jax.ShapeDtypeStruct((b, g, s, hd), BF16),
                   jax.ShapeDtypeStruct((b, g, s, hd), BF16),
                   jax.ShapeDtypeStruct((b, g, s, hd), BF16),
                   jax.ShapeDtypeStruct((b, s, 4 * LANES), F32)],
        compiler_params=_params("parallel", "parallel"),
    )(x, gain, w, cos_t, sin_t)


def _cmp_proj_kernel(x_ref, plo_ref, phi_ref, a_ref, b_ref, ya_ref, yb_ref):
    x = x_ref[...]
    ya_ref[...] = _dot((x + plo_ref[...]).astype(BF16), a_ref[...])
    yb_ref[...] = _dot((x + phi_ref[...]).astype(BF16), b_ref[...])


def _cmp_mix_kernel(ya_ref, yb_ref, w2_ref, o_ref):
    n = ya_ref.shape[0]
    z = ya_ref[...] + pltpu.roll(yb_ref[...], n - 1, 0)
    h = (z * jax.nn.sigmoid(z)).astype(BF16)
    out = _dot(h, w2_ref[...])
    for g in range(NSA_GROUPS):
        o_ref[g] = out[:, NSA_HD * g:NSA_HD * (g + 1)].astype(o_ref.dtype)


def _compress(kv, pos, w1, w2, tc=256):
    b, s, _ = kv.shape
    g, hd = NSA_GROUPS, NSA_HD
    n = s // CMP_STRIDE
    wide = CMP_STRIDE * g * hd
    tc = min(tc, n)
    x = kv.reshape(b, n, wide)
    eye = jnp.eye(g, dtype=F32)
    w1r = w1.reshape(CMP_BLOCK, hd, hd)
    mat_a = jnp.einsum("pde,gh->pgdhe", w1r[:CMP_STRIDE], eye).reshape(wide, g * hd).astype(BF16)
    mat_b = jnp.einsum("pde,gh->pgdhe", w1r[CMP_STRIDE:], eye).reshape(wide, g * hd).astype(BF16)
    p_lo = jnp.broadcast_to(pos[:CMP_STRIDE, None, :], (CMP_STRIDE, g, hd)).reshape(1, wide)
    p_hi = jnp.broadcast_to(pos[CMP_STRIDE:, None, :], (CMP_STRIDE, g, hd)).reshape(1, wide)
    w2bd = jnp.einsum("de,gh->gdhe", w2, eye).reshape(g * hd, g * hd).astype(BF16)
    vec = pl.BlockSpec((1, wide), lambda bi, i: (0, 0))
    mat = pl.BlockSpec((wide, g * hd), lambda bi, i: (0, 0))
    yspec = pl.BlockSpec((None, tc, g * hd), lambda bi, i: (bi, i, 0))
    ya, yb = pl.pallas_call(
        _cmp_proj_kernel,
        name="cmp_proj",
        grid=(b, n // tc),
        in_specs=[pl.BlockSpec((None, tc, wide), lambda bi, i: (bi, i, 0)), vec, vec, mat, mat],
        out_specs=[yspec, yspec],
        out_shape=[jax.ShapeDtypeStruct((b, n, g * hd), F32)] * 2,
        compiler_params=_params("parallel", "parallel"),
    )(x, p_lo, p_hi, mat_a, mat_b)
    full = pl.BlockSpec((None, n, g * hd), lambda bi: (bi, 0, 0))
    return pl.pallas_call(
        _cmp_mix_kernel,
        name="cmp_mix",
        grid=(b,),
        in_specs=[full, full, pl.BlockSpec((g * hd, g * hd), lambda bi: (0, 0))],
        out_specs=pl.BlockSpec((None, g, n, hd), lambda bi: (bi, 0, 0, 0)),
        out_shape=jax.ShapeDtypeStruct((b, g, n, hd), BF16),
        compiler_params=_params("parallel"),
    )(ya, yb, w2bd)


def _nsa_cw_kernel(q_ref, kc_ref, vc_ref, kw_ref, vw_ref, gates_ref, amat_ref, ocw_ref, idx_ref, *, chunk):
    t0 = pl.program_id(2) * Q_BLOCK
    n_chunk = kc_ref.shape[0] // chunk
    last_cmp = (t0 + Q_BLOCK - CMP_BLOCK) // CMP_STRIDE
    need = jnp.minimum(last_cmp // chunk + 1, n_chunk)
    for nc in range(1, n_chunk + 1):
        pl.when(need == nc)(functools.partial(
            _nsa_cw_tile, q_ref, kc_ref, vc_ref, kw_ref, vw_ref, gates_ref, amat_ref, ocw_ref, idx_ref,
            n_cmp=nc * chunk, n_sel=nc * chunk * CMP_STRIDE // SEL_BLOCK))


def _nsa_cw_tile(q_ref, kc_ref, vc_ref, kw_ref, vw_ref, gates_ref, amat_ref, ocw_ref, idx_ref, *, n_cmp, n_sel):
    qb = Q_BLOCK
    rows = NSA_HPG * qb
    t0 = pl.program_id(2) * qb
    q = q_ref[...].reshape(rows, NSA_HD)
    t_row = t0 + lax.broadcasted_iota(jnp.int32, (rows, 1), 0) % qb

    s = _dot_nt(q, kc_ref[0:n_cmp, :])
    cmp_end = lax.broadcasted_iota(jnp.int32, (1, n_cmp), 1) * CMP_STRIDE + (CMP_BLOCK - 1)
    p = _softmax_rows(jnp.where(cmp_end <= t_row, s, NEG_INF))
    o_c = _dot(p.astype(BF16), vc_ref[0:n_cmp, :])
    p_grp = p[0:qb] + p[qb:2 * qb] + p[2 * qb:3 * qb] + p[3 * qb:4 * qb]
    p_hi = p_grp.astype(BF16)
    p_lo = (p_grp - p_hi.astype(F32)).astype(BF16)
    amat = amat_ref[0:n_sel, 0:n_cmp]
    imp = _dot_nt(amat, p_hi) + _dot_nt(amat, p_lo)

    t_tok = t0 + lax.broadcasted_iota(jnp.int32, (1, qb), 1)
    cur = (t_tok // SEL_BLOCK).astype(F32)
    blk = lax.broadcasted_iota(jnp.int32, (n_sel, 1), 0).astype(F32)
    forced = (blk == 0.0) | (blk == cur - 1.0)
    score = jnp.where(forced, jnp.inf, jnp.where(blk < cur, imp, NEG_INF))
    score = jnp.where(blk == cur, NEG_INF, score)
    slot = lax.broadcasted_iota(jnp.int32, (N_SELECT, qb), 0)
    sel = jnp.where(slot == 0, cur, 0.0)
    for n in range(1, N_SELECT):
        m = jnp.max(score, axis=0, keepdims=True)
        pick = jnp.min(jnp.where(score == m, blk, float(n_sel)), axis=0, keepdims=True)
        score = jnp.where(blk == pick, NEG_INF, score)
        sel = jnp.where(slot == n, pick, sel)
    idx_ref[...] = sel.astype(jnp.int32)

    w_len = WINDOW + qb
    w_start = pl.multiple_of(jnp.maximum(t0 - WINDOW, 0), qb)
    s_w = _dot_nt(q, kw_ref[pl.ds(w_start, w_len), :])
    kpos = w_start + lax.broadcasted_iota(jnp.int32, (1, w_len), 1)
    mask_w = (kpos <= t_row) & (kpos > t_row - WINDOW)
    p_w = _softmax_rows(jnp.where(mask_w, s_w, NEG_INF))
    o_w = _dot(p_w.astype(BF16), vw_ref[pl.ds(w_start, w_len), :])

    gates = gates_ref[...]
    outs = []
    for h in range(NSA_HPG):
        g_c = gates[:, 3 * h:3 * h + 1]
        g_w = gates[:, 3 * h + 2:3 * h + 3]
        outs.append(g_c * o_c[h * qb:(h + 1) * qb] + g_w * o_w[h * qb:(h + 1) * qb])
    ocw_ref[...] = jnp.concatenate(outs, axis=1)


def _nsa_cmp_win(q, k_cmp, v_cmp, k_w, v_w, gates, amat, chunk=2 * LANES):
    b, _, s, hd = q.shape
    g = NSA_GROUPS
    n_cmp = k_cmp.shape[2]
    cmp_spec = pl.BlockSpec((None, None, n_cmp, hd), lambda bi, gi, j: (bi, gi, 0, 0))
    win_spec = pl.BlockSpec((None, None, s, hd), lambda bi, gi, j: (bi, gi, 0, 0))
    chunk = min(n_cmp, chunk)
    assert n_cmp % chunk == 0
    return pl.pallas_call(
        functools.partial(_nsa_cw_kernel, chunk=chunk),
        name="nsa_cw",
        grid=(b, g, s // Q_BLOCK),
        in_specs=[pl.BlockSpec((None, NSA_HPG, Q_BLOCK, hd), lambda bi, gi, j: (bi, gi, j, 0)),
                  cmp_spec, cmp_spec, win_spec, win_spec,
                  pl.BlockSpec((None, Q_BLOCK, LANES), lambda bi, gi, j: (bi, j, gi)),
                  pl.BlockSpec(amat.shape, lambda bi, gi, j: (0, 0))],
        out_specs=[pl.BlockSpec((None, Q_BLOCK, NSA_HPG * hd), lambda bi, gi, j: (bi, j, gi)),
                   pl.BlockSpec((None, None, None, N_SELECT, Q_BLOCK), lambda bi, gi, j: (bi, gi, j, 0, 0))],
        out_shape=[jax.ShapeDtypeStruct((b, s, NSA_HEADS * hd), F32),
                   jax.ShapeDtypeStruct((b, g, s // Q_BLOCK, N_SELECT, Q_BLOCK), jnp.int32)],
        compiler_params=_params("parallel", "parallel", "arbitrary"),
    )(q, k_cmp, v_cmp, k_w, v_w, gates, amat)


def _roll_rows(x, shift):
    return jnp.concatenate([pltpu.roll(x[r:r + 8], shift, 0) for r in range(0, x.shape[0], 8)], axis=0)


def _nsa_sel_kernel(idx_ref, q_ref, kv2_ref, o_ref, s_a, s_b, v_a, v_b, *, tq, pairs):
    half_blk = SEL_BLOCK // 2
    n_col = N_SELECT * half_blk
    t0 = pl.program_id(2) * tq
    row = lax.broadcasted_iota(jnp.int32, (16, n_col), 0)
    col = lax.broadcasted_iota(jnp.int32, (16, n_col), 1)
    pos0 = 2 * col + (row % 8) // NSA_HPG
    in_slot0 = col < half_blk
    second = (row >= 8).astype(jnp.int32)
    row8 = lax.broadcasted_iota(jnp.int32, (8, LANES), 0)

    def stage_scores(grp, s_buf, v_buf):
        for u in range(pairs):
            tp = grp * pairs + u
            if u % 2 == 0:
                q_four = q_ref[pl.ds(pl.multiple_of(8 * tp, 16), 16), :].astype(F32)
            q8 = q_four[8 * (u % 2):8 * (u % 2) + 8]
            wide = jnp.concatenate([q8, jnp.zeros_like(q8)], axis=1)
            shifted = pltpu.roll(wide, NSA_HD, 1)
            q16 = jnp.concatenate(
                [jnp.where(row8 < NSA_HPG, wide, pltpu.roll(shifted, NSA_HPG, 0)),
                 jnp.where(row8 < NSA_HPG, pltpu.roll(wide, NSA_HPG, 0), shifted)], axis=0).astype(BF16)
            halves = []
            for w in range(2):
                tok = 2 * tp + w
                base = (tok // Q_BLOCK) * (N_SELECT * Q_BLOCK) + tok % Q_BLOCK
                blocks = []
                for n in range(N_SELECT):
                    off = pl.multiple_of(idx_ref[0, base + n * Q_BLOCK] * half_blk, half_blk)
                    blocks.append(kv2_ref[pl.ds(off, half_blk), :])
                keys = jnp.concatenate([blk[:, :LANES] for blk in blocks], axis=0)
                v_buf[(2 * u + w) * n_col:(2 * u + w + 1) * n_col, :] = jnp.concatenate(
                    [blk[:, LANES:] for blk in blocks], axis=0)
                halves.append(_dot_nt(q16, keys)[8 * w:8 * w + 8])
            s_buf[16 * u:16 * u + 16, :] = jnp.concatenate(halves, axis=0)

    def stage_output(grp, s_buf, v_buf):
        for u in range(pairs):
            tp = grp * pairs + u
            ta = t0 + 2 * tp
            n_valid = jnp.minimum(ta // SEL_BLOCK + 1, N_SELECT)
            ok = jnp.where(in_slot0, pos0 - (ta % SEL_BLOCK + second), col - n_valid * half_blk + 1) <= 0
            s = jnp.where(ok, s_buf[16 * u:16 * u + 16, :], NEG_INF)
            m = jnp.max(s, axis=-1, keepdims=True)
            m = jnp.maximum(m, _roll_rows(m, NSA_HPG))
            e = jnp.exp(s - m)
            l = jnp.sum(e, axis=-1, keepdims=True)
            l = l + _roll_rows(l, NSA_HPG)
            p = (e / l).astype(BF16)
            xa = _dot(p, v_buf[(2 * u) * n_col:(2 * u + 1) * n_col, :])
            xb = _dot(p, v_buf[(2 * u + 1) * n_col:(2 * u + 2) * n_col, :])
            x = jnp.concatenate([xa[:8], xb[8:]], axis=0)
            y = x + pltpu.roll(_roll_rows(x, NSA_HPG), NSA_HD, 1)
            out = jnp.where(row8 < NSA_HPG, y[:8], pltpu.roll(y[8:], NSA_HPG, 0))
            o_ref[pl.ds(pl.multiple_of(8 * tp, 8), 8), :] = out

    n_grp = tq // (2 * pairs)
    stage_scores(0, s_a, v_a)

    def body(k, carry):
        stage_scores(2 * k + 1, s_b, v_b)
        stage_output(2 * k, s_a, v_a)
        stage_scores(2 * k + 2, s_a, v_a)
        stage_output(2 * k + 1, s_b, v_b)
        return carry

    lax.fori_loop(0, n_grp // 2 - 1, body, 0)
    stage_scores(n_grp - 1, s_b, v_b)
    stage_output(n_grp - 2, s_a, v_a)
    stage_output(n_grp - 1, s_b, v_b)


def _nsa_selected(idx, q_rows, kv2, tq=512, pairs=8):
    b, g = idx.shape[:2]
    s = kv2.shape[2] * 2
    tq = min(tq, s)
    nt = s // tq
    assert (tq // (2 * pairs)) % 2 == 0 and tq % Q_BLOCK == 0
    idx_flat = idx.reshape(b * g * nt, 1, tq * N_SELECT)
    n_col = N_SELECT * SEL_BLOCK // 2
    scratch = [pltpu.VMEM((pairs * 16, n_col), F32)] * 2 + [pltpu.VMEM((pairs * 2 * n_col, LANES), BF16)] * 2
    return pl.pallas_call(
        functools.partial(_nsa_sel_kernel, tq=tq, pairs=pairs),
        scratch_shapes=scratch,
        name="nsa_sel",
        grid=(b, g, nt),
        in_specs=[pl.BlockSpec((None, 1, tq * N_SELECT), lambda bi, gi, j: ((bi * g + gi) * nt + j, 0, 0),
                               memory_space=pltpu.SMEM),
                  pl.BlockSpec((None, None, tq * NSA_HPG, NSA_HD), lambda bi, gi, j: (bi, gi, j, 0)),
                  pl.BlockSpec((None, None, s // 2, 2 * LANES), lambda bi, gi, j: (bi, gi, 0, 0))],
        out_specs=pl.BlockSpec((None, None, tq * NSA_HPG, LANES), lambda bi, gi, j: (bi, gi, j, 0)),
        out_shape=jax.ShapeDtypeStruct((b, g, s * NSA_HPG, LANES), F32),
        compiler_params=_params("parallel", "parallel", "arbitrary"),
    )(idx_flat, q_rows, kv2)


def _nsa_out_kernel(x_ref, ocw_ref, os0_ref, os1_ref, os2_ref, os3_ref, gates_ref, emat_ref, w_ref, gain_ref,
                    o_ref):
    tm = x_ref.shape[0]
    left = lax.broadcasted_iota(jnp.int32, (tm, LANES), 1) < NSA_HD
    tiles = []
    for os_ref in (os0_ref, os1_ref, os2_ref, os3_ref):
        heads = [os_ref[pl.ds(h, tm, stride=NSA_HPG), :] for h in range(NSA_HPG)]
        for h in range(0, NSA_HPG, 2):
            tiles.append(jnp.where(left, heads[h], pltpu.roll(heads[h + 1], NSA_HD, 1)))
    o_s = jnp.concatenate(tiles, axis=1)
    g_s = _dot_split(gates_ref[...], emat_ref[...])
    a = (ocw_ref[...] + g_s * o_s).astype(BF16)
    o_ref[...] = x_ref[...] + _rms(_dot(a, w_ref[...]), gain_ref[...])


def _nsa_out_proj(x, o_cw, o_s, gates, emat, w, gain, tm=512):
    b, s, d = x.shape
    tm = min(tm, s)
    assert s % tm == 0
    row = pl.BlockSpec((None, tm, d), lambda bi, i: (bi, i, 0))
    sel = [pl.BlockSpec((None, None, tm * NSA_HPG, LANES), functools.partial(lambda bi, i, gi: (bi, gi, i, 0), gi=gi))
           for gi in range(NSA_GROUPS)]
    return pl.pallas_call(
        _nsa_out_kernel,
        name="nsa_out",
        grid=(b, s // tm),
        in_specs=[row, row, *sel,
                  pl.BlockSpec((None, tm, gates.shape[2]), lambda bi, i: (bi, i, 0)),
                  pl.BlockSpec(emat.shape, lambda bi, i: (0, 0)),
                  pl.BlockSpec(w.shape, lambda bi, i: (0, 0)),
                  pl.BlockSpec((1, d), lambda bi, i: (0, 0))],
        out_specs=row,
        out_shape=jax.ShapeDtypeStruct((b, s, d), F32),
        compiler_params=_params("parallel", "parallel"),
    )(x, o_cw, o_s, o_s, o_s, o_s, gates, emat, w, gain)


def _proj_out_kernel(x_ref, a_ref, w_ref, gain_ref, o_ref):
    o_ref[...] = x_ref[...] + _rms(_dot(a_ref[...], w_ref[...]), gain_ref[...])


def _proj_out(x, a, w, gain, tm=1024):
    t, d = x.shape
    tm = min(tm, t)
    assert t % tm == 0
    row = pl.BlockSpec((tm, d), lambda i: (i, 0))
    return pl.pallas_call(
        _proj_out_kernel,
        name="proj_out",
        grid=(t // tm,),
        in_specs=[row, pl.BlockSpec((tm, a.shape[1]), lambda i: (i, 0)),
                  pl.BlockSpec(w.shape, lambda i: (0, 0)),
                  pl.BlockSpec((1, d), lambda i: (0, 0))],
        out_specs=row,
        out_shape=jax.ShapeDtypeStruct((t, d), F32),
        compiler_params=_params("parallel"),
    )(x, a, w, gain)


def _nsa_mixer(x, gain_pre, gain_post, w_in, w_out, cmp_pos, cmp_w1, cmp_w2):
    b, s, d = x.shape
    g, hd, hpg = NSA_GROUPS, NSA_HD, NSA_HPG
    n_gate = 3 * hpg
    q_cols = NSA_HEADS * hd
    kv_end = q_cols + 6 * NSA_KV_COLS

    gate_w = w_in[:, kv_end:].reshape(d, g, n_gate)
    gate_w = jnp.pad(gate_w, ((0, 0), (0, 0), (0, LANES - n_gate))).reshape(d, g * LANES)
    w_all = jnp.concatenate([w_in[:, :kv_end], gate_w], axis=1).astype(BF16)

    half = hd // 2
    inv = ROPE_THETA ** (-jnp.arange(half, dtype=F32) / half)
    ang = jnp.arange(s).astype(F32)[:, None] * inv[None, :]
    cos, sin = jnp.cos(ang), jnp.sin(ang)
    cos_t = jnp.tile(cos, (1, LANES // half))
    sin_t = jnp.tile(jnp.concatenate([-sin, sin], axis=1), (1, LANES // hd))

    q, k_c, v_c, k_s, v_s, k_w, v_w, gates = _nsa_in_proj(x, gain_pre, w_all, cos_t, sin_t)
    k_cmp = _compress(k_c, cmp_pos[0], cmp_w1[0], cmp_w2[0])
    v_cmp = _compress(v_c, cmp_pos[1], cmp_w1[1], cmp_w2[1])

    n_cmp_rows = s // CMP_STRIDE
    n_sel = s // SEL_BLOCK
    ratio = SEL_BLOCK // CMP_STRIDE
    c_id = jnp.arange(n_cmp_rows)[:, None]
    n_id = jnp.arange(n_sel)[None, :]
    lo = ratio * n_id - (CMP_BLOCK // CMP_STRIDE - 1)
    amat = ((c_id >= lo) & (c_id <= ratio * n_id + ratio - 1) & (c_id < n_cmp_rows - 1)).astype(BF16)

    o_cw, idx = _nsa_cmp_win(q, k_cmp, v_cmp, k_w, v_w, gates, amat.T)

    q_rows = q.reshape(b, g, hpg, s, hd).transpose(0, 1, 3, 2, 4).reshape(b, g, s * hpg, hd)
    kv2 = jnp.concatenate([k_s.reshape(b, g, s // 2, 2 * hd), v_s.reshape(b, g, s // 2, 2 * hd)], axis=-1)
    o_s = _nsa_selected(idx, q_rows, kv2)

    r_id = jnp.arange(g * LANES)[:, None]
    col = jnp.arange(q_cols)[None, :]
    emat = (r_id == (col // (hpg * hd)) * LANES + 3 * ((col // hd) % hpg) + 1).astype(BF16)

    return _nsa_out_proj(x, o_cw, o_s, gates, emat, w_out.astype(BF16), gain_post)


def _hgrn_in_kernel(x_ref, gain_ref, w_ref, lb_ref, q_ref, k_ref, lf_ref, v_ref, gs_ref):
    xn = _rms(x_ref[...], gain_ref[...]).astype(BF16)
    n_chunk = D_MODEL // 256

    def chunk(c):
        return _dot(xn, w_ref[:, 256 * c:256 * (c + 1)])

    for c in range(n_chunk):
        cols = slice(256 * c, 256 * (c + 1))
        r = chunk(c)
        q_ref[:, cols] = r * jax.nn.sigmoid(r)
        f = chunk(n_chunk + c)
        lb = lb_ref[:, cols]
        log_sig = jnp.minimum(f, 0.0) - jnp.log1p(jnp.exp(-jnp.abs(f)))
        a = jnp.log(lb)
        bb = jnp.log1p(-lb) + log_sig
        log_f = jnp.maximum(a, bb) + jnp.log1p(jnp.exp(-jnp.abs(a - bb)))
        lf_ref[:, cols] = log_f
        k_ref[:, cols] = 1.0 - jnp.exp(log_f)
        v_ref[:, cols] = chunk(2 * n_chunk + c)
        r = chunk(3 * n_chunk + c)
        gs_ref[:, cols] = r * jax.nn.sigmoid(r)


def _hgrn_in_proj(x, gain, w, lb, tm=512):
    t, d = x.shape
    tm = min(tm, t)
    assert t % tm == 0
    row = pl.BlockSpec((tm, d), lambda i: (i, 0))
    vec = pl.BlockSpec((1, d), lambda i: (0, 0))
    return pl.pallas_call(
        _hgrn_in_kernel,
        name="hgrn_in",
        grid=(t // tm,),
        in_specs=[row, vec, pl.BlockSpec(w.shape, lambda i: (0, 0)), vec],
        out_specs=[row] * 5,
        out_shape=[jax.ShapeDtypeStruct((t, d), F32)] * 5,
        compiler_params=_params("parallel"),
    )(x, gain, w, lb)


def _hgrn_scan_kernel(q_ref, k_ref, lf_ref, v_ref, gs_ref, gain_ref, o_ref, state_ref, *, rows, heads):
    sub = HGRN_SUB
    n_sub = rows // sub

    @pl.when(pl.program_id(2) == 0)
    def _():
        state_ref[...] = jnp.zeros_like(state_ref)

    r_id = lax.broadcasted_iota(jnp.int32, (rows, rows), 0)
    c_id = lax.broadcasted_iota(jnp.int32, (rows, rows), 1)
    tri = jnp.where(((r_id // sub) == (c_id // sub)) & (c_id <= r_id), 1.0, 0.0)
    row_id = lax.broadcasted_iota(jnp.int32, (sub, HGRN_DK), 0)

    for h in range(heads):
        lanes = slice(LANES * h, LANES * (h + 1))
        cum = jnp.dot(tri, lf_ref[:, lanes], preferred_element_type=F32, precision=lax.Precision.HIGHEST)
        q = q_ref[:, lanes]
        k = k_ref[:, lanes]
        v = v_ref[:, lanes]
        q_dec = (q * jnp.exp(cum)).astype(BF16)

        intra, update, decay = [], [], []
        for i in range(n_sub):
            sl = slice(sub * i, sub * (i + 1))
            qi, ki, vi, ci = q[sl], k[sl], v[sl], cum[sl]
            total = ci[sub - 1:sub]
            o_i = None
            for j in range(sub):
                w = jnp.exp(jnp.where(row_id >= j, ci - ci[j:j + 1], NEG_INF))
                att = jnp.sum(qi * w * ki[j:j + 1], axis=-1, keepdims=True)
                o_i = att * vi[j:j + 1] if o_i is None else o_i + att * vi[j:j + 1]
            intra.append(o_i)
            update.append(_dot_tn(vi.astype(BF16), (ki * jnp.exp(total - ci)).astype(BF16)))
            decay.append(jnp.exp(total))

        state = state_ref[h]
        outs = []
        for i in range(n_sub):
            outs.append(intra[i] + _dot_nt(q_dec[sub * i:sub * (i + 1)], state.astype(BF16)))
            state = state * decay[i] + update[i]
        state_ref[h] = state

        o = jnp.concatenate(outs, axis=0)
        o_ref[:, lanes] = (_rms(o, gain_ref[...]) * gs_ref[:, lanes]).astype(o_ref.dtype)


def _hgrn_scan(q, k, lf, v, gs, gain, rows=128, heads=4):
    b, s, d = q.shape
    rows = min(rows, s)
    blk = pl.BlockSpec((None, rows, heads * LANES), lambda bi, h, l: (bi, l, h))
    return pl.pallas_call(
        functools.partial(_hgrn_scan_kernel, rows=rows, heads=heads),
        name="hgrn_scan",
        grid=(b, d // (heads * LANES), s // rows),
        in_specs=[blk] * 5 + [pl.BlockSpec((1, LANES), lambda bi, h, l: (0, 0))],
        out_specs=blk,
        out_shape=jax.ShapeDtypeStruct((b, s, d), BF16),
        scratch_shapes=[pltpu.VMEM((heads, LANES, HGRN_DK), F32)],
        compiler_params=_params("parallel", "parallel", "arbitrary"),
    )(q, k, lf, v, gs, gain)


def _hgrn_mixer(x, gain_pre, gain_post, w_in, w_out, norm_gain, lower_bound):
    b, s, d = x.shape
    x2 = x.reshape(b * s, d)
    q, k, lf, v, gs = _hgrn_in_proj(x2, gain_pre, w_in.astype(BF16), lower_bound.reshape(1, d))
    shape = (b, s, d)
    o = _hgrn_scan(q.reshape(shape), k.reshape(shape), lf.reshape(shape), v.reshape(shape),
                   gs.reshape(shape), norm_gain.reshape(1, LANES))
    return _proj_out(x2, o.reshape(b * s, d), w_out.astype(BF16), gain_post).reshape(shape)


def kernel(x, p, norm_gains, ffn_w_in, ffn_w_out, ple_w_in, ple_w_gate, nsa_w_in, nsa_w_out, nsa_cmp_pos,
           nsa_cmp_w1, nsa_cmp_w2, hgrn_w_in, hgrn_w_out, hgrn_norm, hgrn_lb_logits):
    b, s, d = x.shape
    depth = norm_gains.shape[0]
    lb_sm = jax.nn.softmax(hgrn_lb_logits.astype(F32), axis=0)
    lower_bounds = jnp.cumsum(lb_sm, axis=0) - lb_sm[0]
    for layer in range(depth):
        ng = norm_gains[layer].reshape(8, 1, d)
        x2 = x.reshape(b * s, d)
        x2 = _ffn_block(x2, ng[0], ng[1], ffn_w_in[layer, 0].astype(BF16), ffn_w_out[layer, 0].astype(BF16))
        x = x2.reshape(b, s, d)
        j = layer // 2
        if layer % 2 == 0:
            x = _nsa_mixer(x, ng[2], ng[3], nsa_w_in[j], nsa_w_out[j], nsa_cmp_pos[j], nsa_cmp_w1[j],
                           nsa_cmp_w2[j])
        else:
            x = _hgrn_mixer(x, ng[2], ng[3], hgrn_w_in[j], hgrn_w_out[j], hgrn_norm[j], lower_bounds[layer])
        x2 = x.reshape(b * s, d)
        ple = (p[layer].reshape(b * s, -1), ng[6], ng[7], ple_w_gate[layer].astype(BF16),
               ple_w_in[layer].astype(BF16))
        x2 = _ffn_block(x2, ng[4], ng[5], ffn_w_in[layer, 1].astype(BF16), ffn_w_out[layer, 1].astype(BF16), ple)
        x = x2.reshape(b, s, d)
    return x
```

```python
import functools

import jax
import jax.numpy as jnp
from jax import lax
from jax.experimental import pallas as pl
from jax.experimental.pallas import tpu as pltpu

F32 = jnp.float32
BF16 = jnp.bfloat16
NEG_INF = float("-inf")

D_MODEL = 1024
NORM_EPS = 1e-6
ROPE_THETA = 10000.0

NSA_HEADS = 16
NSA_GROUPS = 4
NSA_HPG = NSA_HEADS // NSA_GROUPS
NSA_HD = 64
NSA_KV_COLS = NSA_GROUPS * NSA_HD
CMP_BLOCK = 32
CMP_STRIDE = 16
SEL_BLOCK = 64
N_SELECT = 16
WINDOW = 512
Q_BLOCK = 128

HGRN_HEADS = 8
HGRN_DK = 128
HGRN_SUB = 16
LOG2E = 1.4426950408889634

LANES = 128
VMEM_LIMIT = 56 * 1024 * 1024


def _params(*sem):
    return pltpu.CompilerParams(dimension_semantics=sem, vmem_limit_bytes=VMEM_LIMIT)


def _rms(x, gain):
    return x * lax.rsqrt(jnp.mean(x * x, axis=-1, keepdims=True) + NORM_EPS) * gain


def _dot(a, b):
    return jnp.dot(a, b, preferred_element_type=F32)


def _dot_nt(a, b):
    return lax.dot_general(a, b, (((1,), (1,)), ((), ())), preferred_element_type=F32)


def _dot_tn(a, b):
    return lax.dot_general(a, b, (((0,), (0,)), ((), ())), preferred_element_type=F32)


def _dot_split(x, w):
    hi = x.astype(BF16)
    lo = (x - hi.astype(F32)).astype(BF16)
    return _dot(hi, w) + _dot(lo, w)


def _softmax_rows(s):
    m = jnp.max(s, axis=-1, keepdims=True)
    m = jnp.where(m == NEG_INF, 0.0, m)
    e = jnp.exp(s - m)
    return e / jnp.maximum(jnp.sum(e, axis=-1, keepdims=True), 1e-30)


def _ffn_kernel(x_ref, g0_ref, g1_ref, wi_ref, wo_ref, *rest, with_ple):
    x = x_ref[...]
    dff = wo_ref.shape[0]
    gu = _dot(_rms(x, g0_ref[...]).astype(BF16), wi_ref[...])
    g = gu[:, :dff]
    h = (g * jax.nn.sigmoid(g) * gu[:, dff:]).astype(BF16)
    x = x + 0.5 * _rms(_dot(h, wo_ref[...]), g1_ref[...])
    if with_ple:
        p_ref, g2_ref, g3_ref, wg_ref, wp_ref, o_ref = rest
        gate = jax.nn.sigmoid(_dot(_rms(x, g2_ref[...]).astype(BF16), wg_ref[...]))
        e = _dot(p_ref[...].astype(BF16), wp_ref[...])
        x = x + _rms(e * gate, g3_ref[...])
    else:
        o_ref, = rest
    o_ref[...] = x


def _ffn_block(x, g_pre, g_post, w_in, w_out, ple=None, tm=512):
    t, d = x.shape
    tm = min(tm, t)
    assert t % tm == 0
    row = pl.BlockSpec((tm, d), lambda i: (i, 0))
    vec = pl.BlockSpec((1, d), lambda i: (0, 0))

    def resident(w):
        return pl.BlockSpec(w.shape, lambda i: (0, 0), pipeline_mode=pl.Buffered(1))

    args = [x, g_pre, g_post, w_in, w_out]
    specs = [row, vec, vec, resident(w_in), resident(w_out)]
    if ple is not None:
        p, g2, g3, w_gate, w_p = ple
        args += [p, g2, g3, w_gate, w_p]
        specs += [pl.BlockSpec((tm, p.shape[1]), lambda i: (i, 0)), vec, vec, resident(w_gate), resident(w_p)]
    return pl.pallas_call(
        functools.partial(_ffn_kernel, with_ple=ple is not None),
        name="ffn_ple" if ple is not None else "ffn",
        grid=(t // tm,),
        in_specs=specs,
        out_specs=row,
        out_shape=jax.ShapeDtypeStruct((t, d), F32),
        compiler_params=_params("parallel"),
    )(*args)


def _rope_tile(xb, cos_t, sin_t, first_half):
    fwd = pltpu.roll(xb, LANES - NSA_HD // 2, 1)
    bwd = pltpu.roll(xb, NSA_HD // 2, 1)
    return xb * cos_t + jnp.where(first_half, fwd, bwd) * sin_t


def _nsa_in_kernel(x_ref, gain_ref, w_ref, cos_ref, sin_ref,
                   q_ref, kc_ref, vc_ref, ks_ref, vs_ref, kw_ref, vw_ref, gates_ref):
    xn = _rms(x_ref[...], gain_ref[...]).astype(BF16)
    cos_t = cos_ref[...]
    sin_t = sin_ref[...]
    tm = cos_t.shape[0]
    lane = lax.broadcasted_iota(jnp.int32, (tm, LANES), 1)
    first_half = (lane % NSA_HD) < (NSA_HD // 2)

    def chunk(c):
        return _dot(xn, w_ref[:, 256 * c:256 * (c + 1)])

    def roped(r):
        return [_rope_tile(r[:, LANES * h:LANES * (h + 1)], cos_t, sin_t, first_half) for h in range(2)]

    def store_heads(ref, base, tiles, scale):
        for h, tile in enumerate(tiles):
            ref[base + 2 * h] = (tile[:, :NSA_HD] * scale).astype(ref.dtype)
            ref[base + 2 * h + 1] = (tile[:, NSA_HD:] * scale).astype(ref.dtype)

    for c in range(4):
        store_heads(q_ref, 4 * c, roped(chunk(c)), NSA_HD ** -0.5)
    kc = roped(chunk(4))
    kc_ref[...] = jnp.concatenate(kc, axis=1)
    vc_ref[...] = chunk(5)
    store_heads(ks_ref, 0, roped(chunk(6)), 1.0)
    r = chunk(7)
    store_heads(vs_ref, 0, [r[:, :LANES], r[:, LANES:]], 1.0)
    store_heads(kw_ref, 0, roped(chunk(8)), 1.0)
    r = chunk(9)
    store_heads(vw_ref, 0, [r[:, :LANES], r[:, LANES:]], 1.0)
    gates_ref[:, :256] = jax.nn.sigmoid(chunk(10))
    gates_ref[:, 256:] = jax.nn.sigmoid(chunk(11))


def _nsa_in_proj(x, gain, w, cos_t, sin_t, tm=512):
    b, s, d = x.shape
    tm = min(tm, s)
    assert s % tm == 0
    g, hd = NSA_GROUPS, NSA_HD
    kv_heads = pl.BlockSpec((None, g, tm, hd), lambda bi, i: (bi, 0, i, 0))
    kv_flat = pl.BlockSpec((None, tm, g * hd), lambda bi, i: (bi, i, 0))
    tab = pl.BlockSpec((tm, LANES), lambda bi, i: (i, 0))
    return pl.pallas_call(
        _nsa_in_kernel,
        name="nsa_in",
        grid=(b, s // tm),
        in_specs=[pl.BlockSpec((None, tm, d), lambda bi, i: (bi, i, 0)),
                  pl.BlockSpec((1, d), lambda bi, i: (0, 0)),
                  pl.BlockSpec(w.shape, lambda bi, i: (0, 0)),
                  tab, tab],
        out_specs=[pl.BlockSpec((None, NSA_HEADS, tm, hd), lambda bi, i: (bi, 0, i, 0)),
                   kv_flat, kv_flat, kv_heads, kv_heads, kv_heads, kv_heads,
                   pl.BlockSpec((None, tm, 4 * LANES), lambda bi, i: (bi, i, 0))],
        out_shape=[jax.ShapeDtypeStruct((b, NSA_HEADS, s, hd), BF16),
                   jax.ShapeDtypeStruct((b, s, g * hd), F32),
                   jax.ShapeDtypeStruct((b, s, g * hd), F32),
                   jax.ShapeDtypeStruct((b, g, s, hd), BF16),
                   jax.ShapeDtypeStruct((b, g, s, hd), BF16),
                   jax.ShapeDtypeStruct((b, g, s, hd), BF16),
                   jax.ShapeDtypeStruct((b, g, s, hd), BF16),
                   jax.ShapeDtypeStruct((b, s, 4 * LANES), F32)],
        compiler_params=_params("parallel", "parallel"),
    )(x, gain, w, cos_t, sin_t)


def _cmp_proj_kernel(x_ref, plo_ref, phi_ref, a_ref, b_ref, ya_ref, yb_ref):
    x = x_ref[...]
    ya_ref[...] = _dot((x + plo_ref[...]).astype(BF16), a_ref[...])
    yb_ref[...] = _dot((x + phi_ref[...]).astype(BF16), b_ref[...])


def _cmp_mix_kernel(ya_ref, yb_ref, w2_ref, o_ref):
    n = ya_ref.shape[0]
    z = ya_ref[...] + pltpu.roll(yb_ref[...], n - 1, 0)
    h = (z * jax.nn.sigmoid(z)).astype(BF16)
    out = _dot(h, w2_ref[...])
    for g in range(NSA_GROUPS):
        o_ref[g] = out[:, NSA_HD * g:NSA_HD * (g + 1)].astype(o_ref.dtype)


def _compress(kv, pos, w1, w2, tc=256):
    b, s, _ = kv.shape
    g, hd = NSA_GROUPS, NSA_HD
    n = s // CMP_STRIDE
    wide = CMP_STRIDE * g * hd
    tc = min(tc, n)
    assert n % tc == 0
    x = kv.reshape(b, n, wide)
    eye = jnp.eye(g, dtype=F32)
    w1r = w1.reshape(CMP_BLOCK, hd, hd)
    mat_a = jnp.einsum("pde,gh->pgdhe", w1r[:CMP_STRIDE], eye).reshape(wide, g * hd).astype(BF16)
    mat_b = jnp.einsum("pde,gh->pgdhe", w1r[CMP_STRIDE:], eye).reshape(wide, g * hd).astype(BF16)
    p_lo = jnp.broadcast_to(pos[:CMP_STRIDE, None, :], (CMP_STRIDE, g, hd)).reshape(1, wide)
    p_hi = jnp.broadcast_to(pos[CMP_STRIDE:, None, :], (CMP_STRIDE, g, hd)).reshape(1, wide)
    w2bd = jnp.einsum("de,gh->gdhe", w2, eye).reshape(g * hd, g * hd).astype(BF16)
    vec = pl.BlockSpec((1, wide), lambda bi, i: (0, 0))
    mat = pl.BlockSpec((wide, g * hd), lambda bi, i: (0, 0))
    yspec = pl.BlockSpec((None, tc, g * hd), lambda bi, i: (bi, i, 0))
    ya, yb = pl.pallas_call(
        _cmp_proj_kernel,
        name="cmp_proj",
        grid=(b, n // tc),
        in_specs=[pl.BlockSpec((None, tc, wide), lambda bi, i: (bi, i, 0)), vec, vec, mat, mat],
        out_specs=[yspec, yspec],
        out_shape=[jax.ShapeDtypeStruct((b, n, g * hd), F32)] * 2,
        compiler_params=_params("parallel", "parallel"),
    )(x, p_lo, p_hi, mat_a, mat_b)
    full = pl.BlockSpec((None, n, g * hd), lambda bi: (bi, 0, 0))
    return pl.pallas_call(
        _cmp_mix_kernel,
        name="cmp_mix",
        grid=(b,),
        in_specs=[full, full, pl.BlockSpec((g * hd, g * hd), lambda bi: (0, 0))],
        out_specs=pl.BlockSpec((None, g, n, hd), lambda bi: (bi, 0, 0, 0)),
        out_shape=jax.ShapeDtypeStruct((b, g, n, hd), BF16),
        compiler_params=_params("parallel"),
    )(ya, yb, w2bd)


def _nsa_cw_kernel(q_ref, kc_ref, vc_ref, kw_ref, vw_ref, gates_ref, amat_ref, ocw_ref, idx_ref, *, chunk):
    t0 = pl.program_id(2) * Q_BLOCK
    n_chunk = kc_ref.shape[0] // chunk
    last_cmp = (t0 + Q_BLOCK - CMP_BLOCK) // CMP_STRIDE
    need = jnp.minimum(last_cmp // chunk + 1, n_chunk)
    for nc in range(1, n_chunk + 1):
        pl.when(need == nc)(functools.partial(
            _nsa_cw_tile, q_ref, kc_ref, vc_ref, kw_ref, vw_ref, gates_ref, amat_ref, ocw_ref, idx_ref,
            n_cmp=nc * chunk, n_sel=nc * chunk * CMP_STRIDE // SEL_BLOCK))


def _nsa_cw_tile(q_ref, kc_ref, vc_ref, kw_ref, vw_ref, gates_ref, amat_ref, ocw_ref, idx_ref, *, n_cmp, n_sel):
    qb = Q_BLOCK
    rows = NSA_HPG * qb
    t0 = pl.program_id(2) * qb
    q = q_ref[...].reshape(rows, NSA_HD)
    t_row = t0 + lax.broadcasted_iota(jnp.int32, (rows, 1), 0) % qb

    s = _dot_nt(q, kc_ref[0:n_cmp, :])
    cmp_end = lax.broadcasted_iota(jnp.int32, (1, n_cmp), 1) * CMP_STRIDE + (CMP_BLOCK - 1)
    p = _softmax_rows(jnp.where(cmp_end <= t_row, s, NEG_INF))
    o_c = _dot(p.astype(BF16), vc_ref[0:n_cmp, :])
    p_grp = p[0:qb] + p[qb:2 * qb] + p[2 * qb:3 * qb] + p[3 * qb:4 * qb]
    p_hi = p_grp.astype(BF16)
    p_lo = (p_grp - p_hi.astype(F32)).astype(BF16)
    amat = amat_ref[0:n_sel, 0:n_cmp]
    imp = _dot_nt(amat, p_hi) + _dot_nt(amat, p_lo)

    t_tok = t0 + lax.broadcasted_iota(jnp.int32, (1, qb), 1)
    cur = (t_tok // SEL_BLOCK).astype(F32)
    blk = lax.broadcasted_iota(jnp.int32, (n_sel, 1), 0).astype(F32)
    forced = (blk == 0.0) | (blk == cur - 1.0)
    score = jnp.where(forced, jnp.inf, jnp.where(blk < cur, imp, NEG_INF))
    score = jnp.where(blk == cur, NEG_INF, score)
    slot = lax.broadcasted_iota(jnp.int32, (N_SELECT, qb), 0)
    sel = jnp.where(slot == 0, cur, 0.0)
    for n in range(1, N_SELECT):
        m = jnp.max(score, axis=0, keepdims=True)
        pick = jnp.min(jnp.where(score == m, blk, float(n_sel)), axis=0, keepdims=True)
        score = jnp.where(blk == pick, NEG_INF, score)
        sel = jnp.where(slot == n, pick, sel)
    idx_ref[...] = sel.astype(jnp.int32)

    w_len = WINDOW + qb
    w_start = pl.multiple_of(jnp.maximum(t0 - WINDOW, 0), qb)
    s_w = _dot_nt(q, kw_ref[pl.ds(w_start, w_len), :])
    kpos = w_start + lax.broadcasted_iota(jnp.int32, (1, w_len), 1)
    mask_w = (kpos <= t_row) & (kpos > t_row - WINDOW)
    p_w = _softmax_rows(jnp.where(mask_w, s_w, NEG_INF))
    o_w = _dot(p_w.astype(BF16), vw_ref[pl.ds(w_start, w_len), :])

    gates = gates_ref[...]
    outs = []
    for h in range(NSA_HPG):
        g_c = gates[:, 3 * h:3 * h + 1]
        g_w = gates[:, 3 * h + 2:3 * h + 3]
        outs.append(g_c * o_c[h * qb:(h + 1) * qb] + g_w * o_w[h * qb:(h + 1) * qb])
    ocw_ref[...] = jnp.concatenate(outs, axis=1)


def _nsa_cmp_win(q, k_cmp, v_cmp, k_w, v_w, gates, amat, chunk=2 * LANES):
    b, _, s, hd = q.shape
    g = NSA_GROUPS
    n_cmp = k_cmp.shape[2]
    cmp_spec = pl.BlockSpec((None, None, n_cmp, hd), lambda bi, gi, j: (bi, gi, 0, 0))
    win_spec = pl.BlockSpec((None, None, s, hd), lambda bi, gi, j: (bi, gi, 0, 0))
    chunk = min(n_cmp, chunk)
    assert n_cmp % chunk == 0
    return pl.pallas_call(
        functools.partial(_nsa_cw_kernel, chunk=chunk),
        name="nsa_cw",
        grid=(b, g, s // Q_BLOCK),
        in_specs=[pl.BlockSpec((None, NSA_HPG, Q_BLOCK, hd), lambda bi, gi, j: (bi, gi, j, 0)),
                  cmp_spec, cmp_spec, win_spec, win_spec,
                  pl.BlockSpec((None, Q_BLOCK, LANES), lambda bi, gi, j: (bi, j, gi)),
                  pl.BlockSpec(amat.shape, lambda bi, gi, j: (0, 0))],
        out_specs=[pl.BlockSpec((None, Q_BLOCK, NSA_HPG * hd), lambda bi, gi, j: (bi, j, gi)),
                   pl.BlockSpec((None, None, None, N_SELECT, Q_BLOCK), lambda bi, gi, j: (bi, gi, j, 0, 0))],
        out_shape=[jax.ShapeDtypeStruct((b, s, NSA_HEADS * hd), F32),
                   jax.ShapeDtypeStruct((b, g, s // Q_BLOCK, N_SELECT, Q_BLOCK), jnp.int32)],
        compiler_params=_params("parallel", "parallel", "arbitrary"),
    )(q, k_cmp, v_cmp, k_w, v_w, gates, amat)


def _roll_rows(x, shift):
    return jnp.concatenate([pltpu.roll(x[r:r + 8], shift, 0) for r in range(0, x.shape[0], 8)], axis=0)


def _nsa_sel_kernel(idx_ref, q_ref, kv2_ref, o_ref, s_a, s_b, v_a, v_b, *, tq, pairs):
    half_blk = SEL_BLOCK // 2
    n_col = N_SELECT * half_blk
    t0 = pl.program_id(2) * tq
    row = lax.broadcasted_iota(jnp.int32, (16, n_col), 0)
    col = lax.broadcasted_iota(jnp.int32, (16, n_col), 1)
    pos0 = 2 * col + (row % 8) // NSA_HPG
    in_slot0 = col < half_blk
    second = (row >= 8).astype(jnp.int32)
    row8 = lax.broadcasted_iota(jnp.int32, (8, LANES), 0)

    def stage_scores(grp, s_buf, v_buf):
        for u in range(pairs):
            tp = grp * pairs + u
            if u % 2 == 0:
                q_four = q_ref[pl.ds(pl.multiple_of(8 * tp, 16), 16), :].astype(F32)
            q8 = q_four[8 * (u % 2):8 * (u % 2) + 8]
            wide = jnp.concatenate([q8, jnp.zeros_like(q8)], axis=1)
            shifted = pltpu.roll(wide, NSA_HD, 1)
            q16 = jnp.concatenate(
                [jnp.where(row8 < NSA_HPG, wide, pltpu.roll(shifted, NSA_HPG, 0)),
                 jnp.where(row8 < NSA_HPG, pltpu.roll(wide, NSA_HPG, 0), shifted)], axis=0).astype(BF16)
            halves = []
            for w in range(2):
                tok = 2 * tp + w
                base = (tok // Q_BLOCK) * (N_SELECT * Q_BLOCK) + tok % Q_BLOCK
                blocks = []
                for n in range(N_SELECT):
                    off = pl.multiple_of(idx_ref[0, base + n * Q_BLOCK] * half_blk, half_blk)
                    blocks.append(kv2_ref[pl.ds(off, half_blk), :])
                keys = jnp.concatenate([blk[:, :LANES] for blk in blocks], axis=0)
                v_buf[(2 * u + w) * n_col:(2 * u + w + 1) * n_col, :] = jnp.concatenate(
                    [blk[:, LANES:] for blk in blocks], axis=0)
                halves.append(_dot_nt(q16, keys)[8 * w:8 * w + 8])
            s_buf[16 * u:16 * u + 16, :] = jnp.concatenate(halves, axis=0)

    def stage_output(grp, s_buf, v_buf):
        for u in range(pairs):
            tp = grp * pairs + u
            ta = t0 + 2 * tp
            n_valid = jnp.minimum(ta // SEL_BLOCK + 1, N_SELECT)
            ok = jnp.where(in_slot0, pos0 - (ta % SEL_BLOCK + second), col - n_valid * half_blk + 1) <= 0
            s = jnp.where(ok, s_buf[16 * u:16 * u + 16, :], NEG_INF)
            m = jnp.max(s, axis=-1, keepdims=True)
            m = jnp.maximum(m, _roll_rows(m, NSA_HPG))
            e = jnp.exp(s - m)
            l = jnp.sum(e, axis=-1, keepdims=True)
            l = l + _roll_rows(l, NSA_HPG)
            p = (e / l).astype(BF16)
            xa = _dot(p, v_buf[(2 * u) * n_col:(2 * u + 1) * n_col, :])
            xb = _dot(p, v_buf[(2 * u + 1) * n_col:(2 * u + 2) * n_col, :])
            x = jnp.concatenate([xa[:8], xb[8:]], axis=0)
            y = x + pltpu.roll(_roll_rows(x, NSA_HPG), NSA_HD, 1)
            out = jnp.where(row8 < NSA_HPG, y[:8], pltpu.roll(y[8:], NSA_HPG, 0))
            o_ref[pl.ds(pl.multiple_of(8 * tp, 8), 8), :] = out

    n_grp = tq // (2 * pairs)
    stage_scores(0, s_a, v_a)

    def body(k, carry):
        stage_scores(2 * k + 1, s_b, v_b)
        stage_output(2 * k, s_a, v_a)
        stage_scores(2 * k + 2, s_a, v_a)
        stage_output(2 * k + 1, s_b, v_b)
        return carry

    lax.fori_loop(0, n_grp // 2 - 1, body, 0)
    stage_scores(n_grp - 1, s_b, v_b)
    stage_output(n_grp - 2, s_a, v_a)
    stage_output(n_grp - 1, s_b, v_b)


def _nsa_selected(idx, q_rows, kv2, tq=512, pairs=8):
    b, g = idx.shape[:2]
    s = kv2.shape[2] * 2
    tq = min(tq, s)
    nt = s // tq
    assert (tq // (2 * pairs)) % 2 == 0 and tq % Q_BLOCK == 0
    idx_flat = idx.reshape(b * g * nt, 1, tq * N_SELECT)
    n_col = N_SELECT * SEL_BLOCK // 2
    scratch = [pltpu.VMEM((pairs * 16, n_col), F32)] * 2 + [pltpu.VMEM((pairs * 2 * n_col, LANES), BF16)] * 2
    return pl.pallas_call(
        functools.partial(_nsa_sel_kernel, tq=tq, pairs=pairs),
        scratch_shapes=scratch,
        name="nsa_sel",
        grid=(b, g, nt),
        in_specs=[pl.BlockSpec((None, 1, tq * N_SELECT), lambda bi, gi, j: ((bi * g + gi) * nt + j, 0, 0),
                               memory_space=pltpu.SMEM),
                  pl.BlockSpec((None, None, tq * NSA_HPG, NSA_HD), lambda bi, gi, j: (bi, gi, j, 0)),
                  pl.BlockSpec((None, None, s // 2, 2 * LANES), lambda bi, gi, j: (bi, gi, 0, 0))],
        out_specs=pl.BlockSpec((None, None, tq * NSA_HPG, LANES), lambda bi, gi, j: (bi, gi, j, 0)),
        out_shape=jax.ShapeDtypeStruct((b, g, s * NSA_HPG, LANES), F32),
        compiler_params=_params("parallel", "parallel", "arbitrary"),
    )(idx_flat, q_rows, kv2)


def _nsa_out_kernel(x_ref, ocw_ref, os0_ref, os1_ref, os2_ref, os3_ref, gates_ref, emat_ref, w_ref, gain_ref,
                    o_ref):
    tm = x_ref.shape[0]
    left = lax.broadcasted_iota(jnp.int32, (tm, LANES), 1) < NSA_HD
    tiles = []
    for os_ref in (os0_ref, os1_ref, os2_ref, os3_ref):
        heads = [os_ref[pl.ds(h, tm, stride=NSA_HPG), :] for h in range(NSA_HPG)]
        for h in range(0, NSA_HPG, 2):
            tiles.append(jnp.where(left, heads[h], pltpu.roll(heads[h + 1], NSA_HD, 1)))
    o_s = jnp.concatenate(tiles, axis=1)
    g_s = _dot_split(gates_ref[...], emat_ref[...])
    a = (ocw_ref[...] + g_s * o_s).astype(BF16)
    o_ref[...] = x_ref[...] + _rms(_dot(a, w_ref[...]), gain_ref[...])


def _nsa_out_proj(x, o_cw, o_s, gates, emat, w, gain, tm=512):
    b, s, d = x.shape
    tm = min(tm, s)
    assert s % tm == 0
    row = pl.BlockSpec((None, tm, d), lambda bi, i: (bi, i, 0))
    sel = [pl.BlockSpec((None, None, tm * NSA_HPG, LANES), functools.partial(lambda bi, i, gi: (bi, gi, i, 0), gi=gi))
           for gi in range(NSA_GROUPS)]
    return pl.pallas_call(
        _nsa_out_kernel,
        name="nsa_out",
        grid=(b, s // tm),
        in_specs=[row, row, *sel,
                  pl.BlockSpec((None, tm, gates.shape[2]), lambda bi, i: (bi, i, 0)),
                  pl.BlockSpec(emat.shape, lambda bi, i: (0, 0)),
                  pl.BlockSpec(w.shape, lambda bi, i: (0, 0)),
                  pl.BlockSpec((1, d), lambda bi, i: (0, 0))],
        out_specs=row,
        out_shape=jax.ShapeDtypeStruct((b, s, d), F32),
        compiler_params=_params("parallel", "parallel"),
    )(x, o_cw, o_s, o_s, o_s, o_s, gates, emat, w, gain)


def _proj_out_kernel(x_ref, a_ref, w_ref, gain_ref, o_ref):
    o_ref[...] = x_ref[...] + _rms(_dot(a_ref[...], w_ref[...]), gain_ref[...])


def _proj_out(x, a, w, gain, tm=1024):
    t, d = x.shape
    tm = min(tm, t)
    assert t % tm == 0
    row = pl.BlockSpec((tm, d), lambda i: (i, 0))
    return pl.pallas_call(
        _proj_out_kernel,
        name="proj_out",
        grid=(t // tm,),
        in_specs=[row, pl.BlockSpec((tm, a.shape[1]), lambda i: (i, 0)),
                  pl.BlockSpec(w.shape, lambda i: (0, 0)),
                  pl.BlockSpec((1, d), lambda i: (0, 0))],
        out_specs=row,
        out_shape=jax.ShapeDtypeStruct((t, d), F32),
        compiler_params=_params("parallel"),
    )(x, a, w, gain)


def _nsa_mixer(x, gain_pre, gain_post, w_in, w_out, cmp_pos, cmp_w1, cmp_w2):
    b, s, d = x.shape
    g, hd, hpg = NSA_GROUPS, NSA_HD, NSA_HPG
    n_gate = 3 * hpg
    q_cols = NSA_HEADS * hd
    kv_end = q_cols + 6 * NSA_KV_COLS

    gate_w = w_in[:, kv_end:].reshape(d, g, n_gate)
    gate_w = jnp.pad(gate_w, ((0, 0), (0, 0), (0, LANES - n_gate))).reshape(d, g * LANES)
    w_all = jnp.concatenate([w_in[:, :kv_end], gate_w], axis=1).astype(BF16)

    half = hd // 2
    inv = ROPE_THETA ** (-jnp.arange(half, dtype=F32) / half)
    ang = jnp.arange(s).astype(F32)[:, None] * inv[None, :]
    cos, sin = jnp.cos(ang), jnp.sin(ang)
    cos_t = jnp.tile(cos, (1, LANES // half))
    sin_t = jnp.tile(jnp.concatenate([-sin, sin], axis=1), (1, LANES // hd))

    q, k_c, v_c, k_s, v_s, k_w, v_w, gates = _nsa_in_proj(x, gain_pre, w_all, cos_t, sin_t)
    k_cmp = _compress(k_c, cmp_pos[0], cmp_w1[0], cmp_w2[0])
    v_cmp = _compress(v_c, cmp_pos[1], cmp_w1[1], cmp_w2[1])

    n_cmp_rows = s // CMP_STRIDE
    n_sel = s // SEL_BLOCK
    ratio = SEL_BLOCK // CMP_STRIDE
    c_id = jnp.arange(n_cmp_rows)[:, None]
    n_id = jnp.arange(n_sel)[None, :]
    lo = ratio * n_id - (CMP_BLOCK // CMP_STRIDE - 1)
    amat = ((c_id >= lo) & (c_id <= ratio * n_id + ratio - 1) & (c_id < n_cmp_rows - 1)).astype(BF16)

    o_cw, idx = _nsa_cmp_win(q, k_cmp, v_cmp, k_w, v_w, gates, amat.T)

    q_rows = q.reshape(b, g, hpg, s, hd).transpose(0, 1, 3, 2, 4).reshape(b, g, s * hpg, hd)
    kv2 = jnp.concatenate([k_s.reshape(b, g, s // 2, 2 * hd), v_s.reshape(b, g, s // 2, 2 * hd)], axis=-1)
    o_s = _nsa_selected(idx, q_rows, kv2)

    r_id = jnp.arange(g * LANES)[:, None]
    col = jnp.arange(q_cols)[None, :]
    emat = (r_id == (col // (hpg * hd)) * LANES + 3 * ((col // hd) % hpg) + 1).astype(BF16)

    return _nsa_out_proj(x, o_cw, o_s, gates, emat, w_out.astype(BF16), gain_post)


def _hgrn_in_kernel(x_ref, gain_ref, w_ref, lb_ref, q_ref, k_ref, lf_ref, v_ref, gs_ref):
    xn = _rms(x_ref[...], gain_ref[...]).astype(BF16)
    n_chunk = D_MODEL // 256

    def chunk(c):
        return _dot(xn, w_ref[:, 256 * c:256 * (c + 1)])

    for c in range(n_chunk):
        cols = slice(256 * c, 256 * (c + 1))
        r = chunk(c)
        q_ref[:, cols] = r * jax.nn.sigmoid(r)
        f = chunk(n_chunk + c)
        lb = lb_ref[:, cols]
        sig = jax.nn.sigmoid(f)
        lf_ref[:, cols] = jnp.log(lb + (1.0 - lb) * sig)
        k_ref[:, cols] = (1.0 - lb) * (1.0 - sig)
        v_ref[:, cols] = chunk(2 * n_chunk + c)
        r = chunk(3 * n_chunk + c)
        gs_ref[:, cols] = r * jax.nn.sigmoid(r)


def _hgrn_in_proj(x, gain, w, lb, tm=512):
    t, d = x.shape
    tm = min(tm, t)
    assert t % tm == 0
    row = pl.BlockSpec((tm, d), lambda i: (i, 0))
    vec = pl.BlockSpec((1, d), lambda i: (0, 0))
    return pl.pallas_call(
        _hgrn_in_kernel,
        name="hgrn_in",
        grid=(t // tm,),
        in_specs=[row, vec, pl.BlockSpec(w.shape, lambda i: (0, 0)), vec],
        out_specs=[row] * 5,
        out_shape=[jax.ShapeDtypeStruct((t, d), F32)] * 5,
        compiler_params=_params("parallel"),
    )(x, gain, w, lb)


def _hgrn_scan_kernel(q_ref, k_ref, lf_ref, v_ref, gs_ref, gain_ref, o_ref, state_ref, *, rows, heads):
    sub = HGRN_SUB
    n_sub = rows // sub

    @pl.when(pl.program_id(2) == 0)
    def _():
        state_ref[...] = jnp.zeros_like(state_ref)

    r_id = lax.broadcasted_iota(jnp.int32, (rows, rows), 0)
    c_id = lax.broadcasted_iota(jnp.int32, (rows, rows), 1)
    tri = jnp.where(((r_id // sub) == (c_id // sub)) & (c_id <= r_id), 1.0, 0.0)
    row_id = lax.broadcasted_iota(jnp.int32, (sub, HGRN_DK), 0)

    for h in range(heads):
        lanes = slice(LANES * h, LANES * (h + 1))
        cum = jnp.dot(tri, lf_ref[:, lanes], preferred_element_type=F32, precision=lax.Precision.HIGHEST) * LOG2E
        q = q_ref[:, lanes]
        k = k_ref[:, lanes]
        v = v_ref[:, lanes]
        q_dec = (q * jnp.exp2(cum)).astype(BF16)

        intra, update, decay = [], [], []
        for i in range(n_sub):
            sl = slice(sub * i, sub * (i + 1))
            qi, ki, vi, ci = q[sl], k[sl], v[sl], cum[sl]
            total = ci[sub - 1:sub]
            o_i = None
            for j in range(sub):
                w = jnp.exp2(jnp.where(row_id >= j, ci - ci[j:j + 1], NEG_INF))
                att = jnp.sum(qi * w * ki[j:j + 1], axis=-1, keepdims=True)
                o_i = att * vi[j:j + 1] if o_i is None else o_i + att * vi[j:j + 1]
            intra.append(o_i)
            update.append(_dot_tn(vi.astype(BF16), (ki * jnp.exp2(total - ci)).astype(BF16)))
            decay.append(jnp.exp2(total))

        state = state_ref[h]
        outs = []
        for i in range(n_sub):
            outs.append(intra[i] + _dot_nt(q_dec[sub * i:sub * (i + 1)], state.astype(BF16)))
            state = state * decay[i] + update[i]
        state_ref[h] = state

        o = jnp.concatenate(outs, axis=0)
        o_ref[:, lanes] = (_rms(o, gain_ref[...]) * gs_ref[:, lanes]).astype(o_ref.dtype)


def _hgrn_scan(q, k, lf, v, gs, gain, rows=128, heads=4):
    b, s, d = q.shape
    rows = min(rows, s)
    blk = pl.BlockSpec((None, rows, heads * LANES), lambda bi, h, l: (bi, l, h))
    return pl.pallas_call(
        functools.partial(_hgrn_scan_kernel, rows=rows, heads=heads),
        name="hgrn_scan",
        grid=(b, d // (heads * LANES), s // rows),
        in_specs=[blk] * 5 + [pl.BlockSpec((1, LANES), lambda bi, h, l: (0, 0))],
        out_specs=blk,
        out_shape=jax.ShapeDtypeStruct((b, s, d), BF16),
        scratch_shapes=[pltpu.VMEM((heads, LANES, HGRN_DK), F32)],
        compiler_params=_params("parallel", "parallel", "arbitrary"),
    )(q, k, lf, v, gs, gain)


def _hgrn_mixer(x, gain_pre, gain_post, w_in, w_out, norm_gain, lower_bound):
    b, s, d = x.shape
    x2 = x.reshape(b * s, d)
    q, k, lf, v, gs = _hgrn_in_proj(x2, gain_pre, w_in.astype(BF16), lower_bound.reshape(1, d))
    shape = (b, s, d)
    o = _hgrn_scan(q.reshape(shape), k.reshape(shape), lf.reshape(shape), v.reshape(shape),
                   gs.reshape(shape), norm_gain.reshape(1, LANES))
    return _proj_out(x2, o.reshape(b * s, d), w_out.astype(BF16), gain_post).reshape(shape)


def kernel(x, p, norm_gains, ffn_w_in, ffn_w_out, ple_w_in, ple_w_gate, nsa_w_in, nsa_w_out, nsa_cmp_pos,
           nsa_cmp_w1, nsa_cmp_w2, hgrn_w_in, hgrn_w_out, hgrn_norm, hgrn_lb_logits):
    b, s, d = x.shape
    depth = norm_gains.shape[0]
    lb_sm = jax.nn.softmax(hgrn_lb_logits.astype(F32), axis=0)
    lower_bounds = jnp.cumsum(lb_sm, axis=0) - lb_sm[0]
    for layer in range(depth):
        ng = norm_gains[layer].reshape(8, 1, d)
        x2 = x.reshape(b * s, d)
        x2 = _ffn_block(x2, ng[0], ng[1], ffn_w_in[layer, 0].astype(BF16), ffn_w_out[layer, 0].astype(BF16))
        x = x2.reshape(b, s, d)
        j = layer // 2
        if layer % 2 == 0:
            x = _nsa_mixer(x, ng[2], ng[3], nsa_w_in[j], nsa_w_out[j], nsa_cmp_pos[j], nsa_cmp_w1[j],
                           nsa_cmp_w2[j])
        else:
            x = _hgrn_mixer(x, ng[2], ng[3], hgrn_w_in[j], hgrn_w_out[j], hgrn_norm[j], lower_bounds[layer])
        x2 = x.reshape(b * s, d)
        ple = (p[layer].reshape(b * s, -1), ng[6], ng[7], ple_w_gate[layer].astype(BF16),
               ple_w_in[layer].astype(BF16))
        x2 = _ffn_block(x2, ng[4], ng[5], ffn_w_in[layer, 1].astype(BF16), ffn_w_out[layer, 1].astype(BF16), ple)
        x = x2.reshape(b, s, d)
    return x
```

```python
import functools

import jax
import jax.numpy as jnp
from jax import lax
from jax.experimental import pallas as pl
from jax.experimental.pallas import tpu as pltpu

F32 = jnp.float32
BF16 = jnp.bfloat16
NEG_INF = float("-inf")

D_MODEL = 1024
NORM_EPS = 1e-6
ROPE_THETA = 10000.0

NSA_HEADS = 16
NSA_GROUPS = 4
NSA_HPG = NSA_HEADS // NSA_GROUPS
NSA_HD = 64
NSA_KV_COLS = NSA_GROUPS * NSA_HD
CMP_BLOCK = 32
CMP_STRIDE = 16
SEL_BLOCK = 64
N_SELECT = 16
WINDOW = 512
Q_BLOCK = 128

HGRN_HEADS = 8
HGRN_DK = 128
HGRN_SUB = 16
LOG2E = 1.4426950408889634

LANES = 128
VMEM_LIMIT = 56 * 1024 * 1024


def _params(*sem):
    return pltpu.CompilerParams(dimension_semantics=sem, vmem_limit_bytes=VMEM_LIMIT)


def _rms(x, gain):
    return x * lax.rsqrt(jnp.mean(x * x, axis=-1, keepdims=True) + NORM_EPS) * gain


def _dot(a, b):
    return jnp.dot(a, b, preferred_element_type=F32)


def _dot_nt(a, b):
    return lax.dot_general(a, b, (((1,), (1,)), ((), ())), preferred_element_type=F32)


def _dot_tn(a, b):
    return lax.dot_general(a, b, (((0,), (0,)), ((), ())), preferred_element_type=F32)


def _dot_split(x, w):
    hi = x.astype(BF16)
    lo = (x - hi.astype(F32)).astype(BF16)
    return _dot(hi, w) + _dot(lo, w)


def _softmax_rows(s):
    m = jnp.max(s, axis=-1, keepdims=True)
    m = jnp.where(m == NEG_INF, 0.0, m)
    e = jnp.exp(s - m)
    return e / jnp.maximum(jnp.sum(e, axis=-1, keepdims=True), 1e-30)


def _ffn_kernel(x_ref, g0_ref, g1_ref, wi_ref, wo_ref, *rest, with_ple):
    x = x_ref[...]
    dff = wo_ref.shape[0]
    gu = _dot(_rms(x, g0_ref[...]).astype(BF16), wi_ref[...])
    g = gu[:, :dff]
    h = (g * jax.nn.sigmoid(g) * gu[:, dff:]).astype(BF16)
    x = x + 0.5 * _rms(_dot(h, wo_ref[...]), g1_ref[...])
    if with_ple:
        p_ref, g2_ref, g3_ref, wg_ref, wp_ref, o_ref = rest
        gate = jax.nn.sigmoid(_dot(_rms(x, g2_ref[...]).astype(BF16), wg_ref[...]))
        e = _dot(p_ref[...].astype(BF16), wp_ref[...])
        x = x + _rms(e * gate, g3_ref[...])
    else:
        o_ref, = rest
    o_ref[...] = x


def _ffn_block(x, g_pre, g_post, w_in, w_out, ple=None, tm=512):
    t, d = x.shape
    tm = min(tm, t)
    assert t % tm == 0
    row = pl.BlockSpec((tm, d), lambda i: (i, 0))
    vec = pl.BlockSpec((1, d), lambda i: (0, 0))

    def resident(w):
        return pl.BlockSpec(w.shape, lambda i: (0, 0), pipeline_mode=pl.Buffered(1))

    args = [x, g_pre, g_post, w_in, w_out]
    specs = [row, vec, vec, resident(w_in), resident(w_out)]
    if ple is not None:
        p, g2, g3, w_gate, w_p = ple
        args += [p, g2, g3, w_gate, w_p]
        specs += [pl.BlockSpec((tm, p.shape[1]), lambda i: (i, 0)), vec, vec, resident(w_gate), resident(w_p)]
    return pl.pallas_call(
        functools.partial(_ffn_kernel, with_ple=ple is not None),
        name="ffn_ple" if ple is not None else "ffn",
        grid=(t // tm,),
        in_specs=specs,
        out_specs=row,
        out_shape=jax.ShapeDtypeStruct((t, d), F32),
        compiler_params=_params("parallel"),
    )(*args)


def _rope_tile(xb, cos_t, sin_t, first_half):
    fwd = pltpu.roll(xb, LANES - NSA_HD // 2, 1)
    bwd = pltpu.roll(xb, NSA_HD // 2, 1)
    return xb * cos_t + jnp.where(first_half, fwd, bwd) * sin_t


def _nsa_in_kernel(x_ref, gain_ref, w_ref, cos_ref, sin_ref,
                   q_ref, kc_ref, vc_ref, ks_ref, vs_ref, kw_ref, vw_ref, gates_ref):
    xn = _rms(x_ref[...], gain_ref[...]).astype(BF16)
    cos_t = cos_ref[...]
    sin_t = sin_ref[...]
    tm = cos_t.shape[0]
    lane = lax.broadcasted_iota(jnp.int32, (tm, LANES), 1)
    first_half = (lane % NSA_HD) < (NSA_HD // 2)

    def chunk(c):
        return _dot(xn, w_ref[:, 256 * c:256 * (c + 1)])

    def roped(r):
        return [_rope_tile(r[:, LANES * h:LANES * (h + 1)], cos_t, sin_t, first_half) for h in range(2)]

    def store_heads(ref, base, tiles, scale):
        for h, tile in enumerate(tiles):
            ref[base + 2 * h] = (tile[:, :NSA_HD] * scale).astype(ref.dtype)
            ref[base + 2 * h + 1] = (tile[:, NSA_HD:] * scale).astype(ref.dtype)

    for c in range(4):
        store_heads(q_ref, 4 * c, roped(chunk(c)), NSA_HD ** -0.5)
    kc = roped(chunk(4))
    kc_ref[...] = jnp.concatenate(kc, axis=1)
    vc_ref[...] = chunk(5)
    store_heads(ks_ref, 0, roped(chunk(6)), 1.0)
    r = chunk(7)
    store_heads(vs_ref, 0, [r[:, :LANES], r[:, LANES:]], 1.0)
    store_heads(kw_ref, 0, roped(chunk(8)), 1.0)
    r = chunk(9)
    store_heads(vw_ref, 0, [r[:, :LANES], r[:, LANES:]], 1.0)
    gates_ref[:, :256] = jax.nn.sigmoid(chunk(10))
    gates_ref[:, 256:] = jax.nn.sigmoid(chunk(11))


def _nsa_in_proj(x, gain, w, cos_t, sin_t, tm=512):
    b, s, d = x.shape
    tm = min(tm, s)
    assert s % tm == 0
    g, hd = NSA_GROUPS, NSA_HD
    kv_heads = pl.BlockSpec((None, g, tm, hd), lambda bi, i: (bi, 0, i, 0))
    kv_flat = pl.BlockSpec((None, tm, g * hd), lambda bi, i: (bi, i, 0))
    tab = pl.BlockSpec((tm, LANES), lambda bi, i: (i, 0))
    return pl.pallas_call(
        _nsa_in_kernel,
        name="nsa_in",
        grid=(b, s // tm),
        in_specs=[pl.BlockSpec((None, tm, d), lambda bi, i: (bi, i, 0)),
                  pl.BlockSpec((1, d), lambda bi, i: (0, 0)),
                  pl.BlockSpec(w.shape, lambda bi, i: (0, 0)),
                  tab, tab],
        out_specs=[pl.BlockSpec((None, NSA_HEADS, tm, hd), lambda bi, i: (bi, 0, i, 0)),
                   kv_flat, kv_flat, kv_heads, kv_heads, kv_heads, kv_heads,
                   pl.BlockSpec((None, tm, 4 * LANES), lambda bi, i: (bi, i, 0))],
        out_shape=[jax.ShapeDtypeStruct((b, NSA_HEADS, s, hd), BF16),
                   jax.ShapeDtypeStruct((b, s, g * hd), F32),
                   jax.ShapeDtypeStruct((b, s, g * hd), F32),
                   jax.ShapeDtypeStruct((b, g, s, hd), BF16),
                   jax.ShapeDtypeStruct((b, g, s, hd), BF16),
                   jax.ShapeDtypeStruct((b, g, s, hd), BF16),
                   jax.ShapeDtypeStruct((b, g, s, hd), BF16),
                   jax.ShapeDtypeStruct((b, s, 4 * LANES), F32)],
        compiler_params=_params("parallel", "parallel"),
    )(x, gain, w, cos_t, sin_t)


def _cmp_proj_kernel(x_ref, plo_ref, phi_ref, a_ref, b_ref, ya_ref, yb_ref):
    x = x_ref[...]
    ya_ref[...] = _dot((x + plo_ref[...]).astype(BF16), a_ref[...])
    yb_ref[...] = _dot((x + phi_ref[...]).astype(BF16), b_ref[...])


def _cmp_mix_kernel(ya_ref, yb_ref, w2_ref, o_ref):
    n = ya_ref.shape[0]
    z = ya_ref[...] + pltpu.roll(yb_ref[...], n - 1, 0)
    h = (z * jax.nn.sigmoid(z)).astype(BF16)
    out = _dot(h, w2_ref[...])
    for g in range(NSA_GROUPS):
        o_ref[g] = out[:, NSA_HD * g:NSA_HD * (g + 1)].astype(o_ref.dtype)


def _compress(kv, pos, w1, w2, tc=256):
    b, s, _ = kv.shape
    g, hd = NSA_GROUPS, NSA_HD
    n = s // CMP_STRIDE
    wide = CMP_STRIDE * g * hd
    tc = min(tc, n)
    assert n % tc == 0
    x = kv.reshape(b, n, wide)
    eye = jnp.eye(g, dtype=F32)
    w1r = w1.reshape(CMP_BLOCK, hd, hd)
    mat_a = jnp.einsum("pde,gh->pgdhe", w1r[:CMP_STRIDE], eye).reshape(wide, g * hd).astype(BF16)
    mat_b = jnp.einsum("pde,gh->pgdhe", w1r[CMP_STRIDE:], eye).reshape(wide, g * hd).astype(BF16)
    p_lo = jnp.broadcast_to(pos[:CMP_STRIDE, None, :], (CMP_STRIDE, g, hd)).reshape(1, wide)
    p_hi = jnp.broadcast_to(pos[CMP_STRIDE:, None, :], (CMP_STRIDE, g, hd)).reshape(1, wide)
    w2bd = jnp.einsum("de,gh->gdhe", w2, eye).reshape(g * hd, g * hd).astype(BF16)
    vec = pl.BlockSpec((1, wide), lambda bi, i: (0, 0))
    mat = pl.BlockSpec((wide, g * hd), lambda bi, i: (0, 0))
    yspec = pl.BlockSpec((None, tc, g * hd), lambda bi, i: (bi, i, 0))
    ya, yb = pl.pallas_call(
        _cmp_proj_kernel,
        name="cmp_proj",
        grid=(b, n // tc),
        in_specs=[pl.BlockSpec((None, tc, wide), lambda bi, i: (bi, i, 0)), vec, vec, mat, mat],
        out_specs=[yspec, yspec],
        out_shape=[jax.ShapeDtypeStruct((b, n, g * hd), F32)] * 2,
        compiler_params=_params("parallel", "parallel"),
    )(x, p_lo, p_hi, mat_a, mat_b)
    full = pl.BlockSpec((None, n, g * hd), lambda bi: (bi, 0, 0))
    return pl.pallas_call(
        _cmp_mix_kernel,
        name="cmp_mix",
        grid=(b,),
        in_specs=[full, full, pl.BlockSpec((g * hd, g * hd), lambda bi: (0, 0))],
        out_specs=pl.BlockSpec((None, g, n, hd), lambda bi: (bi, 0, 0, 0)),
        out_shape=jax.ShapeDtypeStruct((b, g, n, hd), BF16),
        compiler_params=_params("parallel"),
    )(ya, yb, w2bd)


def _nsa_cw_kernel(q_ref, kc_ref, vc_ref, kw_ref, vw_ref, gates_ref, amat_ref, ocw_ref, idx_ref, *, chunk, tiles):
    t_end = (pl.program_id(2) + 1) * tiles * Q_BLOCK
    n_chunk = kc_ref.shape[0] // chunk
    last_cmp = (t_end - CMP_BLOCK) // CMP_STRIDE
    need = jnp.minimum(last_cmp // chunk + 1, n_chunk)

    def variant(nc):
        for sub in range(tiles):
            _nsa_cw_tile(q_ref, kc_ref, vc_ref, kw_ref, vw_ref, gates_ref, amat_ref, ocw_ref, idx_ref, sub=sub,
                         tiles=tiles, n_cmp=nc * chunk, n_sel=nc * chunk * CMP_STRIDE // SEL_BLOCK)

    for nc in range(1, n_chunk + 1):
        pl.when(need == nc)(functools.partial(variant, nc))


def _nsa_cw_tile(q_ref, kc_ref, vc_ref, kw_ref, vw_ref, gates_ref, amat_ref, ocw_ref, idx_ref, *, sub, tiles, n_cmp,
                 n_sel):
    qb = Q_BLOCK
    rows = NSA_HPG * qb
    t0 = (pl.program_id(2) * tiles + sub) * qb
    tok = slice(sub * qb, (sub + 1) * qb)
    q = q_ref[:, tok, :].reshape(rows, NSA_HD)
    t_row = t0 + lax.broadcasted_iota(jnp.int32, (rows, 1), 0) % qb

    s = _dot_nt(q, kc_ref[0:n_cmp, :])
    cmp_end = lax.broadcasted_iota(jnp.int32, (1, n_cmp), 1) * CMP_STRIDE + (CMP_BLOCK - 1)
    p = _softmax_rows(jnp.where(cmp_end <= t_row, s, NEG_INF))
    o_c = _dot(p.astype(BF16), vc_ref[0:n_cmp, :])
    p_grp = p[0:qb] + p[qb:2 * qb] + p[2 * qb:3 * qb] + p[3 * qb:4 * qb]
    p_hi = p_grp.astype(BF16)
    p_lo = (p_grp - p_hi.astype(F32)).astype(BF16)
    amat = amat_ref[0:n_sel, 0:n_cmp]
    imp = _dot_nt(amat, p_hi) + _dot_nt(amat, p_lo)

    t_tok = t0 + lax.broadcasted_iota(jnp.int32, (1, qb), 1)
    cur = (t_tok // SEL_BLOCK).astype(F32)
    blk = lax.broadcasted_iota(jnp.int32, (n_sel, 1), 0).astype(F32)
    forced = (blk == 0.0) | (blk == cur - 1.0)
    score = jnp.where(forced, jnp.inf, jnp.where(blk < cur, imp, NEG_INF))
    score = jnp.where(blk == cur, NEG_INF, score)
    slot = lax.broadcasted_iota(jnp.int32, (N_SELECT, qb), 0)
    sel = jnp.where(slot == 0, cur, 0.0)
    for n in range(1, N_SELECT):
        m = jnp.max(score, axis=0, keepdims=True)
        pick = jnp.min(jnp.where(score == m, blk, float(n_sel)), axis=0, keepdims=True)
        score = jnp.where(blk == pick, NEG_INF, score)
        sel = jnp.where(slot == n, pick, sel)
    idx_ref[sub] = sel.astype(jnp.int32)

    w_len = WINDOW + qb
    w_start = pl.multiple_of(jnp.maximum(t0 - WINDOW, 0), qb)
    s_w = _dot_nt(q, kw_ref[pl.ds(w_start, w_len), :])
    kpos = w_start + lax.broadcasted_iota(jnp.int32, (1, w_len), 1)
    mask_w = (kpos <= t_row) & (kpos > t_row - WINDOW)
    p_w = _softmax_rows(jnp.where(mask_w, s_w, NEG_INF))
    o_w = _dot(p_w.astype(BF16), vw_ref[pl.ds(w_start, w_len), :])

    gates = gates_ref[tok, :]
    outs = []
    for h in range(NSA_HPG):
        g_c = gates[:, 3 * h:3 * h + 1]
        g_w = gates[:, 3 * h + 2:3 * h + 3]
        outs.append(g_c * o_c[h * qb:(h + 1) * qb] + g_w * o_w[h * qb:(h + 1) * qb])
    ocw_ref[tok, :] = jnp.concatenate(outs, axis=1)


def _nsa_cmp_win(q, k_cmp, v_cmp, k_w, v_w, gates, amat, chunk=2 * LANES, tiles=4):
    b, _, s, hd = q.shape
    g = NSA_GROUPS
    n_cmp = k_cmp.shape[2]
    cmp_spec = pl.BlockSpec((None, None, n_cmp, hd), lambda bi, gi, j: (bi, gi, 0, 0))
    win_spec = pl.BlockSpec((None, None, s, hd), lambda bi, gi, j: (bi, gi, 0, 0))
    chunk = min(n_cmp, chunk)
    tiles = min(tiles, s // Q_BLOCK)
    step = tiles * Q_BLOCK
    assert n_cmp % chunk == 0 and s % step == 0 and (chunk * CMP_STRIDE) % step == 0
    return pl.pallas_call(
        functools.partial(_nsa_cw_kernel, chunk=chunk, tiles=tiles),
        name="nsa_cw",
        grid=(b, g, s // step),
        in_specs=[pl.BlockSpec((None, NSA_HPG, step, hd), lambda bi, gi, j: (bi, gi, j, 0)),
                  cmp_spec, cmp_spec, win_spec, win_spec,
                  pl.BlockSpec((None, step, LANES), lambda bi, gi, j: (bi, j, gi)),
                  pl.BlockSpec(amat.shape, lambda bi, gi, j: (0, 0))],
        out_specs=[pl.BlockSpec((None, step, NSA_HPG * hd), lambda bi, gi, j: (bi, j, gi)),
                   pl.BlockSpec((None, None, tiles, N_SELECT, Q_BLOCK), lambda bi, gi, j: (bi, gi, j, 0, 0))],
        out_shape=[jax.ShapeDtypeStruct((b, s, NSA_HEADS * hd), F32),
                   jax.ShapeDtypeStruct((b, g, s // Q_BLOCK, N_SELECT, Q_BLOCK), jnp.int32)],
        compiler_params=_params("parallel", "parallel", "arbitrary"),
    )(q, k_cmp, v_cmp, k_w, v_w, gates, amat)


def _roll_rows(x, shift):
    return jnp.concatenate([pltpu.roll(x[r:r + 8], shift, 0) for r in range(0, x.shape[0], 8)], axis=0)


def _nsa_sel_kernel(idx_ref, q_ref, kv2_ref, o_ref, s_a, s_b, v_a, v_b, *, tq, pairs):
    half_blk = SEL_BLOCK // 2
    n_col = N_SELECT * half_blk
    t0 = pl.program_id(2) * tq
    row = lax.broadcasted_iota(jnp.int32, (16, n_col), 0)
    col = lax.broadcasted_iota(jnp.int32, (16, n_col), 1)
    pos0 = 2 * col + (row % 8) // NSA_HPG
    in_slot0 = col < half_blk
    second = (row >= 8).astype(jnp.int32)
    row8 = lax.broadcasted_iota(jnp.int32, (8, LANES), 0)

    def stage_scores(grp, s_buf, v_buf):
        for u in range(pairs):
            tp = grp * pairs + u
            if u % 2 == 0:
                q_four = q_ref[pl.ds(pl.multiple_of(8 * tp, 16), 16), :].astype(F32)
            q8 = q_four[8 * (u % 2):8 * (u % 2) + 8]
            wide = jnp.concatenate([q8, jnp.zeros_like(q8)], axis=1)
            shifted = pltpu.roll(wide, NSA_HD, 1)
            q16 = jnp.concatenate(
                [jnp.where(row8 < NSA_HPG, wide, pltpu.roll(shifted, NSA_HPG, 0)),
                 jnp.where(row8 < NSA_HPG, pltpu.roll(wide, NSA_HPG, 0), shifted)], axis=0).astype(BF16)
            halves = []
            for w in range(2):
                tok = 2 * tp + w
                base = (tok // Q_BLOCK) * (N_SELECT * Q_BLOCK) + tok % Q_BLOCK
                blocks = []
                for n in range(N_SELECT):
                    off = pl.multiple_of(idx_ref[0, base + n * Q_BLOCK] * half_blk, half_blk)
                    blocks.append(kv2_ref[pl.ds(off, half_blk), :])
                keys = jnp.concatenate([blk[:, :LANES] for blk in blocks], axis=0)
                v_buf[(2 * u + w) * n_col:(2 * u + w + 1) * n_col, :] = jnp.concatenate(
                    [blk[:, LANES:] for blk in blocks], axis=0)
                halves.append(_dot_nt(q16, keys)[8 * w:8 * w + 8])
            s_buf[16 * u:16 * u + 16, :] = jnp.concatenate(halves, axis=0)

    def stage_output(grp, s_buf, v_buf):
        for u in range(pairs):
            tp = grp * pairs + u
            ta = t0 + 2 * tp
            n_valid = jnp.minimum(ta // SEL_BLOCK + 1, N_SELECT)
            ok = jnp.where(in_slot0, pos0 - (ta % SEL_BLOCK + second), col - n_valid * half_blk + 1) <= 0
            s = jnp.where(ok, s_buf[16 * u:16 * u + 16, :], NEG_INF)
            m = jnp.max(s, axis=-1, keepdims=True)
            m = jnp.maximum(m, _roll_rows(m, NSA_HPG))
            e = jnp.exp(s - m)
            l = jnp.sum(e, axis=-1, keepdims=True)
            l = l + _roll_rows(l, NSA_HPG)
            p = (e / l).astype(BF16)
            xa = _dot(p, v_buf[(2 * u) * n_col:(2 * u + 1) * n_col, :])
            xb = _dot(p, v_buf[(2 * u + 1) * n_col:(2 * u + 2) * n_col, :])
            x = jnp.concatenate([xa[:8], xb[8:]], axis=0)
            y = x + pltpu.roll(_roll_rows(x, NSA_HPG), NSA_HD, 1)
            out = jnp.where(row8 < NSA_HPG, y[:8], pltpu.roll(y[8:], NSA_HPG, 0))
            o_ref[pl.ds(pl.multiple_of(8 * tp, 8), 8), :] = out

    n_grp = tq // (2 * pairs)
    stage_scores(0, s_a, v_a)

    def body(k, carry):
        stage_scores(2 * k + 1, s_b, v_b)
        stage_output(2 * k, s_a, v_a)
        stage_scores(2 * k + 2, s_a, v_a)
        stage_output(2 * k + 1, s_b, v_b)
        return carry

    lax.fori_loop(0, n_grp // 2 - 1, body, 0)
    stage_scores(n_grp - 1, s_b, v_b)
    stage_output(n_grp - 2, s_a, v_a)
    stage_output(n_grp - 1, s_b, v_b)


def _nsa_selected(idx, q_rows, kv2, tq=512, pairs=8):
    b, g = idx.shape[:2]
    s = kv2.shape[2] * 2
    tq = min(tq, s)
    nt = s // tq
    assert (tq // (2 * pairs)) % 2 == 0 and tq % Q_BLOCK == 0
    idx_flat = idx.reshape(b * g * nt, 1, tq * N_SELECT)
    n_col = N_SELECT * SEL_BLOCK // 2
    scratch = [pltpu.VMEM((pairs * 16, n_col), F32)] * 2 + [pltpu.VMEM((pairs * 2 * n_col, LANES), BF16)] * 2
    return pl.pallas_call(
        functools.partial(_nsa_sel_kernel, tq=tq, pairs=pairs),
        scratch_shapes=scratch,
        name="nsa_sel",
        grid=(b, g, nt),
        in_specs=[pl.BlockSpec((None, 1, tq * N_SELECT), lambda bi, gi, j: ((bi * g + gi) * nt + j, 0, 0),
                               memory_space=pltpu.SMEM),
                  pl.BlockSpec((None, None, tq * NSA_HPG, NSA_HD), lambda bi, gi, j: (bi, gi, j, 0)),
                  pl.BlockSpec((None, None, s // 2, 2 * LANES), lambda bi, gi, j: (bi, gi, 0, 0))],
        out_specs=pl.BlockSpec((None, None, tq * NSA_HPG, LANES), lambda bi, gi, j: (bi, gi, j, 0)),
        out_shape=jax.ShapeDtypeStruct((b, g, s * NSA_HPG, LANES), F32),
        compiler_params=_params("parallel", "parallel", "arbitrary"),
    )(idx_flat, q_rows, kv2)


def _nsa_out_kernel(x_ref, ocw_ref, os0_ref, os1_ref, os2_ref, os3_ref, gates_ref, emat_ref, w_ref, gain_ref,
                    o_ref):
    tm = x_ref.shape[0]
    left = lax.broadcasted_iota(jnp.int32, (tm, LANES), 1) < NSA_HD
    tiles = []
    for os_ref in (os0_ref, os1_ref, os2_ref, os3_ref):
        heads = [os_ref[pl.ds(h, tm, stride=NSA_HPG), :] for h in range(NSA_HPG)]
        for h in range(0, NSA_HPG, 2):
            tiles.append(jnp.where(left, heads[h], pltpu.roll(heads[h + 1], NSA_HD, 1)))
    o_s = jnp.concatenate(tiles, axis=1)
    g_s = _dot_split(gates_ref[...], emat_ref[...])
    a = (ocw_ref[...] + g_s * o_s).astype(BF16)
    o_ref[...] = x_ref[...] + _rms(_dot(a, w_ref[...]), gain_ref[...])


def _nsa_out_proj(x, o_cw, o_s, gates, emat, w, gain, tm=512):
    b, s, d = x.shape
    tm = min(tm, s)
    assert s % tm == 0
    row = pl.BlockSpec((None, tm, d), lambda bi, i: (bi, i, 0))
    sel = [pl.BlockSpec((None, None, tm * NSA_HPG, LANES), functools.partial(lambda bi, i, gi: (bi, gi, i, 0), gi=gi))
           for gi in range(NSA_GROUPS)]
    return pl.pallas_call(
        _nsa_out_kernel,
        name="nsa_out",
        grid=(b, s // tm),
        in_specs=[row, row, *sel,
                  pl.BlockSpec((None, tm, gates.shape[2]), lambda bi, i: (bi, i, 0)),
                  pl.BlockSpec(emat.shape, lambda bi, i: (0, 0)),
                  pl.BlockSpec(w.shape, lambda bi, i: (0, 0)),
                  pl.BlockSpec((1, d), lambda bi, i: (0, 0))],
        out_specs=row,
        out_shape=jax.ShapeDtypeStruct((b, s, d), F32),
        compiler_params=_params("parallel", "parallel"),
    )(x, o_cw, o_s, o_s, o_s, o_s, gates, emat, w, gain)


def _proj_out_kernel(x_ref, a_ref, w_ref, gain_ref, o_ref):
    o_ref[...] = x_ref[...] + _rms(_dot(a_ref[...], w_ref[...]), gain_ref[...])


def _proj_out(x, a, w, gain, tm=1024):
    t, d = x.shape
    tm = min(tm, t)
    assert t % tm == 0
    row = pl.BlockSpec((tm, d), lambda i: (i, 0))
    return pl.pallas_call(
        _proj_out_kernel,
        name="proj_out",
        grid=(t // tm,),
        in_specs=[row, pl.BlockSpec((tm, a.shape[1]), lambda i: (i, 0)),
                  pl.BlockSpec(w.shape, lambda i: (0, 0)),
                  pl.BlockSpec((1, d), lambda i: (0, 0))],
        out_specs=row,
        out_shape=jax.ShapeDtypeStruct((t, d), F32),
        compiler_params=_params("parallel"),
    )(x, a, w, gain)


def _nsa_mixer(x, gain_pre, gain_post, w_in, w_out, cmp_pos, cmp_w1, cmp_w2):
    b, s, d = x.shape
    g, hd, hpg = NSA_GROUPS, NSA_HD, NSA_HPG
    n_gate = 3 * hpg
    q_cols = NSA_HEADS * hd
    kv_end = q_cols + 6 * NSA_KV_COLS

    gate_w = w_in[:, kv_end:].reshape(d, g, n_gate)
    gate_w = jnp.pad(gate_w, ((0, 0), (0, 0), (0, LANES - n_gate))).reshape(d, g * LANES)
    w_all = jnp.concatenate([w_in[:, :kv_end], gate_w], axis=1).astype(BF16)

    half = hd // 2
    inv = ROPE_THETA ** (-jnp.arange(half, dtype=F32) / half)
    ang = jnp.arange(s).astype(F32)[:, None] * inv[None, :]
    cos, sin = jnp.cos(ang), jnp.sin(ang)
    cos_t = jnp.tile(cos, (1, LANES // half))
    sin_t = jnp.tile(jnp.concatenate([-sin, sin], axis=1), (1, LANES // hd))

    q, k_c, v_c, k_s, v_s, k_w, v_w, gates = _nsa_in_proj(x, gain_pre, w_all, cos_t, sin_t)
    k_cmp = _compress(k_c, cmp_pos[0], cmp_w1[0], cmp_w2[0])
    v_cmp = _compress(v_c, cmp_pos[1], cmp_w1[1], cmp_w2[1])

    n_cmp_rows = s // CMP_STRIDE
    n_sel = s // SEL_BLOCK
    ratio = SEL_BLOCK // CMP_STRIDE
    c_id = jnp.arange(n_cmp_rows)[:, None]
    n_id = jnp.arange(n_sel)[None, :]
    lo = ratio * n_id - (CMP_BLOCK // CMP_STRIDE - 1)
    amat = ((c_id >= lo) & (c_id <= ratio * n_id + ratio - 1) & (c_id < n_cmp_rows - 1)).astype(BF16)

    o_cw, idx = _nsa_cmp_win(q, k_cmp, v_cmp, k_w, v_w, gates, amat.T)

    q_rows = q.reshape(b, g, hpg, s, hd).transpose(0, 1, 3, 2, 4).reshape(b, g, s * hpg, hd)
    kv2 = jnp.concatenate([k_s.reshape(b, g, s // 2, 2 * hd), v_s.reshape(b, g, s // 2, 2 * hd)], axis=-1)
    o_s = _nsa_selected(idx, q_rows, kv2)

    r_id = jnp.arange(g * LANES)[:, None]
    col = jnp.arange(q_cols)[None, :]
    emat = (r_id == (col // (hpg * hd)) * LANES + 3 * ((col // hd) % hpg) + 1).astype(BF16)

    return _nsa_out_proj(x, o_cw, o_s, gates, emat, w_out.astype(BF16), gain_post)


def _hgrn_in_kernel(x_ref, gain_ref, w_ref, lb_ref, q_ref, k_ref, lf_ref, v_ref, gs_ref):
    xn = _rms(x_ref[...], gain_ref[...]).astype(BF16)
    n_chunk = D_MODEL // 256

    def chunk(c):
        return _dot(xn, w_ref[:, 256 * c:256 * (c + 1)])

    for c in range(n_chunk):
        cols = slice(256 * c, 256 * (c + 1))
        r = chunk(c)
        q_ref[:, cols] = r * jax.nn.sigmoid(r)
        f = chunk(n_chunk + c)
        lb = lb_ref[:, cols]
        sig = jax.nn.sigmoid(f)
        lf_ref[:, cols] = jnp.log(lb + (1.0 - lb) * sig)
        k_ref[:, cols] = (1.0 - lb) * (1.0 - sig)
        v_ref[:, cols] = chunk(2 * n_chunk + c)
        r = chunk(3 * n_chunk + c)
        gs_ref[:, cols] = r * jax.nn.sigmoid(r)


def _hgrn_in_proj(x, gain, w, lb, tm=512):
    t, d = x.shape
    tm = min(tm, t)
    assert t % tm == 0
    row = pl.BlockSpec((tm, d), lambda i: (i, 0))
    vec = pl.BlockSpec((1, d), lambda i: (0, 0))
    return pl.pallas_call(
        _hgrn_in_kernel,
        name="hgrn_in",
        grid=(t // tm,),
        in_specs=[row, vec, pl.BlockSpec(w.shape, lambda i: (0, 0)), vec],
        out_specs=[row] * 5,
        out_shape=[jax.ShapeDtypeStruct((t, d), F32)] * 5,
        compiler_params=_params("parallel"),
    )(x, gain, w, lb)


def _hgrn_scan_kernel(q_ref, k_ref, lf_ref, v_ref, gs_ref, gain_ref, o_ref, state_ref, *, rows, heads):
    sub = HGRN_SUB
    n_sub = rows // sub

    @pl.when(pl.program_id(2) == 0)
    def _():
        state_ref[...] = jnp.zeros_like(state_ref)

    r_id = lax.broadcasted_iota(jnp.int32, (rows, rows), 0)
    c_id = lax.broadcasted_iota(jnp.int32, (rows, rows), 1)
    tri = jnp.where(((r_id // sub) == (c_id // sub)) & (c_id <= r_id), 1.0, 0.0)
    row_id = lax.broadcasted_iota(jnp.int32, (sub, HGRN_DK), 0)

    for h in range(heads):
        lanes = slice(LANES * h, LANES * (h + 1))
        cum = jnp.dot(tri, lf_ref[:, lanes], preferred_element_type=F32, precision=lax.Precision.HIGHEST) * LOG2E
        q = q_ref[:, lanes]
        k = k_ref[:, lanes]
        v = v_ref[:, lanes]
        q_dec = (q * jnp.exp2(cum)).astype(BF16)

        intra, update, decay = [], [], []
        for i in range(n_sub):
            sl = slice(sub * i, sub * (i + 1))
            qi, ki, vi, ci = q[sl], k[sl], v[sl], cum[sl]
            total = ci[sub - 1:sub]
            o_i = None
            for j in range(sub):
                w = jnp.exp2(jnp.where(row_id >= j, ci - ci[j:j + 1], NEG_INF))
                att = jnp.sum(qi * w * ki[j:j + 1], axis=-1, keepdims=True)
                o_i = att * vi[j:j + 1] if o_i is None else o_i + att * vi[j:j + 1]
            intra.append(o_i)
            update.append(_dot_tn(vi.astype(BF16), (ki * jnp.exp2(total - ci)).astype(BF16)))
            decay.append(jnp.exp2(total))

        state = state_ref[h]
        outs = []
        for i in range(n_sub):
            outs.append(intra[i] + _dot_nt(q_dec[sub * i:sub * (i + 1)], state.astype(BF16)))
            state = state * decay[i] + update[i]
        state_ref[h] = state

        o = jnp.concatenate(outs, axis=0)
        o_ref[:, lanes] = (_rms(o, gain_ref[...]) * gs_ref[:, lanes]).astype(o_ref.dtype)


def _hgrn_scan(q, k, lf, v, gs, gain, rows=128, heads=4):
    b, s, d = q.shape
    rows = min(rows, s)
    blk = pl.BlockSpec((None, rows, heads * LANES), lambda bi, h, l: (bi, l, h))
    return pl.pallas_call(
        functools.partial(_hgrn_scan_kernel, rows=rows, heads=heads),
        name="hgrn_scan",
        grid=(b, d // (heads * LANES), s // rows),
        in_specs=[blk] * 5 + [pl.BlockSpec((1, LANES), lambda bi, h, l: (0, 0))],
        out_specs=blk,
        out_shape=jax.ShapeDtypeStruct((b, s, d), BF16),
        scratch_shapes=[pltpu.VMEM((heads, LANES, HGRN_DK), F32)],
        compiler_params=_params("parallel", "parallel", "arbitrary"),
    )(q, k, lf, v, gs, gain)


def _hgrn_mixer(x, gain_pre, gain_post, w_in, w_out, norm_gain, lower_bound):
    b, s, d = x.shape
    x2 = x.reshape(b * s, d)
    q, k, lf, v, gs = _hgrn_in_proj(x2, gain_pre, w_in.astype(BF16), lower_bound.reshape(1, d))
    shape = (b, s, d)
    o = _hgrn_scan(q.reshape(shape), k.reshape(shape), lf.reshape(shape), v.reshape(shape),
                   gs.reshape(shape), norm_gain.reshape(1, LANES))
    return _proj_out(x2, o.reshape(b * s, d), w_out.astype(BF16), gain_post).reshape(shape)


def kernel(x, p, norm_gains, ffn_w_in, ffn_w_out, ple_w_in, ple_w_gate, nsa_w_in, nsa_w_out, nsa_cmp_pos,
           nsa_cmp_w1, nsa_cmp_w2, hgrn_w_in, hgrn_w_out, hgrn_norm, hgrn_lb_logits):
    b, s, d = x.shape
    depth = norm_gains.shape[0]
    lb_sm = jax.nn.softmax(hgrn_lb_logits.astype(F32), axis=0)
    lower_bounds = jnp.cumsum(lb_sm, axis=0) - lb_sm[0]
    for layer in range(depth):
        ng = norm_gains[layer].reshape(8, 1, d)
        x2 = x.reshape(b * s, d)
        x2 = _ffn_block(x2, ng[0], ng[1], ffn_w_in[layer, 0].astype(BF16), ffn_w_out[layer, 0].astype(BF16))
        x = x2.reshape(b, s, d)
        j = layer // 2
        if layer % 2 == 0:
            x = _nsa_mixer(x, ng[2], ng[3], nsa_w_in[j], nsa_w_out[j], nsa_cmp_pos[j], nsa_cmp_w1[j],
                           nsa_cmp_w2[j])
        else:
            x = _hgrn_mixer(x, ng[2], ng[3], hgrn_w_in[j], hgrn_w_out[j], hgrn_norm[j], lower_bounds[layer])
        x2 = x.reshape(b * s, d)
        ple = (p[layer].reshape(b * s, -1), ng[6], ng[7], ple_w_gate[layer].astype(BF16),
               ple_w_in[layer].astype(BF16))
        x2 = _ffn_block(x2, ng[4], ng[5], ffn_w_in[layer, 1].astype(BF16), ffn_w_out[layer, 1].astype(BF16), ple)
        x = x2.reshape(b, s, d)
    return x
```

```python
import functools

import jax
import jax.numpy as jnp
from jax import lax
from jax.experimental import pallas as pl
from jax.experimental.pallas import tpu as pltpu

F32 = jnp.float32
BF16 = jnp.bfloat16
NEG_INF = float("-inf")

D_MODEL = 1024
NORM_EPS = 1e-6
ROPE_THETA = 10000.0

NSA_HEADS = 16
NSA_GROUPS = 4
NSA_HPG = NSA_HEADS // NSA_GROUPS
NSA_HD = 64
NSA_KV_COLS = NSA_GROUPS * NSA_HD
CMP_BLOCK = 32
CMP_STRIDE = 16
SEL_BLOCK = 64
N_SELECT = 16
WINDOW = 512
Q_BLOCK = 128

HGRN_HEADS = 8
HGRN_DK = 128
HGRN_SUB = 16
LOG2E = 1.4426950408889634

LANES = 128
VMEM_LIMIT = 56 * 1024 * 1024


def _params(*sem):
    return pltpu.CompilerParams(dimension_semantics=sem, vmem_limit_bytes=VMEM_LIMIT)


def _rms(x, gain):
    return x * lax.rsqrt(jnp.mean(x * x, axis=-1, keepdims=True) + NORM_EPS) * gain


def _dot(a, b):
    return jnp.dot(a, b, preferred_element_type=F32)


def _dot_nt(a, b):
    return lax.dot_general(a, b, (((1,), (1,)), ((), ())), preferred_element_type=F32)


def _dot_tn(a, b):
    return lax.dot_general(a, b, (((0,), (0,)), ((), ())), preferred_element_type=F32)


def _dot_split(x, w):
    hi = x.astype(BF16)
    lo = (x - hi.astype(F32)).astype(BF16)
    return _dot(hi, w) + _dot(lo, w)


def _softmax_rows(s):
    m = jnp.max(s, axis=-1, keepdims=True)
    m = jnp.where(m == NEG_INF, 0.0, m)
    e = jnp.exp(s - m)
    return e / jnp.maximum(jnp.sum(e, axis=-1, keepdims=True), 1e-30)


def _ffn_kernel(x_ref, g0_ref, g1_ref, wi_ref, wo_ref, *rest, with_ple):
    x = x_ref[...]
    dff = wo_ref.shape[0]
    gu = _dot(_rms(x, g0_ref[...]).astype(BF16), wi_ref[...])
    g = gu[:, :dff]
    h = (g * jax.nn.sigmoid(g) * gu[:, dff:]).astype(BF16)
    x = x + 0.5 * _rms(_dot(h, wo_ref[...]), g1_ref[...])
    if with_ple:
        p_ref, g2_ref, g3_ref, wg_ref, wp_ref, o_ref = rest
        gate = jax.nn.sigmoid(_dot(_rms(x, g2_ref[...]).astype(BF16), wg_ref[...]))
        e = _dot(p_ref[...].astype(BF16), wp_ref[...])
        x = x + _rms(e * gate, g3_ref[...])
    else:
        o_ref, = rest
    o_ref[...] = x


def _ffn_block(x, g_pre, g_post, w_in, w_out, ple=None, tm=512):
    t, d = x.shape
    tm = min(tm, t)
    assert t % tm == 0
    row = pl.BlockSpec((tm, d), lambda i: (i, 0))
    vec = pl.BlockSpec((1, d), lambda i: (0, 0))

    def resident(w):
        return pl.BlockSpec(w.shape, lambda i: (0, 0), pipeline_mode=pl.Buffered(1))

    args = [x, g_pre, g_post, w_in, w_out]
    specs = [row, vec, vec, resident(w_in), resident(w_out)]
    if ple is not None:
        p, g2, g3, w_gate, w_p = ple
        args += [p, g2, g3, w_gate, w_p]
        specs += [pl.BlockSpec((tm, p.shape[1]), lambda i: (i, 0)), vec, vec, resident(w_gate), resident(w_p)]
    return pl.pallas_call(
        functools.partial(_ffn_kernel, with_ple=ple is not None),
        name="ffn_ple" if ple is not None else "ffn",
        grid=(t // tm,),
        in_specs=specs,
        out_specs=row,
        out_shape=jax.ShapeDtypeStruct((t, d), F32),
        compiler_params=_params("parallel"),
    )(*args)


def _rope_tile(xb, cos_t, sin_t, first_half):
    fwd = pltpu.roll(xb, LANES - NSA_HD // 2, 1)
    bwd = pltpu.roll(xb, NSA_HD // 2, 1)
    return xb * cos_t + jnp.where(first_half, fwd, bwd) * sin_t


def _nsa_in_kernel(x_ref, gain_ref, w_ref, cos_ref, sin_ref,
                   q_ref, kc_ref, vc_ref, ks_ref, vs_ref, kw_ref, vw_ref, gates_ref):
    xn = _rms(x_ref[...], gain_ref[...]).astype(BF16)
    cos_t = cos_ref[...]
    sin_t = sin_ref[...]
    tm = cos_t.shape[0]
    lane = lax.broadcasted_iota(jnp.int32, (tm, LANES), 1)
    first_half = (lane % NSA_HD) < (NSA_HD // 2)

    def chunk(c):
        return _dot(xn, w_ref[:, 256 * c:256 * (c + 1)])

    def roped(r):
        return [_rope_tile(r[:, LANES * h:LANES * (h + 1)], cos_t, sin_t, first_half) for h in range(2)]

    def store_heads(ref, base, tiles, scale):
        for h, tile in enumerate(tiles):
            ref[base + 2 * h] = (tile[:, :NSA_HD] * scale).astype(ref.dtype)
            ref[base + 2 * h + 1] = (tile[:, NSA_HD:] * scale).astype(ref.dtype)

    for c in range(4):
        store_heads(q_ref, 4 * c, roped(chunk(c)), NSA_HD ** -0.5)
    kc = roped(chunk(4))
    kc_ref[...] = jnp.concatenate(kc, axis=1)
    vc_ref[...] = chunk(5)
    store_heads(ks_ref, 0, roped(chunk(6)), 1.0)
    r = chunk(7)
    store_heads(vs_ref, 0, [r[:, :LANES], r[:, LANES:]], 1.0)
    store_heads(kw_ref, 0, roped(chunk(8)), 1.0)
    r = chunk(9)
    store_heads(vw_ref, 0, [r[:, :LANES], r[:, LANES:]], 1.0)
    gates_ref[:, :256] = jax.nn.sigmoid(chunk(10))
    gates_ref[:, 256:] = jax.nn.sigmoid(chunk(11))


def _nsa_in_proj(x, gain, w, cos_t, sin_t, tm=512):
    b, s, d = x.shape
    tm = min(tm, s)
    assert s % tm == 0
    g, hd = NSA_GROUPS, NSA_HD
    kv_heads = pl.BlockSpec((None, g, tm, hd), lambda bi, i: (bi, 0, i, 0))
    kv_flat = pl.BlockSpec((None, tm, g * hd), lambda bi, i: (bi, i, 0))
    tab = pl.BlockSpec((tm, LANES), lambda bi, i: (i, 0))
    return pl.pallas_call(
        _nsa_in_kernel,
        name="nsa_in",
        grid=(b, s // tm),
        in_specs=[pl.BlockSpec((None, tm, d), lambda bi, i: (bi, i, 0)),
                  pl.BlockSpec((1, d), lambda bi, i: (0, 0)),
                  pl.BlockSpec(w.shape, lambda bi, i: (0, 0)),
                  tab, tab],
        out_specs=[pl.BlockSpec((None, NSA_HEADS, tm, hd), lambda bi, i: (bi, 0, i, 0)),
                   kv_flat, kv_flat, kv_heads, kv_heads, kv_heads, kv_heads,
                   pl.BlockSpec((None, tm, 4 * LANES), lambda bi, i: (bi, i, 0))],
        out_shape=[jax.ShapeDtypeStruct((b, NSA_HEADS, s, hd), BF16),
                   jax.ShapeDtypeStruct((b, s, g * hd), F32),
                   jax.ShapeDtypeStruct((b, s, g * hd), F32),
                   jax.ShapeDtypeStruct((b, g, s, hd), BF16),
                   jax.ShapeDtypeStruct((b, g, s, hd), BF16),
                   jax.ShapeDtypeStruct((b, g, s, hd), BF16),
                   jax.ShapeDtypeStruct((b, g, s, hd), BF16),
                   jax.ShapeDtypeStruct((b, s, 4 * LANES), F32)],
        compiler_params=_params("parallel", "parallel"),
    )(x, gain, w, cos_t, sin_t)


def _cmp_proj_kernel(x_ref, plo_ref, phi_ref, a_ref, b_ref, ya_ref, yb_ref):
    x = x_ref[...]
    ya_ref[...] = _dot((x + plo_ref[...]).astype(BF16), a_ref[...])
    yb_ref[...] = _dot((x + phi_ref[...]).astype(BF16), b_ref[...])


def _cmp_mix_kernel(ya_ref, yb_ref, w2_ref, o_ref):
    n = ya_ref.shape[0]
    z = ya_ref[...] + pltpu.roll(yb_ref[...], n - 1, 0)
    h = (z * jax.nn.sigmoid(z)).astype(BF16)
    out = _dot(h, w2_ref[...])
    for g in range(NSA_GROUPS):
        o_ref[g] = out[:, NSA_HD * g:NSA_HD * (g + 1)].astype(o_ref.dtype)


def _compress(kv, pos, w1, w2, tc=256):
    b, s, _ = kv.shape
    g, hd = NSA_GROUPS, NSA_HD
    n = s // CMP_STRIDE
    wide = CMP_STRIDE * g * hd
    tc = min(tc, n)
    assert n % tc == 0
    x = kv.reshape(b, n, wide)
    eye = jnp.eye(g, dtype=F32)
    w1r = w1.reshape(CMP_BLOCK, hd, hd)
    mat_a = jnp.einsum("pde,gh->pgdhe", w1r[:CMP_STRIDE], eye).reshape(wide, g * hd).astype(BF16)
    mat_b = jnp.einsum("pde,gh->pgdhe", w1r[CMP_STRIDE:], eye).reshape(wide, g * hd).astype(BF16)
    p_lo = jnp.broadcast_to(pos[:CMP_STRIDE, None, :], (CMP_STRIDE, g, hd)).reshape(1, wide)
    p_hi = jnp.broadcast_to(pos[CMP_STRIDE:, None, :], (CMP_STRIDE, g, hd)).reshape(1, wide)
    w2bd = jnp.einsum("de,gh->gdhe", w2, eye).reshape(g * hd, g * hd).astype(BF16)
    vec = pl.BlockSpec((1, wide), lambda bi, i: (0, 0))
    mat = pl.BlockSpec((wide, g * hd), lambda bi, i: (0, 0))
    yspec = pl.BlockSpec((None, tc, g * hd), lambda bi, i: (bi, i, 0))
    ya, yb = pl.pallas_call(
        _cmp_proj_kernel,
        name="cmp_proj",
        grid=(b, n // tc),
        in_specs=[pl.BlockSpec((None, tc, wide), lambda bi, i: (bi, i, 0)), vec, vec, mat, mat],
        out_specs=[yspec, yspec],
        out_shape=[jax.ShapeDtypeStruct((b, n, g * hd), F32)] * 2,
        compiler_params=_params("parallel", "parallel"),
    )(x, p_lo, p_hi, mat_a, mat_b)
    full = pl.BlockSpec((None, n, g * hd), lambda bi: (bi, 0, 0))
    return pl.pallas_call(
        _cmp_mix_kernel,
        name="cmp_mix",
        grid=(b,),
        in_specs=[full, full, pl.BlockSpec((g * hd, g * hd), lambda bi: (0, 0))],
        out_specs=pl.BlockSpec((None, g, n, hd), lambda bi: (bi, 0, 0, 0)),
        out_shape=jax.ShapeDtypeStruct((b, g, n, hd), BF16),
        compiler_params=_params("parallel"),
    )(ya, yb, w2bd)


def _nsa_cw_kernel(q_ref, kc_ref, vc_ref, kw_ref, vw_ref, gates_ref, amat_ref, ocw_ref, idx_ref, *, chunk, tiles):
    t_end = (pl.program_id(2) + 1) * tiles * Q_BLOCK
    n_chunk = kc_ref.shape[0] // chunk
    last_cmp = (t_end - CMP_BLOCK) // CMP_STRIDE
    need = jnp.minimum(last_cmp // chunk + 1, n_chunk)

    def variant(nc):
        for sub in range(tiles):
            _nsa_cw_tile(q_ref, kc_ref, vc_ref, kw_ref, vw_ref, gates_ref, amat_ref, ocw_ref, idx_ref, sub=sub,
                         tiles=tiles, n_cmp=nc * chunk, n_sel=nc * chunk * CMP_STRIDE // SEL_BLOCK)

    for nc in range(1, n_chunk + 1):
        pl.when(need == nc)(functools.partial(variant, nc))


def _nsa_cw_tile(q_ref, kc_ref, vc_ref, kw_ref, vw_ref, gates_ref, amat_ref, ocw_ref, idx_ref, *, sub, tiles, n_cmp,
                 n_sel):
    qb = Q_BLOCK
    rows = NSA_HPG * qb
    t0 = (pl.program_id(2) * tiles + sub) * qb
    tok = slice(sub * qb, (sub + 1) * qb)
    q = q_ref[:, tok, :].reshape(rows, NSA_HD)
    t_row = t0 + lax.broadcasted_iota(jnp.int32, (rows, 1), 0) % qb

    s = _dot_nt(q, kc_ref[0:n_cmp, :])
    cmp_end = lax.broadcasted_iota(jnp.int32, (1, n_cmp), 1) * CMP_STRIDE + (CMP_BLOCK - 1)
    p = _softmax_rows(jnp.where(cmp_end <= t_row, s, NEG_INF))
    o_c = _dot(p.astype(BF16), vc_ref[0:n_cmp, :])
    p_grp = p[0:qb] + p[qb:2 * qb] + p[2 * qb:3 * qb] + p[3 * qb:4 * qb]
    p_hi = p_grp.astype(BF16)
    p_lo = (p_grp - p_hi.astype(F32)).astype(BF16)
    amat = amat_ref[0:n_sel, 0:n_cmp]
    imp = _dot_nt(amat, p_hi) + _dot_nt(amat, p_lo)

    t_tok = t0 + lax.broadcasted_iota(jnp.int32, (1, qb), 1)
    cur = (t_tok // SEL_BLOCK).astype(F32)
    blk = lax.broadcasted_iota(jnp.int32, (n_sel, 1), 0).astype(F32)
    forced = (blk == 0.0) | (blk == cur - 1.0)
    score = jnp.where(forced, jnp.inf, jnp.where(blk < cur, imp, NEG_INF))
    score = jnp.where(blk == cur, NEG_INF, score)
    slot = lax.broadcasted_iota(jnp.int32, (N_SELECT, qb), 0)
    sel = jnp.where(slot == 0, cur, 0.0)
    for n in range(1, N_SELECT):
        m = jnp.max(score, axis=0, keepdims=True)
        pick = jnp.min(jnp.where(score == m, blk, float(n_sel)), axis=0, keepdims=True)
        score = jnp.where(blk == pick, NEG_INF, score)
        sel = jnp.where(slot == n, pick, sel)
    idx_ref[sub] = sel.astype(jnp.int32)

    w_len = WINDOW + qb
    w_start = pl.multiple_of(jnp.maximum(t0 - WINDOW, 0), qb)
    s_w = _dot_nt(q, kw_ref[pl.ds(w_start, w_len), :])
    kpos = w_start + lax.broadcasted_iota(jnp.int32, (1, w_len), 1)
    mask_w = (kpos <= t_row) & (kpos > t_row - WINDOW)
    p_w = _softmax_rows(jnp.where(mask_w, s_w, NEG_INF))
    o_w = _dot(p_w.astype(BF16), vw_ref[pl.ds(w_start, w_len), :])

    gates = gates_ref[tok, :]
    outs = []
    for h in range(NSA_HPG):
        g_c = gates[:, 3 * h:3 * h + 1]
        g_w = gates[:, 3 * h + 2:3 * h + 3]
        outs.append(g_c * o_c[h * qb:(h + 1) * qb] + g_w * o_w[h * qb:(h + 1) * qb])
    ocw_ref[tok, :] = jnp.concatenate(outs, axis=1)


def _nsa_cmp_win(q, k_cmp, v_cmp, k_w, v_w, gates, amat, chunk=2 * LANES, tiles=4):
    b, _, s, hd = q.shape
    g = NSA_GROUPS
    n_cmp = k_cmp.shape[2]
    cmp_spec = pl.BlockSpec((None, None, n_cmp, hd), lambda bi, gi, j: (bi, gi, 0, 0))
    win_spec = pl.BlockSpec((None, None, s, hd), lambda bi, gi, j: (bi, gi, 0, 0))
    chunk = min(n_cmp, chunk)
    tiles = min(tiles, s // Q_BLOCK)
    step = tiles * Q_BLOCK
    assert n_cmp % chunk == 0 and s % step == 0 and (chunk * CMP_STRIDE) % step == 0
    return pl.pallas_call(
        functools.partial(_nsa_cw_kernel, chunk=chunk, tiles=tiles),
        name="nsa_cw",
        grid=(b, g, s // step),
        in_specs=[pl.BlockSpec((None, NSA_HPG, step, hd), lambda bi, gi, j: (bi, gi, j, 0)),
                  cmp_spec, cmp_spec, win_spec, win_spec,
                  pl.BlockSpec((None, step, LANES), lambda bi, gi, j: (bi, j, gi)),
                  pl.BlockSpec(amat.shape, lambda bi, gi, j: (0, 0))],
        out_specs=[pl.BlockSpec((None, step, NSA_HPG * hd), lambda bi, gi, j: (bi, j, gi)),
                   pl.BlockSpec((None, None, tiles, N_SELECT, Q_BLOCK), lambda bi, gi, j: (bi, gi, j, 0, 0))],
        out_shape=[jax.ShapeDtypeStruct((b, s, NSA_HEADS * hd), F32),
                   jax.ShapeDtypeStruct((b, g, s // Q_BLOCK, N_SELECT, Q_BLOCK), jnp.int32)],
        compiler_params=_params("parallel", "parallel", "arbitrary"),
    )(q, k_cmp, v_cmp, k_w, v_w, gates, amat)


def _roll_rows(x, shift):
    return jnp.concatenate([pltpu.roll(x[r:r + 8], shift, 0) for r in range(0, x.shape[0], 8)], axis=0)


def _nsa_sel_kernel(idx_ref, q_ref, kv2_ref, o_ref, s_a, s_b, v_a, v_b, *, tq, pairs):
    half_blk = SEL_BLOCK // 2
    n_col = N_SELECT * half_blk
    t0 = pl.program_id(2) * tq
    row = lax.broadcasted_iota(jnp.int32, (16, n_col), 0)
    col = lax.broadcasted_iota(jnp.int32, (16, n_col), 1)
    pos0 = 2 * col + (row % 8) // NSA_HPG
    in_slot0 = col < half_blk
    second = (row >= 8).astype(jnp.int32)
    row8 = lax.broadcasted_iota(jnp.int32, (8, LANES), 0)

    def stage_scores(grp, s_buf, v_buf):
        for u in range(pairs):
            tp = grp * pairs + u
            if u % 2 == 0:
                q_four = q_ref[pl.ds(pl.multiple_of(8 * tp, 16), 16), :].astype(F32)
            q8 = q_four[8 * (u % 2):8 * (u % 2) + 8]
            wide = jnp.concatenate([q8, jnp.zeros_like(q8)], axis=1)
            shifted = pltpu.roll(wide, NSA_HD, 1)
            q16 = jnp.concatenate(
                [jnp.where(row8 < NSA_HPG, wide, pltpu.roll(shifted, NSA_HPG, 0)),
                 jnp.where(row8 < NSA_HPG, pltpu.roll(wide, NSA_HPG, 0), shifted)], axis=0).astype(BF16)
            halves = []
            for w in range(2):
                tok = 2 * tp + w
                base = (tok // Q_BLOCK) * (N_SELECT * Q_BLOCK) + tok % Q_BLOCK
                blocks = []
                for n in range(N_SELECT):
                    off = pl.multiple_of(idx_ref[0, base + n * Q_BLOCK] * half_blk, half_blk)
                    blocks.append(kv2_ref[pl.ds(off, half_blk), :])
                keys = jnp.concatenate([blk[:, :LANES] for blk in blocks], axis=0)
                v_buf[(2 * u + w) * n_col:(2 * u + w + 1) * n_col, :] = jnp.concatenate(
                    [blk[:, LANES:] for blk in blocks], axis=0)
                halves.append(_dot_nt(q16, keys)[8 * w:8 * w + 8])
            s_buf[16 * u:16 * u + 16, :] = jnp.concatenate(halves, axis=0)

    def stage_output(grp, s_buf, v_buf):
        for u in range(pairs):
            tp = grp * pairs + u
            ta = t0 + 2 * tp
            n_valid = jnp.minimum(ta // SEL_BLOCK + 1, N_SELECT)
            ok = jnp.where(in_slot0, pos0 - (ta % SEL_BLOCK + second), col - n_valid * half_blk + 1) <= 0
            s = jnp.where(ok, s_buf[16 * u:16 * u + 16, :], NEG_INF)
            m = jnp.max(s, axis=-1, keepdims=True)
            m = jnp.maximum(m, _roll_rows(m, NSA_HPG))
            e = jnp.exp(s - m)
            l = jnp.sum(e, axis=-1, keepdims=True)
            l = l + _roll_rows(l, NSA_HPG)
            p = (e / l).astype(BF16)
            xa = _dot(p, v_buf[(2 * u) * n_col:(2 * u + 1) * n_col, :])
            xb = _dot(p, v_buf[(2 * u + 1) * n_col:(2 * u + 2) * n_col, :])
            x = jnp.concatenate([xa[:8], xb[8:]], axis=0)
            y = x + pltpu.roll(_roll_rows(x, NSA_HPG), NSA_HD, 1)
            out = jnp.where(row8 < NSA_HPG, y[:8], pltpu.roll(y[8:], NSA_HPG, 0))
            o_ref[pl.ds(pl.multiple_of(8 * tp, 8), 8), :] = out

    n_grp = tq // (2 * pairs)
    stage_scores(0, s_a, v_a)

    def body(k, carry):
        stage_scores(2 * k + 1, s_b, v_b)
        stage_output(2 * k, s_a, v_a)
        stage_scores(2 * k + 2, s_a, v_a)
        stage_output(2 * k + 1, s_b, v_b)
        return carry

    lax.fori_loop(0, n_grp // 2 - 1, body, 0)
    stage_scores(n_grp - 1, s_b, v_b)
    stage_output(n_grp - 2, s_a, v_a)
    stage_output(n_grp - 1, s_b, v_b)


def _nsa_selected(idx, q_rows, kv2, tq=512, pairs=8):
    b, g = idx.shape[:2]
    s = kv2.shape[2] * 2
    tq = min(tq, s)
    nt = s // tq
    assert (tq // (2 * pairs)) % 2 == 0 and tq % Q_BLOCK == 0
    idx_flat = idx.reshape(b * g * nt, 1, tq * N_SELECT)
    n_col = N_SELECT * SEL_BLOCK // 2
    scratch = [pltpu.VMEM((pairs * 16, n_col), F32)] * 2 + [pltpu.VMEM((pairs * 2 * n_col, LANES), BF16)] * 2
    return pl.pallas_call(
        functools.partial(_nsa_sel_kernel, tq=tq, pairs=pairs),
        scratch_shapes=scratch,
        name="nsa_sel",
        grid=(b, g, nt),
        in_specs=[pl.BlockSpec((None, 1, tq * N_SELECT), lambda bi, gi, j: ((bi * g + gi) * nt + j, 0, 0),
                               memory_space=pltpu.SMEM),
                  pl.BlockSpec((None, None, tq * NSA_HPG, NSA_HD), lambda bi, gi, j: (bi, gi, j, 0)),
                  pl.BlockSpec((None, None, s // 2, 2 * LANES), lambda bi, gi, j: (bi, gi, 0, 0))],
        out_specs=pl.BlockSpec((None, None, tq * NSA_HPG, LANES), lambda bi, gi, j: (bi, gi, j, 0)),
        out_shape=jax.ShapeDtypeStruct((b, g, s * NSA_HPG, LANES), F32),
        compiler_params=_params("parallel", "parallel", "arbitrary"),
    )(idx_flat, q_rows, kv2)


def _nsa_out_kernel(x_ref, ocw_ref, os0_ref, os1_ref, os2_ref, os3_ref, gates_ref, emat_ref, w_ref, gain_ref,
                    o_ref):
    tm = x_ref.shape[0]
    left = lax.broadcasted_iota(jnp.int32, (tm, LANES), 1) < NSA_HD
    tiles = []
    for os_ref in (os0_ref, os1_ref, os2_ref, os3_ref):
        heads = [os_ref[pl.ds(h, tm, stride=NSA_HPG), :] for h in range(NSA_HPG)]
        for h in range(0, NSA_HPG, 2):
            tiles.append(jnp.where(left, heads[h], pltpu.roll(heads[h + 1], NSA_HD, 1)))
    o_s = jnp.concatenate(tiles, axis=1)
    g_s = _dot_split(gates_ref[...], emat_ref[...])
    a = (ocw_ref[...] + g_s * o_s).astype(BF16)
    o_ref[...] = x_ref[...] + _rms(_dot(a, w_ref[...]), gain_ref[...])


def _nsa_out_proj(x, o_cw, o_s, gates, emat, w, gain, tm=512):
    b, s, d = x.shape
    tm = min(tm, s)
    assert s % tm == 0
    row = pl.BlockSpec((None, tm, d), lambda bi, i: (bi, i, 0))
    sel = [pl.BlockSpec((None, None, tm * NSA_HPG, LANES), functools.partial(lambda bi, i, gi: (bi, gi, i, 0), gi=gi))
           for gi in range(NSA_GROUPS)]
    return pl.pallas_call(
        _nsa_out_kernel,
        name="nsa_out",
        grid=(b, s // tm),
        in_specs=[row, row, *sel,
                  pl.BlockSpec((None, tm, gates.shape[2]), lambda bi, i: (bi, i, 0)),
                  pl.BlockSpec(emat.shape, lambda bi, i: (0, 0)),
                  pl.BlockSpec(w.shape, lambda bi, i: (0, 0)),
                  pl.BlockSpec((1, d), lambda bi, i: (0, 0))],
        out_specs=row,
        out_shape=jax.ShapeDtypeStruct((b, s, d), F32),
        compiler_params=_params("parallel", "parallel"),
    )(x, o_cw, o_s, o_s, o_s, o_s, gates, emat, w, gain)


def _proj_out_kernel(x_ref, a_ref, w_ref, gain_ref, o_ref):
    o_ref[...] = x_ref[...] + _rms(_dot(a_ref[...], w_ref[...]), gain_ref[...])


def _proj_out(x, a, w, gain, tm=1024):
    t, d = x.shape
    tm = min(tm, t)
    assert t % tm == 0
    row = pl.BlockSpec((tm, d), lambda i: (i, 0))
    return pl.pallas_call(
        _proj_out_kernel,
        name="proj_out",
        grid=(t // tm,),
        in_specs=[row, pl.BlockSpec((tm, a.shape[1]), lambda i: (i, 0)),
                  pl.BlockSpec(w.shape, lambda i: (0, 0)),
                  pl.BlockSpec((1, d), lambda i: (0, 0))],
        out_specs=row,
        out_shape=jax.ShapeDtypeStruct((t, d), F32),
        compiler_params=_params("parallel"),
    )(x, a, w, gain)


def _nsa_mixer(x, gain_pre, gain_post, w_in, w_out, cmp_pos, cmp_w1, cmp_w2):
    b, s, d = x.shape
    g, hd, hpg = NSA_GROUPS, NSA_HD, NSA_HPG
    n_gate = 3 * hpg
    q_cols = NSA_HEADS * hd
    kv_end = q_cols + 6 * NSA_KV_COLS

    gate_w = w_in[:, kv_end:].reshape(d, g, n_gate)
    gate_w = jnp.pad(gate_w, ((0, 0), (0, 0), (0, LANES - n_gate))).reshape(d, g * LANES)
    w_all = jnp.concatenate([w_in[:, :kv_end], gate_w], axis=1).astype(BF16)

    half = hd // 2
    inv = ROPE_THETA ** (-jnp.arange(half, dtype=F32) / half)
    ang = jnp.arange(s).astype(F32)[:, None] * inv[None, :]
    cos, sin = jnp.cos(ang), jnp.sin(ang)
    cos_t = jnp.tile(cos, (1, LANES // half))
    sin_t = jnp.tile(jnp.concatenate([-sin, sin], axis=1), (1, LANES // hd))

    q, k_c, v_c, k_s, v_s, k_w, v_w, gates = _nsa_in_proj(x, gain_pre, w_all, cos_t, sin_t)
    k_cmp = _compress(k_c, cmp_pos[0], cmp_w1[0], cmp_w2[0])
    v_cmp = _compress(v_c, cmp_pos[1], cmp_w1[1], cmp_w2[1])

    n_cmp_rows = s // CMP_STRIDE
    n_sel = s // SEL_BLOCK
    ratio = SEL_BLOCK // CMP_STRIDE
    c_id = jnp.arange(n_cmp_rows)[:, None]
    n_id = jnp.arange(n_sel)[None, :]
    lo = ratio * n_id - (CMP_BLOCK // CMP_STRIDE - 1)
    amat = ((c_id >= lo) & (c_id <= ratio * n_id + ratio - 1) & (c_id < n_cmp_rows - 1)).astype(BF16)

    o_cw, idx = _nsa_cmp_win(q, k_cmp, v_cmp, k_w, v_w, gates, amat.T)

    q_rows = q.reshape(b, g, hpg, s, hd).transpose(0, 1, 3, 2, 4).reshape(b, g, s * hpg, hd)
    kv2 = jnp.concatenate([k_s.reshape(b, g, s // 2, 2 * hd), v_s.reshape(b, g, s // 2, 2 * hd)], axis=-1)
    o_s = _nsa_selected(idx, q_rows, kv2)

    r_id = jnp.arange(g * LANES)[:, None]
    col = jnp.arange(q_cols)[None, :]
    emat = (r_id == (col // (hpg * hd)) * LANES + 3 * ((col // hd) % hpg) + 1).astype(BF16)

    return _nsa_out_proj(x, o_cw, o_s, gates, emat, w_out.astype(BF16), gain_post)


def _hgrn_in_kernel(x_ref, gain_ref, w_ref, lb_ref, q_ref, k_ref, lf_ref, v_ref, gs_ref):
    xn = _rms(x_ref[...], gain_ref[...]).astype(BF16)
    n_chunk = D_MODEL // 256

    def chunk(c):
        return _dot(xn, w_ref[:, 256 * c:256 * (c + 1)])

    for c in range(n_chunk):
        cols = slice(256 * c, 256 * (c + 1))
        r = chunk(c)
        q_ref[:, cols] = r * jax.nn.sigmoid(r)
        f = chunk(n_chunk + c)
        lb = lb_ref[:, cols]
        sig = jax.nn.sigmoid(f)
        lf_ref[:, cols] = jnp.log(lb + (1.0 - lb) * sig)
        k_ref[:, cols] = (1.0 - lb) * (1.0 - sig)
        v_ref[:, cols] = chunk(2 * n_chunk + c)
        r = chunk(3 * n_chunk + c)
        gs_ref[:, cols] = r * jax.nn.sigmoid(r)


def _hgrn_in_proj(x, gain, w, lb, tm=512):
    t, d = x.shape
    tm = min(tm, t)
    assert t % tm == 0
    row = pl.BlockSpec((tm, d), lambda i: (i, 0))
    vec = pl.BlockSpec((1, d), lambda i: (0, 0))
    return pl.pallas_call(
        _hgrn_in_kernel,
        name="hgrn_in",
        grid=(t // tm,),
        in_specs=[row, vec, pl.BlockSpec(w.shape, lambda i: (0, 0)), vec],
        out_specs=[row] * 5,
        out_shape=[jax.ShapeDtypeStruct((t, d), F32)] * 5,
        compiler_params=_params("parallel"),
    )(x, gain, w, lb)


def _hgrn_scan_kernel(q_ref, k_ref, lf_ref, v_ref, gs_ref, gain_ref, o_ref, state_ref, *, rows, heads):
    sub = HGRN_SUB
    n_sub = rows // sub

    @pl.when(pl.program_id(2) == 0)
    def _():
        state_ref[...] = jnp.zeros_like(state_ref)

    r_id = lax.broadcasted_iota(jnp.int32, (rows, rows), 0)
    c_id = lax.broadcasted_iota(jnp.int32, (rows, rows), 1)
    tri = jnp.where(((r_id // sub) == (c_id // sub)) & (c_id <= r_id), 1.0, 0.0)
    row_id = lax.broadcasted_iota(jnp.int32, (sub, HGRN_DK), 0)

    for h in range(heads):
        lanes = slice(LANES * h, LANES * (h + 1))
        cum = jnp.dot(tri, lf_ref[:, lanes], preferred_element_type=F32, precision=lax.Precision.HIGHEST) * LOG2E
        q = q_ref[:, lanes]
        k = k_ref[:, lanes]
        v = v_ref[:, lanes]
        q_dec = (q * jnp.exp2(cum)).astype(BF16)

        intra, update, decay = [], [], []
        for i in range(n_sub):
            sl = slice(sub * i, sub * (i + 1))
            qi, ki, vi, ci = q[sl], k[sl], v[sl], cum[sl]
            total = ci[sub - 1:sub]
            o_i = None
            for j in range(sub):
                w = jnp.exp2(jnp.where(row_id >= j, ci - ci[j:j + 1], NEG_INF))
                att = jnp.sum(qi * w * ki[j:j + 1], axis=-1, keepdims=True)
                o_i = att * vi[j:j + 1] if o_i is None else o_i + att * vi[j:j + 1]
            intra.append(o_i)
            update.append(_dot_tn(vi.astype(BF16), (ki * jnp.exp2(total - ci)).astype(BF16)))
            decay.append(jnp.exp2(total))

        state = state_ref[h]
        outs = []
        for i in range(n_sub):
            outs.append(intra[i] + _dot_nt(q_dec[sub * i:sub * (i + 1)], state.astype(BF16)))
            state = state * decay[i] + update[i]
        state_ref[h] = state

        o = jnp.concatenate(outs, axis=0)
        o_ref[:, lanes] = (_rms(o, gain_ref[...]) * gs_ref[:, lanes]).astype(o_ref.dtype)


def _hgrn_scan(q, k, lf, v, gs, gain, rows=128, heads=HGRN_HEADS):
    b, s, d = q.shape
    rows = min(rows, s)
    blk = pl.BlockSpec((None, rows, heads * LANES), lambda bi, h, l: (bi, l, h))
    return pl.pallas_call(
        functools.partial(_hgrn_scan_kernel, rows=rows, heads=heads),
        name="hgrn_scan",
        grid=(b, d // (heads * LANES), s // rows),
        in_specs=[blk] * 5 + [pl.BlockSpec((1, LANES), lambda bi, h, l: (0, 0))],
        out_specs=blk,
        out_shape=jax.ShapeDtypeStruct((b, s, d), BF16),
        scratch_shapes=[pltpu.VMEM((heads, LANES, HGRN_DK), F32)],
        compiler_params=_params("parallel", "parallel", "arbitrary"),
    )(q, k, lf, v, gs, gain)


def _hgrn_mixer(x, gain_pre, gain_post, w_in, w_out, norm_gain, lower_bound):
    b, s, d = x.shape
    x2 = x.reshape(b * s, d)
    q, k, lf, v, gs = _hgrn_in_proj(x2, gain_pre, w_in.astype(BF16), lower_bound.reshape(1, d))
    shape = (b, s, d)
    o = _hgrn_scan(q.reshape(shape), k.reshape(shape), lf.reshape(shape), v.reshape(shape),
                   gs.reshape(shape), norm_gain.reshape(1, LANES))
    return _proj_out(x2, o.reshape(b * s, d), w_out.astype(BF16), gain_post).reshape(shape)


def kernel(x, p, norm_gains, ffn_w_in, ffn_w_out, ple_w_in, ple_w_gate, nsa_w_in, nsa_w_out, nsa_cmp_pos,
           nsa_cmp_w1, nsa_cmp_w2, hgrn_w_in, hgrn_w_out, hgrn_norm, hgrn_lb_logits):
    b, s, d = x.shape
    depth = norm_gains.shape[0]
    lb_sm = jax.nn.softmax(hgrn_lb_logits.astype(F32), axis=0)
    lower_bounds = jnp.cumsum(lb_sm, axis=0) - lb_sm[0]
    for layer in range(depth):
        ng = norm_gains[layer].reshape(8, 1, d)
        x2 = x.reshape(b * s, d)
        x2 = _ffn_block(x2, ng[0], ng[1], ffn_w_in[layer, 0].astype(BF16), ffn_w_out[layer, 0].astype(BF16))
        x = x2.reshape(b, s, d)
        j = layer // 2
        if layer % 2 == 0:
            x = _nsa_mixer(x, ng[2], ng[3], nsa_w_in[j], nsa_w_out[j], nsa_cmp_pos[j], nsa_cmp_w1[j],
                           nsa_cmp_w2[j])
        else:
            x = _hgrn_mixer(x, ng[2], ng[3], hgrn_w_in[j], hgrn_w_out[j], hgrn_norm[j], lower_bounds[layer])
        x2 = x.reshape(b * s, d)
        ple = (p[layer].reshape(b * s, -1), ng[6], ng[7], ple_w_gate[layer].astype(BF16),
               ple_w_in[layer].astype(BF16))
        x2 = _ffn_block(x2, ng[4], ng[5], ffn_w_in[layer, 1].astype(BF16), ffn_w_out[layer, 1].astype(BF16), ple)
        x = x2.reshape(b, s, d)
    return x
```

```python
import functools

import jax
import jax.numpy as jnp
from jax import lax
from jax.experimental import pallas as pl
from jax.experimental.pallas import tpu as pltpu

F32 = jnp.float32
BF16 = jnp.bfloat16
NEG_INF = float("-inf")

D_MODEL = 1024
NORM_EPS = 1e-6
ROPE_THETA = 10000.0

NSA_HEADS = 16
NSA_GROUPS = 4
NSA_HPG = NSA_HEADS // NSA_GROUPS
NSA_HD = 64
NSA_KV_COLS = NSA_GROUPS * NSA_HD
CMP_BLOCK = 32
CMP_STRIDE = 16
SEL_BLOCK = 64
N_SELECT = 16
WINDOW = 512
Q_BLOCK = 128

HGRN_HEADS = 8
HGRN_DK = 128
HGRN_SUB = 16
LOG2E = 1.4426950408889634

LANES = 128
VMEM_LIMIT = 56 * 1024 * 1024


def _params(*sem):
    return pltpu.CompilerParams(dimension_semantics=sem, vmem_limit_bytes=VMEM_LIMIT)


def _rms(x, gain):
    return x * lax.rsqrt(jnp.mean(x * x, axis=-1, keepdims=True) + NORM_EPS) * gain


def _dot(a, b):
    return jnp.dot(a, b, preferred_element_type=F32)


def _dot_nt(a, b):
    return lax.dot_general(a, b, (((1,), (1,)), ((), ())), preferred_element_type=F32)


def _dot_tn(a, b):
    return lax.dot_general(a, b, (((0,), (0,)), ((), ())), preferred_element_type=F32)


def _dot_split(x, w):
    hi = x.astype(BF16)
    lo = (x - hi.astype(F32)).astype(BF16)
    return _dot(hi, w) + _dot(lo, w)


def _softmax_rows(s):
    m = jnp.max(s, axis=-1, keepdims=True)
    m = jnp.where(m == NEG_INF, 0.0, m)
    e = jnp.exp(s - m)
    return e / jnp.maximum(jnp.sum(e, axis=-1, keepdims=True), 1e-30)


def _ffn_kernel(x_ref, g0_ref, g1_ref, wi_ref, wo_ref, *rest, with_ple):
    x = x_ref[...]
    dff = wo_ref.shape[0]
    gu = _dot(_rms(x, g0_ref[...]).astype(BF16), wi_ref[...])
    g = gu[:, :dff]
    h = (g * jax.nn.sigmoid(g) * gu[:, dff:]).astype(BF16)
    x = x + 0.5 * _rms(_dot(h, wo_ref[...]), g1_ref[...])
    if with_ple:
        p_ref, g2_ref, g3_ref, wg_ref, wp_ref, o_ref = rest
        gate = jax.nn.sigmoid(_dot(_rms(x, g2_ref[...]).astype(BF16), wg_ref[...]))
        e = _dot(p_ref[...].astype(BF16), wp_ref[...])
        x = x + _rms(e * gate, g3_ref[...])
    else:
        o_ref, = rest
    o_ref[...] = x


def _ffn_block(x, g_pre, g_post, w_in, w_out, ple=None, tm=512):
    t, d = x.shape
    tm = min(tm, t)
    assert t % tm == 0
    row = pl.BlockSpec((tm, d), lambda i: (i, 0))
    vec = pl.BlockSpec((1, d), lambda i: (0, 0))

    def resident(w):
        return pl.BlockSpec(w.shape, lambda i: (0, 0), pipeline_mode=pl.Buffered(1))

    args = [x, g_pre, g_post, w_in, w_out]
    specs = [row, vec, vec, resident(w_in), resident(w_out)]
    if ple is not None:
        p, g2, g3, w_gate, w_p = ple
        args += [p, g2, g3, w_gate, w_p]
        specs += [pl.BlockSpec((tm, p.shape[1]), lambda i: (i, 0)), vec, vec, resident(w_gate), resident(w_p)]
    return pl.pallas_call(
        functools.partial(_ffn_kernel, with_ple=ple is not None),
        name="ffn_ple" if ple is not None else "ffn",
        grid=(t // tm,),
        in_specs=specs,
        out_specs=row,
        out_shape=jax.ShapeDtypeStruct((t, d), F32),
        compiler_params=_params("parallel"),
    )(*args)


def _rope_tile(xb, cos_t, sin_t, first_half):
    fwd = pltpu.roll(xb, LANES - NSA_HD // 2, 1)
    bwd = pltpu.roll(xb, NSA_HD // 2, 1)
    return xb * cos_t + jnp.where(first_half, fwd, bwd) * sin_t


def _nsa_in_kernel(x_ref, gain_ref, w_ref, cos_ref, sin_ref,
                   q_ref, kc_ref, vc_ref, ks_ref, vs_ref, kw_ref, vw_ref, gates_ref):
    xn = _rms(x_ref[...], gain_ref[...]).astype(BF16)
    cos_t = cos_ref[...]
    sin_t = sin_ref[...]
    tm = cos_t.shape[0]
    lane = lax.broadcasted_iota(jnp.int32, (tm, LANES), 1)
    first_half = (lane % NSA_HD) < (NSA_HD // 2)

    def chunk(c):
        return _dot(xn, w_ref[:, 256 * c:256 * (c + 1)])

    def roped(r):
        return [_rope_tile(r[:, LANES * h:LANES * (h + 1)], cos_t, sin_t, first_half) for h in range(2)]

    def store_heads(ref, base, tiles, scale):
        for h, tile in enumerate(tiles):
            ref[base + 2 * h] = (tile[:, :NSA_HD] * scale).astype(ref.dtype)
            ref[base + 2 * h + 1] = (tile[:, NSA_HD:] * scale).astype(ref.dtype)

    for c in range(4):
        store_heads(q_ref, 4 * c, roped(chunk(c)), NSA_HD ** -0.5)
    kc = roped(chunk(4))
    kc_ref[...] = jnp.concatenate(kc, axis=1)
    vc_ref[...] = chunk(5)
    store_heads(ks_ref, 0, roped(chunk(6)), 1.0)
    r = chunk(7)
    store_heads(vs_ref, 0, [r[:, :LANES], r[:, LANES:]], 1.0)
    store_heads(kw_ref, 0, roped(chunk(8)), 1.0)
    r = chunk(9)
    store_heads(vw_ref, 0, [r[:, :LANES], r[:, LANES:]], 1.0)
    gates_ref[:, :256] = jax.nn.sigmoid(chunk(10))
    gates_ref[:, 256:] = jax.nn.sigmoid(chunk(11))


def _nsa_in_proj(x, gain, w, cos_t, sin_t, tm=512):
    b, s, d = x.shape
    tm = min(tm, s)
    assert s % tm == 0
    g, hd = NSA_GROUPS, NSA_HD
    kv_heads = pl.BlockSpec((None, g, tm, hd), lambda bi, i: (bi, 0, i, 0))
    kv_flat = pl.BlockSpec((None, tm, g * hd), lambda bi, i: (bi, i, 0))
    tab = pl.BlockSpec((tm, LANES), lambda bi, i: (i, 0))
    return pl.pallas_call(
        _nsa_in_kernel,
        name="nsa_in",
        grid=(b, s // tm),
        in_specs=[pl.BlockSpec((None, tm, d), lambda bi, i: (bi, i, 0)),
                  pl.BlockSpec((1, d), lambda bi, i: (0, 0)),
                  pl.BlockSpec(w.shape, lambda bi, i: (0, 0)),
                  tab, tab],
        out_specs=[pl.BlockSpec((None, NSA_HEADS, tm, hd), lambda bi, i: (bi, 0, i, 0)),
                   kv_flat, kv_flat, kv_heads, kv_heads, kv_heads, kv_heads,
                   pl.BlockSpec((None, tm, 4 * LANES), lambda bi, i: (bi, i, 0))],
        out_shape=[jax.ShapeDtypeStruct((b, NSA_HEADS, s, hd), BF16),
                   jax.ShapeDtypeStruct((b, s, g * hd), F32),
                   jax.ShapeDtypeStruct((b, s, g * hd), F32),
                   jax.ShapeDtypeStruct((b, g, s, hd), BF16),
                   jax.ShapeDtypeStruct((b, g, s, hd), BF16),
                   jax.ShapeDtypeStruct((b, g, s, hd), BF16),
                   jax.ShapeDtypeStruct((b, g, s, hd), BF16),
                   jax.ShapeDtypeStruct((b, s, 4 * LANES), F32)],
        compiler_params=_params("parallel", "parallel"),
    )(x, gain, w, cos_t, sin_t)


def _cmp_proj_kernel(x_ref, plo_ref, phi_ref, a_ref, b_ref, ya_ref, yb_ref):
    x = x_ref[...]
    ya_ref[...] = _dot((x + plo_ref[...]).astype(BF16), a_ref[...])
    yb_ref[...] = _dot((x + phi_ref[...]).astype(BF16), b_ref[...])


def _cmp_mix_kernel(ya_ref, yb_ref, w2_ref, o_ref):
    n = ya_ref.shape[0]
    z = ya_ref[...] + pltpu.roll(yb_ref[...], n - 1, 0)
    h = (z * jax.nn.sigmoid(z)).astype(BF16)
    out = _dot(h, w2_ref[...])
    for g in range(NSA_GROUPS):
        o_ref[g] = out[:, NSA_HD * g:NSA_HD * (g + 1)].astype(o_ref.dtype)


def _compress(kv, pos, w1, w2, tc=256):
    b, s, _ = kv.shape
    g, hd = NSA_GROUPS, NSA_HD
    n = s // CMP_STRIDE
    wide = CMP_STRIDE * g * hd
    tc = min(tc, n)
    assert n % tc == 0
    x = kv.reshape(b, n, wide)
    eye = jnp.eye(g, dtype=F32)
    w1r = w1.reshape(CMP_BLOCK, hd, hd)
    mat_a = jnp.einsum("pde,gh->pgdhe", w1r[:CMP_STRIDE], eye).reshape(wide, g * hd).astype(BF16)
    mat_b = jnp.einsum("pde,gh->pgdhe", w1r[CMP_STRIDE:], eye).reshape(wide, g * hd).astype(BF16)
    p_lo = jnp.broadcast_to(pos[:CMP_STRIDE, None, :], (CMP_STRIDE, g, hd)).reshape(1, wide)
    p_hi = jnp.broadcast_to(pos[CMP_STRIDE:, None, :], (CMP_STRIDE, g, hd)).reshape(1, wide)
    w2bd = jnp.einsum("de,gh->gdhe", w2, eye).reshape(g * hd, g * hd).astype(BF16)
    vec = pl.BlockSpec((1, wide), lambda bi, i: (0, 0))
    mat = pl.BlockSpec((wide, g * hd), lambda bi, i: (0, 0))
    yspec = pl.BlockSpec((None, tc, g * hd), lambda bi, i: (bi, i, 0))
    ya, yb = pl.pallas_call(
        _cmp_proj_kernel,
        name="cmp_proj",
        grid=(b, n // tc),
        in_specs=[pl.BlockSpec((None, tc, wide), lambda bi, i: (bi, i, 0)), vec, vec, mat, mat],
        out_specs=[yspec, yspec],
        out_shape=[jax.ShapeDtypeStruct((b, n, g * hd), F32)] * 2,
        compiler_params=_params("parallel", "parallel"),
    )(x, p_lo, p_hi, mat_a, mat_b)
    full = pl.BlockSpec((None, n, g * hd), lambda bi: (bi, 0, 0))
    return pl.pallas_call(
        _cmp_mix_kernel,
        name="cmp_mix",
        grid=(b,),
        in_specs=[full, full, pl.BlockSpec((g * hd, g * hd), lambda bi: (0, 0))],
        out_specs=pl.BlockSpec((None, g, n, hd), lambda bi: (bi, 0, 0, 0)),
        out_shape=jax.ShapeDtypeStruct((b, g, n, hd), BF16),
        compiler_params=_params("parallel"),
    )(ya, yb, w2bd)


def _nsa_cw_kernel(q_ref, kc_ref, vc_ref, kw_ref, vw_ref, gates_ref, amat_ref, ocw_ref, idx_ref, *, chunk, tiles):
    t_end = (pl.program_id(2) + 1) * tiles * Q_BLOCK
    n_chunk = kc_ref.shape[0] // chunk
    last_cmp = (t_end - CMP_BLOCK) // CMP_STRIDE
    need = jnp.minimum(last_cmp // chunk + 1, n_chunk)

    def variant(nc):
        for sub in range(tiles):
            _nsa_cw_tile(q_ref, kc_ref, vc_ref, kw_ref, vw_ref, gates_ref, amat_ref, ocw_ref, idx_ref, sub=sub,
                         tiles=tiles, n_cmp=nc * chunk, n_sel=nc * chunk * CMP_STRIDE // SEL_BLOCK)

    for nc in range(1, n_chunk + 1):
        pl.when(need == nc)(functools.partial(variant, nc))


def _nsa_cw_tile(q_ref, kc_ref, vc_ref, kw_ref, vw_ref, gates_ref, amat_ref, ocw_ref, idx_ref, *, sub, tiles, n_cmp,
                 n_sel):
    qb = Q_BLOCK
    rows = NSA_HPG * qb
    t0 = (pl.program_id(2) * tiles + sub) * qb
    tok = slice(sub * qb, (sub + 1) * qb)
    q = q_ref[:, tok, :].reshape(rows, NSA_HD)
    t_row = t0 + lax.broadcasted_iota(jnp.int32, (rows, 1), 0) % qb

    s = _dot_nt(q, kc_ref[0:n_cmp, :])
    cmp_end = lax.broadcasted_iota(jnp.int32, (1, n_cmp), 1) * CMP_STRIDE + (CMP_BLOCK - 1)
    p = _softmax_rows(jnp.where(cmp_end <= t_row, s, NEG_INF))
    o_c = _dot(p.astype(BF16), vc_ref[0:n_cmp, :])
    p_grp = p[0:qb] + p[qb:2 * qb] + p[2 * qb:3 * qb] + p[3 * qb:4 * qb]
    p_hi = p_grp.astype(BF16)
    p_lo = (p_grp - p_hi.astype(F32)).astype(BF16)
    amat = amat_ref[0:n_sel, 0:n_cmp]
    imp = _dot_nt(amat, p_hi) + _dot_nt(amat, p_lo)

    t_tok = t0 + lax.broadcasted_iota(jnp.int32, (1, qb), 1)
    cur = (t_tok // SEL_BLOCK).astype(F32)
    blk = lax.broadcasted_iota(jnp.int32, (n_sel, 1), 0).astype(F32)
    forced = (blk == 0.0) | (blk == cur - 1.0)
    score = jnp.where(forced, jnp.inf, jnp.where(blk < cur, imp, NEG_INF))
    score = jnp.where(blk == cur, NEG_INF, score)
    slot = lax.broadcasted_iota(jnp.int32, (N_SELECT, qb), 0)
    sel = jnp.where(slot == 0, cur, 0.0)
    for n in range(1, N_SELECT):
        m = jnp.max(score, axis=0, keepdims=True)
        pick = jnp.min(jnp.where(score == m, blk, float(n_sel)), axis=0, keepdims=True)
        score = jnp.where(blk == pick, NEG_INF, score)
        sel = jnp.where(slot == n, pick, sel)
    idx_ref[sub] = sel.astype(jnp.int32)

    w_len = WINDOW + qb
    w_start = pl.multiple_of(jnp.maximum(t0 - WINDOW, 0), qb)
    s_w = _dot_nt(q, kw_ref[pl.ds(w_start, w_len), :])
    kpos = w_start + lax.broadcasted_iota(jnp.int32, (1, w_len), 1)
    mask_w = (kpos <= t_row) & (kpos > t_row - WINDOW)
    p_w = _softmax_rows(jnp.where(mask_w, s_w, NEG_INF))
    o_w = _dot(p_w.astype(BF16), vw_ref[pl.ds(w_start, w_len), :])

    gates = gates_ref[tok, :]
    outs = []
    for h in range(NSA_HPG):
        g_c = gates[:, 3 * h:3 * h + 1]
        g_w = gates[:, 3 * h + 2:3 * h + 3]
        outs.append(g_c * o_c[h * qb:(h + 1) * qb] + g_w * o_w[h * qb:(h + 1) * qb])
    ocw_ref[tok, :] = jnp.concatenate(outs, axis=1)


def _nsa_cmp_win(q, k_cmp, v_cmp, k_w, v_w, gates, amat, chunk=2 * LANES, tiles=4):
    b, _, s, hd = q.shape
    g = NSA_GROUPS
    n_cmp = k_cmp.shape[2]
    cmp_spec = pl.BlockSpec((None, None, n_cmp, hd), lambda bi, gi, j: (bi, gi, 0, 0))
    win_spec = pl.BlockSpec((None, None, s, hd), lambda bi, gi, j: (bi, gi, 0, 0))
    chunk = min(n_cmp, chunk)
    tiles = min(tiles, s // Q_BLOCK)
    step = tiles * Q_BLOCK
    assert n_cmp % chunk == 0 and s % step == 0 and (chunk * CMP_STRIDE) % step == 0
    return pl.pallas_call(
        functools.partial(_nsa_cw_kernel, chunk=chunk, tiles=tiles),
        name="nsa_cw",
        grid=(b, g, s // step),
        in_specs=[pl.BlockSpec((None, NSA_HPG, step, hd), lambda bi, gi, j: (bi, gi, j, 0)),
                  cmp_spec, cmp_spec, win_spec, win_spec,
                  pl.BlockSpec((None, step, LANES), lambda bi, gi, j: (bi, j, gi)),
                  pl.BlockSpec(amat.shape, lambda bi, gi, j: (0, 0))],
        out_specs=[pl.BlockSpec((None, step, NSA_HPG * hd), lambda bi, gi, j: (bi, j, gi)),
                   pl.BlockSpec((None, None, tiles, N_SELECT, Q_BLOCK), lambda bi, gi, j: (bi, gi, j, 0, 0))],
        out_shape=[jax.ShapeDtypeStruct((b, s, NSA_HEADS * hd), F32),
                   jax.ShapeDtypeStruct((b, g, s // Q_BLOCK, N_SELECT, Q_BLOCK), jnp.int32)],
        compiler_params=_params("parallel", "parallel", "arbitrary"),
    )(q, k_cmp, v_cmp, k_w, v_w, gates, amat)


def _roll_rows(x, shift):
    return jnp.concatenate([pltpu.roll(x[r:r + 8], shift, 0) for r in range(0, x.shape[0], 8)], axis=0)


def _nsa_sel_kernel(idx_ref, q_ref, kv2_ref, o_ref, s_a, s_b, v_a, v_b, *, tq, pairs):
    half_blk = SEL_BLOCK // 2
    n_col = N_SELECT * half_blk
    t0 = pl.program_id(2) * tq
    row = lax.broadcasted_iota(jnp.int32, (16, n_col), 0)
    col = lax.broadcasted_iota(jnp.int32, (16, n_col), 1)
    pos0 = 2 * col + (row % 8) // NSA_HPG
    in_slot0 = col < half_blk
    second = (row >= 8).astype(jnp.int32)
    row8 = lax.broadcasted_iota(jnp.int32, (8, LANES), 0)

    def stage_scores(grp, s_buf, v_buf):
        for u in range(pairs):
            tp = grp * pairs + u
            if u % 2 == 0:
                q_four = q_ref[pl.ds(pl.multiple_of(8 * tp, 16), 16), :].astype(F32)
            q8 = q_four[8 * (u % 2):8 * (u % 2) + 8]
            wide = jnp.concatenate([q8, jnp.zeros_like(q8)], axis=1)
            shifted = pltpu.roll(wide, NSA_HD, 1)
            q16 = jnp.concatenate(
                [jnp.where(row8 < NSA_HPG, wide, pltpu.roll(shifted, NSA_HPG, 0)),
                 jnp.where(row8 < NSA_HPG, pltpu.roll(wide, NSA_HPG, 0), shifted)], axis=0).astype(BF16)
            halves = []
            for w in range(2):
                tok = 2 * tp + w
                base = (tok // Q_BLOCK) * (N_SELECT * Q_BLOCK) + tok % Q_BLOCK
                blocks = []
                for n in range(N_SELECT):
                    off = pl.multiple_of(idx_ref[0, base + n * Q_BLOCK] * half_blk, half_blk)
                    blocks.append(kv2_ref[pl.ds(off, half_blk), :])
                keys = jnp.concatenate([blk[:, :LANES] for blk in blocks], axis=0)
                v_buf[(2 * u + w) * n_col:(2 * u + w + 1) * n_col, :] = jnp.concatenate(
                    [blk[:, LANES:] for blk in blocks], axis=0)
                halves.append(_dot_nt(q16, keys)[8 * w:8 * w + 8])
            s_buf[16 * u:16 * u + 16, :] = jnp.concatenate(halves, axis=0)

    def stage_output(grp, s_buf, v_buf):
        for u in range(pairs):
            tp = grp * pairs + u
            ta = t0 + 2 * tp
            n_valid = jnp.minimum(ta // SEL_BLOCK + 1, N_SELECT)
            ok = jnp.where(in_slot0, pos0 - (ta % SEL_BLOCK + second), col - n_valid * half_blk + 1) <= 0
            s = jnp.where(ok, s_buf[16 * u:16 * u + 16, :], NEG_INF)
            m = jnp.max(s, axis=-1, keepdims=True)
            m = jnp.maximum(m, _roll_rows(m, NSA_HPG))
            e = jnp.exp(s - m)
            l = jnp.sum(e, axis=-1, keepdims=True)
            l = l + _roll_rows(l, NSA_HPG)
            p = (e / l).astype(BF16)
            xa = _dot(p, v_buf[(2 * u) * n_col:(2 * u + 1) * n_col, :])
            xb = _dot(p, v_buf[(2 * u + 1) * n_col:(2 * u + 2) * n_col, :])
            x = jnp.concatenate([xa[:8], xb[8:]], axis=0)
            y = x + pltpu.roll(_roll_rows(x, NSA_HPG), NSA_HD, 1)
            out = jnp.where(row8 < NSA_HPG, y[:8], pltpu.roll(y[8:], NSA_HPG, 0))
            o_ref[pl.ds(pl.multiple_of(8 * tp, 8), 8), :] = out

    n_grp = tq // (2 * pairs)
    stage_scores(0, s_a, v_a)

    def double_step(d):
        stage_scores(2 * d + 1, s_b, v_b)
        stage_output(2 * d, s_a, v_a)
        stage_scores(2 * d + 2, s_a, v_a)
        stage_output(2 * d + 1, s_b, v_b)

    n_double = n_grp // 2 - 1
    unroll = max(u for u in (5, 3, 1) if n_double % u == 0)

    def body(k, carry):
        for r in range(unroll):
            double_step(unroll * k + r)
        return carry

    lax.fori_loop(0, n_double // unroll, body, 0)
    stage_scores(n_grp - 1, s_b, v_b)
    stage_output(n_grp - 2, s_a, v_a)
    stage_output(n_grp - 1, s_b, v_b)


def _nsa_selected(idx, q_rows, kv2, tq=512, pairs=8):
    b, g = idx.shape[:2]
    s = kv2.shape[2] * 2
    tq = min(tq, s)
    nt = s // tq
    assert (tq // (2 * pairs)) % 2 == 0 and tq % Q_BLOCK == 0
    idx_flat = idx.reshape(b * g * nt, 1, tq * N_SELECT)
    n_col = N_SELECT * SEL_BLOCK // 2
    scratch = [pltpu.VMEM((pairs * 16, n_col), F32)] * 2 + [pltpu.VMEM((pairs * 2 * n_col, LANES), BF16)] * 2
    return pl.pallas_call(
        functools.partial(_nsa_sel_kernel, tq=tq, pairs=pairs),
        scratch_shapes=scratch,
        name="nsa_sel",
        grid=(b, g, nt),
        in_specs=[pl.BlockSpec((None, 1, tq * N_SELECT), lambda bi, gi, j: ((bi * g + gi) * nt + j, 0, 0),
                               memory_space=pltpu.SMEM),
                  pl.BlockSpec((None, None, tq * NSA_HPG, NSA_HD), lambda bi, gi, j: (bi, gi, j, 0)),
                  pl.BlockSpec((None, None, s // 2, 2 * LANES), lambda bi, gi, j: (bi, gi, 0, 0))],
        out_specs=pl.BlockSpec((None, None, tq * NSA_HPG, LANES), lambda bi, gi, j: (bi, gi, j, 0)),
        out_shape=jax.ShapeDtypeStruct((b, g, s * NSA_HPG, LANES), F32),
        compiler_params=_params("parallel", "parallel", "arbitrary"),
    )(idx_flat, q_rows, kv2)


def _nsa_out_kernel(x_ref, ocw_ref, os0_ref, os1_ref, os2_ref, os3_ref, gates_ref, emat_ref, w_ref, gain_ref,
                    o_ref):
    tm = x_ref.shape[0]
    left = lax.broadcasted_iota(jnp.int32, (tm, LANES), 1) < NSA_HD
    tiles = []
    for os_ref in (os0_ref, os1_ref, os2_ref, os3_ref):
        heads = [os_ref[pl.ds(h, tm, stride=NSA_HPG), :] for h in range(NSA_HPG)]
        for h in range(0, NSA_HPG, 2):
            tiles.append(jnp.where(left, heads[h], pltpu.roll(heads[h + 1], NSA_HD, 1)))
    o_s = jnp.concatenate(tiles, axis=1)
    g_s = _dot_split(gates_ref[...], emat_ref[...])
    a = (ocw_ref[...] + g_s * o_s).astype(BF16)
    o_ref[...] = x_ref[...] + _rms(_dot(a, w_ref[...]), gain_ref[...])


def _nsa_out_proj(x, o_cw, o_s, gates, emat, w, gain, tm=512):
    b, s, d = x.shape
    tm = min(tm, s)
    assert s % tm == 0
    row = pl.BlockSpec((None, tm, d), lambda bi, i: (bi, i, 0))
    sel = [pl.BlockSpec((None, None, tm * NSA_HPG, LANES), functools.partial(lambda bi, i, gi: (bi, gi, i, 0), gi=gi))
           for gi in range(NSA_GROUPS)]
    return pl.pallas_call(
        _nsa_out_kernel,
        name="nsa_out",
        grid=(b, s // tm),
        in_specs=[row, row, *sel,
                  pl.BlockSpec((None, tm, gates.shape[2]), lambda bi, i: (bi, i, 0)),
                  pl.BlockSpec(emat.shape, lambda bi, i: (0, 0)),
                  pl.BlockSpec(w.shape, lambda bi, i: (0, 0)),
                  pl.BlockSpec((1, d), lambda bi, i: (0, 0))],
        out_specs=row,
        out_shape=jax.ShapeDtypeStruct((b, s, d), F32),
        compiler_params=_params("parallel", "parallel"),
    )(x, o_cw, o_s, o_s, o_s, o_s, gates, emat, w, gain)


def _proj_out_kernel(x_ref, a_ref, w_ref, gain_ref, o_ref):
    o_ref[...] = x_ref[...] + _rms(_dot(a_ref[...], w_ref[...]), gain_ref[...])


def _proj_out(x, a, w, gain, tm=1024):
    t, d = x.shape
    tm = min(tm, t)
    assert t % tm == 0
    row = pl.BlockSpec((tm, d), lambda i: (i, 0))
    return pl.pallas_call(
        _proj_out_kernel,
        name="proj_out",
        grid=(t // tm,),
        in_specs=[row, pl.BlockSpec((tm, a.shape[1]), lambda i: (i, 0)),
                  pl.BlockSpec(w.shape, lambda i: (0, 0)),
                  pl.BlockSpec((1, d), lambda i: (0, 0))],
        out_specs=row,
        out_shape=jax.ShapeDtypeStruct((t, d), F32),
        compiler_params=_params("parallel"),
    )(x, a, w, gain)


def _nsa_mixer(x, gain_pre, gain_post, w_in, w_out, cmp_pos, cmp_w1, cmp_w2):
    b, s, d = x.shape
    g, hd, hpg = NSA_GROUPS, NSA_HD, NSA_HPG
    n_gate = 3 * hpg
    q_cols = NSA_HEADS * hd
    kv_end = q_cols + 6 * NSA_KV_COLS

    gate_w = w_in[:, kv_end:].reshape(d, g, n_gate)
    gate_w = jnp.pad(gate_w, ((0, 0), (0, 0), (0, LANES - n_gate))).reshape(d, g * LANES)
    w_all = jnp.concatenate([w_in[:, :kv_end], gate_w], axis=1).astype(BF16)

    half = hd // 2
    inv = ROPE_THETA ** (-jnp.arange(half, dtype=F32) / half)
    ang = jnp.arange(s).astype(F32)[:, None] * inv[None, :]
    cos, sin = jnp.cos(ang), jnp.sin(ang)
    cos_t = jnp.tile(cos, (1, LANES // half))
    sin_t = jnp.tile(jnp.concatenate([-sin, sin], axis=1), (1, LANES // hd))

    q, k_c, v_c, k_s, v_s, k_w, v_w, gates = _nsa_in_proj(x, gain_pre, w_all, cos_t, sin_t)
    k_cmp = _compress(k_c, cmp_pos[0], cmp_w1[0], cmp_w2[0])
    v_cmp = _compress(v_c, cmp_pos[1], cmp_w1[1], cmp_w2[1])

    n_cmp_rows = s // CMP_STRIDE
    n_sel = s // SEL_BLOCK
    ratio = SEL_BLOCK // CMP_STRIDE
    c_id = jnp.arange(n_cmp_rows)[:, None]
    n_id = jnp.arange(n_sel)[None, :]
    lo = ratio * n_id - (CMP_BLOCK // CMP_STRIDE - 1)
    amat = ((c_id >= lo) & (c_id <= ratio * n_id + ratio - 1) & (c_id < n_cmp_rows - 1)).astype(BF16)

    o_cw, idx = _nsa_cmp_win(q, k_cmp, v_cmp, k_w, v_w, gates, amat.T)

    q_rows = q.reshape(b, g, hpg, s, hd).transpose(0, 1, 3, 2, 4).reshape(b, g, s * hpg, hd)
    kv2 = jnp.concatenate([k_s.reshape(b, g, s // 2, 2 * hd), v_s.reshape(b, g, s // 2, 2 * hd)], axis=-1)
    o_s = _nsa_selected(idx, q_rows, kv2)

    r_id = jnp.arange(g * LANES)[:, None]
    col = jnp.arange(q_cols)[None, :]
    emat = (r_id == (col // (hpg * hd)) * LANES + 3 * ((col // hd) % hpg) + 1).astype(BF16)

    return _nsa_out_proj(x, o_cw, o_s, gates, emat, w_out.astype(BF16), gain_post)


def _hgrn_in_kernel(x_ref, gain_ref, w_ref, lb_ref, q_ref, k_ref, lf_ref, v_ref, gs_ref):
    xn = _rms(x_ref[...], gain_ref[...]).astype(BF16)
    n_chunk = D_MODEL // 256

    def chunk(c):
        return _dot(xn, w_ref[:, 256 * c:256 * (c + 1)])

    for c in range(n_chunk):
        cols = slice(256 * c, 256 * (c + 1))
        r = chunk(c)
        q_ref[:, cols] = r * jax.nn.sigmoid(r)
        f = chunk(n_chunk + c)
        lb = lb_ref[:, cols]
        sig = jax.nn.sigmoid(f)
        lf_ref[:, cols] = jnp.log(lb + (1.0 - lb) * sig)
        k_ref[:, cols] = (1.0 - lb) * (1.0 - sig)
        v_ref[:, cols] = chunk(2 * n_chunk + c)
        r = chunk(3 * n_chunk + c)
        gs_ref[:, cols] = r * jax.nn.sigmoid(r)


def _hgrn_in_proj(x, gain, w, lb, tm=512):
    t, d = x.shape
    tm = min(tm, t)
    assert t % tm == 0
    row = pl.BlockSpec((tm, d), lambda i: (i, 0))
    vec = pl.BlockSpec((1, d), lambda i: (0, 0))
    return pl.pallas_call(
        _hgrn_in_kernel,
        name="hgrn_in",
        grid=(t // tm,),
        in_specs=[row, vec, pl.BlockSpec(w.shape, lambda i: (0, 0)), vec],
        out_specs=[row] * 5,
        out_shape=[jax.ShapeDtypeStruct((t, d), F32)] * 5,
        compiler_params=_params("parallel"),
    )(x, gain, w, lb)


def _hgrn_scan_kernel(q_ref, k_ref, lf_ref, v_ref, gs_ref, gain_ref, o_ref, state_ref, *, rows, heads):
    sub = HGRN_SUB
    n_sub = rows // sub

    @pl.when(pl.program_id(2) == 0)
    def _():
        state_ref[...] = jnp.zeros_like(state_ref)

    r_id = lax.broadcasted_iota(jnp.int32, (rows, rows), 0)
    c_id = lax.broadcasted_iota(jnp.int32, (rows, rows), 1)
    tri = jnp.where(((r_id // sub) == (c_id // sub)) & (c_id <= r_id), 1.0, 0.0)
    row_id = lax.broadcasted_iota(jnp.int32, (sub, HGRN_DK), 0)

    for h in range(heads):
        lanes = slice(LANES * h, LANES * (h + 1))
        cum = jnp.dot(tri, lf_ref[:, lanes], preferred_element_type=F32, precision=lax.Precision.HIGHEST) * LOG2E
        q = q_ref[:, lanes]
        k = k_ref[:, lanes]
        v = v_ref[:, lanes]
        q_dec = (q * jnp.exp2(cum)).astype(BF16)

        intra, update, decay = [], [], []
        for i in range(n_sub):
            sl = slice(sub * i, sub * (i + 1))
            qi, ki, vi, ci = q[sl], k[sl], v[sl], cum[sl]
            total = ci[sub - 1:sub]
            o_i = None
            for j in range(sub):
                w = jnp.exp2(jnp.where(row_id >= j, ci - ci[j:j + 1], NEG_INF))
                att = jnp.sum(qi * w * ki[j:j + 1], axis=-1, keepdims=True)
                o_i = att * vi[j:j + 1] if o_i is None else o_i + att * vi[j:j + 1]
            intra.append(o_i)
            update.append(_dot_tn(vi.astype(BF16), (ki * jnp.exp2(total - ci)).astype(BF16)))
            decay.append(jnp.exp2(total))

        state = state_ref[h]
        outs = []
        for i in range(n_sub):
            outs.append(intra[i] + _dot_nt(q_dec[sub * i:sub * (i + 1)], state.astype(BF16)))
            state = state * decay[i] + update[i]
        state_ref[h] = state

        o = jnp.concatenate(outs, axis=0)
        o_ref[:, lanes] = (_rms(o, gain_ref[...]) * gs_ref[:, lanes]).astype(o_ref.dtype)


def _hgrn_scan(q, k, lf, v, gs, gain, rows=128, heads=HGRN_HEADS):
    b, s, d = q.shape
    rows = min(rows, s)
    blk = pl.BlockSpec((None, rows, heads * LANES), lambda bi, h, l: (bi, l, h))
    return pl.pallas_call(
        functools.partial(_hgrn_scan_kernel, rows=rows, heads=heads),
        name="hgrn_scan",
        grid=(b, d // (heads * LANES), s // rows),
        in_specs=[blk] * 5 + [pl.BlockSpec((1, LANES), lambda bi, h, l: (0, 0))],
        out_specs=blk,
        out_shape=jax.ShapeDtypeStruct((b, s, d), BF16),
        scratch_shapes=[pltpu.VMEM((heads, LANES, HGRN_DK), F32)],
        compiler_params=_params("parallel", "parallel", "arbitrary"),
    )(q, k, lf, v, gs, gain)


def _hgrn_mixer(x, gain_pre, gain_post, w_in, w_out, norm_gain, lower_bound):
    b, s, d = x.shape
    x2 = x.reshape(b * s, d)
    q, k, lf, v, gs = _hgrn_in_proj(x2, gain_pre, w_in.astype(BF16), lower_bound.reshape(1, d))
    shape = (b, s, d)
    o = _hgrn_scan(q.reshape(shape), k.reshape(shape), lf.reshape(shape), v.reshape(shape),
                   gs.reshape(shape), norm_gain.reshape(1, LANES))
    return _proj_out(x2, o.reshape(b * s, d), w_out.astype(BF16), gain_post).reshape(shape)


def kernel(x, p, norm_gains, ffn_w_in, ffn_w_out, ple_w_in, ple_w_gate, nsa_w_in, nsa_w_out, nsa_cmp_pos,
           nsa_cmp_w1, nsa_cmp_w2, hgrn_w_in, hgrn_w_out, hgrn_norm, hgrn_lb_logits):
    b, s, d = x.shape
    depth = norm_gains.shape[0]
    lb_sm = jax.nn.softmax(hgrn_lb_logits.astype(F32), axis=0)
    lower_bounds = jnp.cumsum(lb_sm, axis=0) - lb_sm[0]
    for layer in range(depth):
        ng = norm_gains[layer].reshape(8, 1, d)
        x2 = x.reshape(b * s, d)
        x2 = _ffn_block(x2, ng[0], ng[1], ffn_w_in[layer, 0].astype(BF16), ffn_w_out[layer, 0].astype(BF16))
        x = x2.reshape(b, s, d)
        j = layer // 2
        if layer % 2 == 0:
            x = _nsa_mixer(x, ng[2], ng[3], nsa_w_in[j], nsa_w_out[j], nsa_cmp_pos[j], nsa_cmp_w1[j],
                           nsa_cmp_w2[j])
        else:
            x = _hgrn_mixer(x, ng[2], ng[3], hgrn_w_in[j], hgrn_w_out[j], hgrn_norm[j], lower_bounds[layer])
        x2 = x.reshape(b * s, d)
        ple = (p[layer].reshape(b * s, -1), ng[6], ng[7], ple_w_gate[layer].astype(BF16),
               ple_w_in[layer].astype(BF16))
        x2 = _ffn_block(x2, ng[4], ng[5], ffn_w_in[layer, 1].astype(BF16), ffn_w_out[layer, 1].astype(BF16), ple)
        x = x2.reshape(b, s, d)
    return x
```

```python
import functools

import jax
import jax.numpy as jnp
from jax import lax
from jax.experimental import pallas as pl
from jax.experimental.pallas import tpu as pltpu

F32 = jnp.float32
BF16 = jnp.bfloat16
NEG_INF = float("-inf")

D_MODEL = 1024
NORM_EPS = 1e-6
ROPE_THETA = 10000.0

NSA_HEADS = 16
NSA_GROUPS = 4
NSA_HPG = NSA_HEADS // NSA_GROUPS
NSA_HD = 64
NSA_KV_COLS = NSA_GROUPS * NSA_HD
CMP_BLOCK = 32
CMP_STRIDE = 16
SEL_BLOCK = 64
N_SELECT = 16
WINDOW = 512
Q_BLOCK = 128

HGRN_HEADS = 8
HGRN_DK = 128
HGRN_SUB = 16
LOG2E = 1.4426950408889634

LANES = 128
VMEM_LIMIT = 56 * 1024 * 1024


def _params(*sem):
    return pltpu.CompilerParams(dimension_semantics=sem, vmem_limit_bytes=VMEM_LIMIT)


def _rms(x, gain):
    return x * lax.rsqrt(jnp.mean(x * x, axis=-1, keepdims=True) + NORM_EPS) * gain


def _dot(a, b):
    return jnp.dot(a, b, preferred_element_type=F32)


def _dot_nt(a, b):
    return lax.dot_general(a, b, (((1,), (1,)), ((), ())), preferred_element_type=F32)


def _dot_tn(a, b):
    return lax.dot_general(a, b, (((0,), (0,)), ((), ())), preferred_element_type=F32)


def _dot_split(x, w):
    hi = x.astype(BF16)
    lo = (x - hi.astype(F32)).astype(BF16)
    return _dot(hi, w) + _dot(lo, w)


def _softmax_rows(s):
    m = jnp.max(s, axis=-1, keepdims=True)
    m = jnp.where(m == NEG_INF, 0.0, m)
    e = jnp.exp(s - m)
    return e / jnp.maximum(jnp.sum(e, axis=-1, keepdims=True), 1e-30)


def _ffn_kernel(x_ref, g0_ref, g1_ref, wi_ref, wo_ref, *rest, with_ple):
    x = x_ref[...]
    dff = wo_ref.shape[0]
    gu = _dot(_rms(x, g0_ref[...]).astype(BF16), wi_ref[...])
    g = gu[:, :dff]
    h = (g * jax.nn.sigmoid(g) * gu[:, dff:]).astype(BF16)
    x = x + 0.5 * _rms(_dot(h, wo_ref[...]), g1_ref[...])
    if with_ple:
        p_ref, g2_ref, g3_ref, wg_ref, wp_ref, o_ref = rest
        gate = jax.nn.sigmoid(_dot(_rms(x, g2_ref[...]).astype(BF16), wg_ref[...]))
        e = _dot(p_ref[...].astype(BF16), wp_ref[...])
        x = x + _rms(e * gate, g3_ref[...])
    else:
        o_ref, = rest
    o_ref[...] = x


def _ffn_block(x, g_pre, g_post, w_in, w_out, ple=None, tm=512):
    t, d = x.shape
    tm = min(tm, t)
    assert t % tm == 0
    row = pl.BlockSpec((tm, d), lambda i: (i, 0))
    vec = pl.BlockSpec((1, d), lambda i: (0, 0))

    def resident(w):
        return pl.BlockSpec(w.shape, lambda i: (0, 0), pipeline_mode=pl.Buffered(1))

    args = [x, g_pre, g_post, w_in, w_out]
    specs = [row, vec, vec, resident(w_in), resident(w_out)]
    if ple is not None:
        p, g2, g3, w_gate, w_p = ple
        args += [p, g2, g3, w_gate, w_p]
        specs += [pl.BlockSpec((tm, p.shape[1]), lambda i: (i, 0)), vec, vec, resident(w_gate), resident(w_p)]
    return pl.pallas_call(
        functools.partial(_ffn_kernel, with_ple=ple is not None),
        name="ffn_ple" if ple is not None else "ffn",
        grid=(t // tm,),
        in_specs=specs,
        out_specs=row,
        out_shape=jax.ShapeDtypeStruct((t, d), F32),
        compiler_params=_params("parallel"),
    )(*args)


def _rope_tile(xb, cos_t, sin_t, first_half):
    fwd = pltpu.roll(xb, LANES - NSA_HD // 2, 1)
    bwd = pltpu.roll(xb, NSA_HD // 2, 1)
    return xb * cos_t + jnp.where(first_half, fwd, bwd) * sin_t


def _nsa_in_kernel(x_ref, gain_ref, w_ref, cos_ref, sin_ref,
                   q_ref, kc_ref, vc_ref, ks_ref, vs_ref, kw_ref, vw_ref, gates_ref):
    xn = _rms(x_ref[...], gain_ref[...]).astype(BF16)
    cos_t = cos_ref[...]
    sin_t = sin_ref[...]
    tm = cos_t.shape[0]
    lane = lax.broadcasted_iota(jnp.int32, (tm, LANES), 1)
    first_half = (lane % NSA_HD) < (NSA_HD // 2)

    def chunk(c):
        return _dot(xn, w_ref[:, 256 * c:256 * (c + 1)])

    def roped(r):
        return [_rope_tile(r[:, LANES * h:LANES * (h + 1)], cos_t, sin_t, first_half) for h in range(2)]

    def store_heads(ref, base, tiles, scale):
        for h, tile in enumerate(tiles):
            ref[base + 2 * h] = (tile[:, :NSA_HD] * scale).astype(ref.dtype)
            ref[base + 2 * h + 1] = (tile[:, NSA_HD:] * scale).astype(ref.dtype)

    for c in range(4):
        store_heads(q_ref, 4 * c, roped(chunk(c)), NSA_HD ** -0.5)
    kc = roped(chunk(4))
    kc_ref[...] = jnp.concatenate(kc, axis=1)
    vc_ref[...] = chunk(5)
    store_heads(ks_ref, 0, roped(chunk(6)), 1.0)
    r = chunk(7)
    store_heads(vs_ref, 0, [r[:, :LANES], r[:, LANES:]], 1.0)
    store_heads(kw_ref, 0, roped(chunk(8)), 1.0)
    r = chunk(9)
    store_heads(vw_ref, 0, [r[:, :LANES], r[:, LANES:]], 1.0)
    gates_ref[:, :256] = jax.nn.sigmoid(chunk(10))
    gates_ref[:, 256:] = jax.nn.sigmoid(chunk(11))


def _nsa_in_proj(x, gain, w, cos_t, sin_t, tm=512):
    b, s, d = x.shape
    tm = min(tm, s)
    assert s % tm == 0
    g, hd = NSA_GROUPS, NSA_HD
    kv_heads = pl.BlockSpec((None, g, tm, hd), lambda bi, i: (bi, 0, i, 0))
    kv_flat = pl.BlockSpec((None, tm, g * hd), lambda bi, i: (bi, i, 0))
    tab = pl.BlockSpec((tm, LANES), lambda bi, i: (i, 0))
    return pl.pallas_call(
        _nsa_in_kernel,
        name="nsa_in",
        grid=(b, s // tm),
        in_specs=[pl.BlockSpec((None, tm, d), lambda bi, i: (bi, i, 0)),
                  pl.BlockSpec((1, d), lambda bi, i: (0, 0)),
                  pl.BlockSpec(w.shape, lambda bi, i: (0, 0)),
                  tab, tab],
        out_specs=[pl.BlockSpec((None, NSA_HEADS, tm, hd), lambda bi, i: (bi, 0, i, 0)),
                   kv_flat, kv_flat, kv_heads, kv_heads, kv_heads, kv_heads,
                   pl.BlockSpec((None, tm, 4 * LANES), lambda bi, i: (bi, i, 0))],
        out_shape=[jax.ShapeDtypeStruct((b, NSA_HEADS, s, hd), BF16),
                   jax.ShapeDtypeStruct((b, s, g * hd), F32),
                   jax.ShapeDtypeStruct((b, s, g * hd), F32),
                   jax.ShapeDtypeStruct((b, g, s, hd), BF16),
                   jax.ShapeDtypeStruct((b, g, s, hd), BF16),
                   jax.ShapeDtypeStruct((b, g, s, hd), BF16),
                   jax.ShapeDtypeStruct((b, g, s, hd), BF16),
                   jax.ShapeDtypeStruct((b, s, 4 * LANES), F32)],
        compiler_params=_params("parallel", "parallel"),
    )(x, gain, w, cos_t, sin_t)


def _cmp_proj_kernel(x_ref, plo_ref, phi_ref, a_ref, b_ref, ya_ref, yb_ref):
    x = x_ref[...]
    ya_ref[...] = _dot((x + plo_ref[...]).astype(BF16), a_ref[...])
    yb_ref[...] = _dot((x + phi_ref[...]).astype(BF16), b_ref[...])


def _cmp_mix_kernel(ya_ref, yb_ref, w2_ref, o_ref):
    n = ya_ref.shape[0]
    z = ya_ref[...] + pltpu.roll(yb_ref[...], n - 1, 0)
    h = (z * jax.nn.sigmoid(z)).astype(BF16)
    out = _dot(h, w2_ref[...])
    for g in range(NSA_GROUPS):
        o_ref[g] = out[:, NSA_HD * g:NSA_HD * (g + 1)].astype(o_ref.dtype)


def _compress(kv, pos, w1, w2, tc=256):
    b, s, _ = kv.shape
    g, hd = NSA_GROUPS, NSA_HD
    n = s // CMP_STRIDE
    wide = CMP_STRIDE * g * hd
    tc = min(tc, n)
    assert n % tc == 0
    x = kv.reshape(b, n, wide)
    eye = jnp.eye(g, dtype=F32)
    w1r = w1.reshape(CMP_BLOCK, hd, hd)
    mat_a = jnp.einsum("pde,gh->pgdhe", w1r[:CMP_STRIDE], eye).reshape(wide, g * hd).astype(BF16)
    mat_b = jnp.einsum("pde,gh->pgdhe", w1r[CMP_STRIDE:], eye).reshape(wide, g * hd).astype(BF16)
    p_lo = jnp.broadcast_to(pos[:CMP_STRIDE, None, :], (CMP_STRIDE, g, hd)).reshape(1, wide)
    p_hi = jnp.broadcast_to(pos[CMP_STRIDE:, None, :], (CMP_STRIDE, g, hd)).reshape(1, wide)
    w2bd = jnp.einsum("de,gh->gdhe", w2, eye).reshape(g * hd, g * hd).astype(BF16)
    vec = pl.BlockSpec((1, wide), lambda bi, i: (0, 0))
    mat = pl.BlockSpec((wide, g * hd), lambda bi, i: (0, 0))
    yspec = pl.BlockSpec((None, tc, g * hd), lambda bi, i: (bi, i, 0))
    ya, yb = pl.pallas_call(
        _cmp_proj_kernel,
        name="cmp_proj",
        grid=(b, n // tc),
        in_specs=[pl.BlockSpec((None, tc, wide), lambda bi, i: (bi, i, 0)), vec, vec, mat, mat],
        out_specs=[yspec, yspec],
        out_shape=[jax.ShapeDtypeStruct((b, n, g * hd), F32)] * 2,
        compiler_params=_params("parallel", "parallel"),
    )(x, p_lo, p_hi, mat_a, mat_b)
    full = pl.BlockSpec((None, n, g * hd), lambda bi: (bi, 0, 0))
    return pl.pallas_call(
        _cmp_mix_kernel,
        name="cmp_mix",
        grid=(b,),
        in_specs=[full, full, pl.BlockSpec((g * hd, g * hd), lambda bi: (0, 0))],
        out_specs=pl.BlockSpec((None, g, n, hd), lambda bi: (bi, 0, 0, 0)),
        out_shape=jax.ShapeDtypeStruct((b, g, n, hd), BF16),
        compiler_params=_params("parallel"),
    )(ya, yb, w2bd)


def _nsa_cw_kernel(q_ref, kc_ref, vc_ref, kw_ref, vw_ref, gates_ref, amat_ref, ocw_ref, idx_ref, *, chunk, tiles):
    t_end = (pl.program_id(2) + 1) * tiles * Q_BLOCK
    n_chunk = kc_ref.shape[0] // chunk
    last_cmp = (t_end - CMP_BLOCK) // CMP_STRIDE
    need = jnp.minimum(last_cmp // chunk + 1, n_chunk)

    def variant(nc):
        for sub in range(tiles):
            _nsa_cw_tile(q_ref, kc_ref, vc_ref, kw_ref, vw_ref, gates_ref, amat_ref, ocw_ref, idx_ref, sub=sub,
                         tiles=tiles, n_cmp=nc * chunk, n_sel=nc * chunk * CMP_STRIDE // SEL_BLOCK)

    for nc in range(1, n_chunk + 1):
        pl.when(need == nc)(functools.partial(variant, nc))


def _nsa_cw_tile(q_ref, kc_ref, vc_ref, kw_ref, vw_ref, gates_ref, amat_ref, ocw_ref, idx_ref, *, sub, tiles, n_cmp,
                 n_sel):
    qb = Q_BLOCK
    rows = NSA_HPG * qb
    t0 = (pl.program_id(2) * tiles + sub) * qb
    tok = slice(sub * qb, (sub + 1) * qb)
    q = q_ref[:, tok, :].reshape(rows, NSA_HD)
    t_row = t0 + lax.broadcasted_iota(jnp.int32, (rows, 1), 0) % qb

    s = _dot_nt(q, kc_ref[0:n_cmp, :])
    cmp_end = lax.broadcasted_iota(jnp.int32, (1, n_cmp), 1) * CMP_STRIDE + (CMP_BLOCK - 1)
    p = _softmax_rows(jnp.where(cmp_end <= t_row, s, NEG_INF))
    o_c = _dot(p.astype(BF16), vc_ref[0:n_cmp, :])
    p_grp = p[0:qb] + p[qb:2 * qb] + p[2 * qb:3 * qb] + p[3 * qb:4 * qb]
    p_hi = p_grp.astype(BF16)
    p_lo = (p_grp - p_hi.astype(F32)).astype(BF16)
    amat = amat_ref[0:n_sel, 0:n_cmp]
    imp = _dot_nt(amat, p_hi) + _dot_nt(amat, p_lo)

    t_tok = t0 + lax.broadcasted_iota(jnp.int32, (1, qb), 1)
    cur = (t_tok // SEL_BLOCK).astype(F32)
    blk = lax.broadcasted_iota(jnp.int32, (n_sel, 1), 0).astype(F32)
    forced = (blk == 0.0) | (blk == cur - 1.0)
    score = jnp.where(forced, jnp.inf, jnp.where(blk < cur, imp, NEG_INF))
    score = jnp.where(blk == cur, NEG_INF, score)
    slot = lax.broadcasted_iota(jnp.int32, (N_SELECT, qb), 0)
    sel = jnp.where(slot == 0, cur, 0.0)
    for n in range(1, N_SELECT):
        m = jnp.max(score, axis=0, keepdims=True)
        pick = jnp.min(jnp.where(score == m, blk, float(n_sel)), axis=0, keepdims=True)
        score = jnp.where(blk == pick, NEG_INF, score)
        sel = jnp.where(slot == n, pick, sel)
    idx_ref[sub] = sel.astype(jnp.int32)

    w_len = WINDOW + qb
    w_start = pl.multiple_of(jnp.maximum(t0 - WINDOW, 0), qb)
    s_w = _dot_nt(q, kw_ref[pl.ds(w_start, w_len), :])
    kpos = w_start + lax.broadcasted_iota(jnp.int32, (1, w_len), 1)
    mask_w = (kpos <= t_row) & (kpos > t_row - WINDOW)
    p_w = _softmax_rows(jnp.where(mask_w, s_w, NEG_INF))
    o_w = _dot(p_w.astype(BF16), vw_ref[pl.ds(w_start, w_len), :])

    gates = gates_ref[tok, :]
    outs = []
    for h in range(NSA_HPG):
        g_c = gates[:, 3 * h:3 * h + 1]
        g_w = gates[:, 3 * h + 2:3 * h + 3]
        outs.append(g_c * o_c[h * qb:(h + 1) * qb] + g_w * o_w[h * qb:(h + 1) * qb])
    ocw_ref[tok, :] = jnp.concatenate(outs, axis=1)


def _nsa_cmp_win(q, k_cmp, v_cmp, k_w, v_w, gates, amat, chunk=2 * LANES, tiles=4):
    b, _, s, hd = q.shape
    g = NSA_GROUPS
    n_cmp = k_cmp.shape[2]
    cmp_spec = pl.BlockSpec((None, None, n_cmp, hd), lambda bi, gi, j: (bi, gi, 0, 0))
    win_spec = pl.BlockSpec((None, None, s, hd), lambda bi, gi, j: (bi, gi, 0, 0))
    chunk = min(n_cmp, chunk)
    tiles = min(tiles, s // Q_BLOCK)
    step = tiles * Q_BLOCK
    assert n_cmp % chunk == 0 and s % step == 0 and (chunk * CMP_STRIDE) % step == 0
    return pl.pallas_call(
        functools.partial(_nsa_cw_kernel, chunk=chunk, tiles=tiles),
        name="nsa_cw",
        grid=(b, g, s // step),
        in_specs=[pl.BlockSpec((None, NSA_HPG, step, hd), lambda bi, gi, j: (bi, gi, j, 0)),
                  cmp_spec, cmp_spec, win_spec, win_spec,
                  pl.BlockSpec((None, step, LANES), lambda bi, gi, j: (bi, j, gi)),
                  pl.BlockSpec(amat.shape, lambda bi, gi, j: (0, 0))],
        out_specs=[pl.BlockSpec((None, step, NSA_HPG * hd), lambda bi, gi, j: (bi, j, gi)),
                   pl.BlockSpec((None, None, tiles, N_SELECT, Q_BLOCK), lambda bi, gi, j: (bi, gi, j, 0, 0))],
        out_shape=[jax.ShapeDtypeStruct((b, s, NSA_HEADS * hd), F32),
                   jax.ShapeDtypeStruct((b, g, s // Q_BLOCK, N_SELECT, Q_BLOCK), jnp.int32)],
        compiler_params=_params("parallel", "parallel", "arbitrary"),
    )(q, k_cmp, v_cmp, k_w, v_w, gates, amat)


def _roll_rows(x, shift):
    return jnp.concatenate([pltpu.roll(x[r:r + 8], shift, 0) for r in range(0, x.shape[0], 8)], axis=0)


def _nsa_sel_kernel(idx_ref, q_ref, kv2_ref, o_ref, s_a, s_b, v_a, v_b, *, tq, pairs):
    half_blk = SEL_BLOCK // 2
    n_col = N_SELECT * half_blk
    t0 = pl.program_id(2) * tq
    row = lax.broadcasted_iota(jnp.int32, (16, n_col), 0)
    col = lax.broadcasted_iota(jnp.int32, (16, n_col), 1)
    pos0 = 2 * col + (row % 8) // NSA_HPG
    in_slot0 = col < half_blk
    second = (row >= 8).astype(jnp.int32)
    row8 = lax.broadcasted_iota(jnp.int32, (8, LANES), 0)

    def stage_scores(grp, s_buf, v_buf):
        for u in range(pairs):
            tp = grp * pairs + u
            if u % 2 == 0:
                q_four = q_ref[pl.ds(pl.multiple_of(8 * tp, 16), 16), :].astype(F32)
            q8 = q_four[8 * (u % 2):8 * (u % 2) + 8]
            wide = jnp.concatenate([q8, jnp.zeros_like(q8)], axis=1)
            shifted = pltpu.roll(wide, NSA_HD, 1)
            q16 = jnp.concatenate(
                [jnp.where(row8 < NSA_HPG, wide, pltpu.roll(shifted, NSA_HPG, 0)),
                 jnp.where(row8 < NSA_HPG, pltpu.roll(wide, NSA_HPG, 0), shifted)], axis=0).astype(BF16)
            halves = []
            for w in range(2):
                tok = 2 * tp + w
                base = (tok // Q_BLOCK) * (N_SELECT * Q_BLOCK) + tok % Q_BLOCK
                blocks = []
                for n in range(N_SELECT):
                    off = pl.multiple_of(idx_ref[0, base + n * Q_BLOCK] * half_blk, half_blk)
                    blocks.append(kv2_ref[pl.ds(off, half_blk), :])
                keys = jnp.concatenate([blk[:, :LANES] for blk in blocks], axis=0)
                v_buf[(2 * u + w) * n_col:(2 * u + w + 1) * n_col, :] = jnp.concatenate(
                    [blk[:, LANES:] for blk in blocks], axis=0)
                halves.append(_dot_nt(q16, keys)[8 * w:8 * w + 8])
            s_buf[16 * u:16 * u + 16, :] = jnp.concatenate(halves, axis=0)

    def stage_output(grp, s_buf, v_buf):
        for u in range(pairs):
            tp = grp * pairs + u
            ta = t0 + 2 * tp
            n_valid = jnp.minimum(ta // SEL_BLOCK + 1, N_SELECT)
            ok = jnp.where(in_slot0, pos0 - (ta % SEL_BLOCK + second), col - n_valid * half_blk + 1) <= 0
            s = jnp.where(ok, s_buf[16 * u:16 * u + 16, :], NEG_INF)
            m = jnp.max(s, axis=-1, keepdims=True)
            m = jnp.maximum(m, _roll_rows(m, NSA_HPG))
            e = jnp.exp(s - m)
            l = jnp.sum(e, axis=-1, keepdims=True)
            l = l + _roll_rows(l, NSA_HPG)
            p = (e / l).astype(BF16)
            xa = _dot(p, v_buf[(2 * u) * n_col:(2 * u + 1) * n_col, :])
            xb = _dot(p, v_buf[(2 * u + 1) * n_col:(2 * u + 2) * n_col, :])
            x = jnp.concatenate([xa[:8], xb[8:]], axis=0)
            y = x + pltpu.roll(_roll_rows(x, NSA_HPG), NSA_HD, 1)
            out = jnp.where(row8 < NSA_HPG, y[:8], pltpu.roll(y[8:], NSA_HPG, 0))
            o_ref[pl.ds(pl.multiple_of(8 * tp, 8), 8), :] = out

    n_grp = tq // (2 * pairs)
    stage_scores(0, s_a, v_a)

    def double_step(d):
        stage_scores(2 * d + 1, s_b, v_b)
        stage_output(2 * d, s_a, v_a)
        stage_scores(2 * d + 2, s_a, v_a)
        stage_output(2 * d + 1, s_b, v_b)

    for d in range(n_grp // 2 - 1):
        double_step(d)
    stage_scores(n_grp - 1, s_b, v_b)
    stage_output(n_grp - 2, s_a, v_a)
    stage_output(n_grp - 1, s_b, v_b)


def _nsa_selected(idx, q_rows, kv2, tq=512, pairs=8):
    b, g = idx.shape[:2]
    s = kv2.shape[2] * 2
    tq = min(tq, s)
    nt = s // tq
    assert (tq // (2 * pairs)) % 2 == 0 and tq % Q_BLOCK == 0
    idx_flat = idx.reshape(b * g * nt, 1, tq * N_SELECT)
    n_col = N_SELECT * SEL_BLOCK // 2
    scratch = [pltpu.VMEM((pairs * 16, n_col), F32)] * 2 + [pltpu.VMEM((pairs * 2 * n_col, LANES), BF16)] * 2
    return pl.pallas_call(
        functools.partial(_nsa_sel_kernel, tq=tq, pairs=pairs),
        scratch_shapes=scratch,
        name="nsa_sel",
        grid=(b, g, nt),
        in_specs=[pl.BlockSpec((None, 1, tq * N_SELECT), lambda bi, gi, j: ((bi * g + gi) * nt + j, 0, 0),
                               memory_space=pltpu.SMEM),
                  pl.BlockSpec((None, None, tq * NSA_HPG, NSA_HD), lambda bi, gi, j: (bi, gi, j, 0)),
                  pl.BlockSpec((None, None, s // 2, 2 * LANES), lambda bi, gi, j: (bi, gi, 0, 0))],
        out_specs=pl.BlockSpec((None, None, tq * NSA_HPG, LANES), lambda bi, gi, j: (bi, gi, j, 0)),
        out_shape=jax.ShapeDtypeStruct((b, g, s * NSA_HPG, LANES), F32),
        compiler_params=_params("parallel", "parallel", "arbitrary"),
    )(idx_flat, q_rows, kv2)


def _nsa_out_kernel(x_ref, ocw_ref, os0_ref, os1_ref, os2_ref, os3_ref, gates_ref, emat_ref, w_ref, gain_ref,
                    o_ref):
    tm = x_ref.shape[0]
    left = lax.broadcasted_iota(jnp.int32, (tm, LANES), 1) < NSA_HD
    tiles = []
    for os_ref in (os0_ref, os1_ref, os2_ref, os3_ref):
        heads = [os_ref[pl.ds(h, tm, stride=NSA_HPG), :] for h in range(NSA_HPG)]
        for h in range(0, NSA_HPG, 2):
            tiles.append(jnp.where(left, heads[h], pltpu.roll(heads[h + 1], NSA_HD, 1)))
    o_s = jnp.concatenate(tiles, axis=1)
    g_s = _dot_split(gates_ref[...], emat_ref[...])
    a = (ocw_ref[...] + g_s * o_s).astype(BF16)
    o_ref[...] = x_ref[...] + _rms(_dot(a, w_ref[...]), gain_ref[...])


def _nsa_out_proj(x, o_cw, o_s, gates, emat, w, gain, tm=512):
    b, s, d = x.shape
    tm = min(tm, s)
    assert s % tm == 0
    row = pl.BlockSpec((None, tm, d), lambda bi, i: (bi, i, 0))
    sel = [pl.BlockSpec((None, None, tm * NSA_HPG, LANES), functools.partial(lambda bi, i, gi: (bi, gi, i, 0), gi=gi))
           for gi in range(NSA_GROUPS)]
    return pl.pallas_call(
        _nsa_out_kernel,
        name="nsa_out",
        grid=(b, s // tm),
        in_specs=[row, row, *sel,
                  pl.BlockSpec((None, tm, gates.shape[2]), lambda bi, i: (bi, i, 0)),
                  pl.BlockSpec(emat.shape, lambda bi, i: (0, 0)),
                  pl.BlockSpec(w.shape, lambda bi, i: (0, 0)),
                  pl.BlockSpec((1, d), lambda bi, i: (0, 0))],
        out_specs=row,
        out_shape=jax.ShapeDtypeStruct((b, s, d), F32),
        compiler_params=_params("parallel", "parallel"),
    )(x, o_cw, o_s, o_s, o_s, o_s, gates, emat, w, gain)


def _proj_out_kernel(x_ref, a_ref, w_ref, gain_ref, o_ref):
    o_ref[...] = x_ref[...] + _rms(_dot(a_ref[...], w_ref[...]), gain_ref[...])


def _proj_out(x, a, w, gain, tm=1024):
    t, d = x.shape
    tm = min(tm, t)
    assert t % tm == 0
    row = pl.BlockSpec((tm, d), lambda i: (i, 0))
    return pl.pallas_call(
        _proj_out_kernel,
        name="proj_out",
        grid=(t // tm,),
        in_specs=[row, pl.BlockSpec((tm, a.shape[1]), lambda i: (i, 0)),
                  pl.BlockSpec(w.shape, lambda i: (0, 0)),
                  pl.BlockSpec((1, d), lambda i: (0, 0))],
        out_specs=row,
        out_shape=jax.ShapeDtypeStruct((t, d), F32),
        compiler_params=_params("parallel"),
    )(x, a, w, gain)


def _nsa_mixer(x, gain_pre, gain_post, w_in, w_out, cmp_pos, cmp_w1, cmp_w2):
    b, s, d = x.shape
    g, hd, hpg = NSA_GROUPS, NSA_HD, NSA_HPG
    n_gate = 3 * hpg
    q_cols = NSA_HEADS * hd
    kv_end = q_cols + 6 * NSA_KV_COLS

    gate_w = w_in[:, kv_end:].reshape(d, g, n_gate)
    gate_w = jnp.pad(gate_w, ((0, 0), (0, 0), (0, LANES - n_gate))).reshape(d, g * LANES)
    w_all = jnp.concatenate([w_in[:, :kv_end], gate_w], axis=1).astype(BF16)

    half = hd // 2
    inv = ROPE_THETA ** (-jnp.arange(half, dtype=F32) / half)
    ang = jnp.arange(s).astype(F32)[:, None] * inv[None, :]
    cos, sin = jnp.cos(ang), jnp.sin(ang)
    cos_t = jnp.tile(cos, (1, LANES // half))
    sin_t = jnp.tile(jnp.concatenate([-sin, sin], axis=1), (1, LANES // hd))

    q, k_c, v_c, k_s, v_s, k_w, v_w, gates = _nsa_in_proj(x, gain_pre, w_all, cos_t, sin_t)
    k_cmp = _compress(k_c, cmp_pos[0], cmp_w1[0], cmp_w2[0])
    v_cmp = _compress(v_c, cmp_pos[1], cmp_w1[1], cmp_w2[1])

    n_cmp_rows = s // CMP_STRIDE
    n_sel = s // SEL_BLOCK
    ratio = SEL_BLOCK // CMP_STRIDE
    c_id = jnp.arange(n_cmp_rows)[:, None]
    n_id = jnp.arange(n_sel)[None, :]
    lo = ratio * n_id - (CMP_BLOCK // CMP_STRIDE - 1)
    amat = ((c_id >= lo) & (c_id <= ratio * n_id + ratio - 1) & (c_id < n_cmp_rows - 1)).astype(BF16)

    o_cw, idx = _nsa_cmp_win(q, k_cmp, v_cmp, k_w, v_w, gates, amat.T)

    q_rows = q.reshape(b, g, hpg, s, hd).transpose(0, 1, 3, 2, 4).reshape(b, g, s * hpg, hd)
    kv2 = jnp.concatenate([k_s.reshape(b, g, s // 2, 2 * hd), v_s.reshape(b, g, s // 2, 2 * hd)], axis=-1)
    o_s = _nsa_selected(idx, q_rows, kv2)

    r_id = jnp.arange(g * LANES)[:, None]
    col = jnp.arange(q_cols)[None, :]
    emat = (r_id == (col // (hpg * hd)) * LANES + 3 * ((col // hd) % hpg) + 1).astype(BF16)

    return _nsa_out_proj(x, o_cw, o_s, gates, emat, w_out.astype(BF16), gain_post)


def _hgrn_in_kernel(x_ref, gain_ref, w_ref, lb_ref, q_ref, k_ref, lf_ref, v_ref, gs_ref):
    xn = _rms(x_ref[...], gain_ref[...]).astype(BF16)
    n_chunk = D_MODEL // 256

    def chunk(c):
        return _dot(xn, w_ref[:, 256 * c:256 * (c + 1)])

    for c in range(n_chunk):
        cols = slice(256 * c, 256 * (c + 1))
        r = chunk(c)
        q_ref[:, cols] = r * jax.nn.sigmoid(r)
        f = chunk(n_chunk + c)
        lb = lb_ref[:, cols]
        sig = jax.nn.sigmoid(f)
        lf_ref[:, cols] = jnp.log(lb + (1.0 - lb) * sig)
        k_ref[:, cols] = (1.0 - lb) * (1.0 - sig)
        v_ref[:, cols] = chunk(2 * n_chunk + c)
        r = chunk(3 * n_chunk + c)
        gs_ref[:, cols] = r * jax.nn.sigmoid(r)


def _hgrn_in_proj(x, gain, w, lb, tm=512):
    t, d = x.shape
    tm = min(tm, t)
    assert t % tm == 0
    row = pl.BlockSpec((tm, d), lambda i: (i, 0))
    vec = pl.BlockSpec((1, d), lambda i: (0, 0))
    return pl.pallas_call(
        _hgrn_in_kernel,
        name="hgrn_in",
        grid=(t // tm,),
        in_specs=[row, vec, pl.BlockSpec(w.shape, lambda i: (0, 0)), vec],
        out_specs=[row] * 5,
        out_shape=[jax.ShapeDtypeStruct((t, d), F32)] * 5,
        compiler_params=_params("parallel"),
    )(x, gain, w, lb)


def _hgrn_scan_kernel(q_ref, k_ref, lf_ref, v_ref, gs_ref, gain_ref, o_ref, state_ref, *, rows, heads):
    sub = HGRN_SUB
    n_sub = rows // sub

    @pl.when(pl.program_id(2) == 0)
    def _():
        state_ref[...] = jnp.zeros_like(state_ref)

    r_id = lax.broadcasted_iota(jnp.int32, (rows, rows), 0)
    c_id = lax.broadcasted_iota(jnp.int32, (rows, rows), 1)
    tri = jnp.where(((r_id // sub) == (c_id // sub)) & (c_id <= r_id), 1.0, 0.0)
    row_id = lax.broadcasted_iota(jnp.int32, (sub, HGRN_DK), 0)

    for h in range(heads):
        lanes = slice(LANES * h, LANES * (h + 1))
        cum = jnp.dot(tri, lf_ref[:, lanes], preferred_element_type=F32, precision=lax.Precision.HIGHEST) * LOG2E
        q = q_ref[:, lanes]
        k = k_ref[:, lanes]
        v = v_ref[:, lanes]
        q_dec = (q * jnp.exp2(cum)).astype(BF16)

        intra, update, decay = [], [], []
        for i in range(n_sub):
            sl = slice(sub * i, sub * (i + 1))
            qi, ki, vi, ci = q[sl], k[sl], v[sl], cum[sl]
            total = ci[sub - 1:sub]
            o_i = None
            for j in range(sub):
                w = jnp.exp2(jnp.where(row_id >= j, ci - ci[j:j + 1], NEG_INF))
                att = jnp.sum(qi * w * ki[j:j + 1], axis=-1, keepdims=True)
                o_i = att * vi[j:j + 1] if o_i is None else o_i + att * vi[j:j + 1]
            intra.append(o_i)
            update.append(_dot_tn(vi.astype(BF16), (ki * jnp.exp2(total - ci)).astype(BF16)))
            decay.append(jnp.exp2(total))

        state = state_ref[h]
        outs = []
        for i in range(n_sub):
            outs.append(intra[i] + _dot_nt(q_dec[sub * i:sub * (i + 1)], state.astype(BF16)))
            state = state * decay[i] + update[i]
        state_ref[h] = state

        o = jnp.concatenate(outs, axis=0)
        o_ref[:, lanes] = (_rms(o, gain_ref[...]) * gs_ref[:, lanes]).astype(o_ref.dtype)


def _hgrn_scan(q, k, lf, v, gs, gain, rows=128, heads=HGRN_HEADS):
    b, s, d = q.shape
    rows = min(rows, s)
    blk = pl.BlockSpec((None, rows, heads * LANES), lambda bi, h, l: (bi, l, h))
    return pl.pallas_call(
        functools.partial(_hgrn_scan_kernel, rows=rows, heads=heads),
        name="hgrn_scan",
        grid=(b, d // (heads * LANES), s // rows),
        in_specs=[blk] * 5 + [pl.BlockSpec((1, LANES), lambda bi, h, l: (0, 0))],
        out_specs=blk,
        out_shape=jax.ShapeDtypeStruct((b, s, d), BF16),
        scratch_shapes=[pltpu.VMEM((heads, LANES, HGRN_DK), F32)],
        compiler_params=_params("parallel", "parallel", "arbitrary"),
    )(q, k, lf, v, gs, gain)


def _hgrn_mixer(x, gain_pre, gain_post, w_in, w_out, norm_gain, lower_bound):
    b, s, d = x.shape
    x2 = x.reshape(b * s, d)
    q, k, lf, v, gs = _hgrn_in_proj(x2, gain_pre, w_in.astype(BF16), lower_bound.reshape(1, d))
    shape = (b, s, d)
    o = _hgrn_scan(q.reshape(shape), k.reshape(shape), lf.reshape(shape), v.reshape(shape),
                   gs.reshape(shape), norm_gain.reshape(1, LANES))
    return _proj_out(x2, o.reshape(b * s, d), w_out.astype(BF16), gain_post).reshape(shape)


def kernel(x, p, norm_gains, ffn_w_in, ffn_w_out, ple_w_in, ple_w_gate, nsa_w_in, nsa_w_out, nsa_cmp_pos,
           nsa_cmp_w1, nsa_cmp_w2, hgrn_w_in, hgrn_w_out, hgrn_norm, hgrn_lb_logits):
    b, s, d = x.shape
    depth = norm_gains.shape[0]
    lb_sm = jax.nn.softmax(hgrn_lb_logits.astype(F32), axis=0)
    lower_bounds = jnp.cumsum(lb_sm, axis=0) - lb_sm[0]
    for layer in range(depth):
        ng = norm_gains[layer].reshape(8, 1, d)
        x2 = x.reshape(b * s, d)
        x2 = _ffn_block(x2, ng[0], ng[1], ffn_w_in[layer, 0].astype(BF16), ffn_w_out[layer, 0].astype(BF16))
        x = x2.reshape(b, s, d)
        j = layer // 2
        if layer % 2 == 0:
            x = _nsa_mixer(x, ng[2], ng[3], nsa_w_in[j], nsa_w_out[j], nsa_cmp_pos[j], nsa_cmp_w1[j],
                           nsa_cmp_w2[j])
        else:
            x = _hgrn_mixer(x, ng[2], ng[3], hgrn_w_in[j], hgrn_w_out[j], hgrn_norm[j], lower_bounds[layer])
        x2 = x.reshape(b * s, d)
        ple = (p[layer].reshape(b * s, -1), ng[6], ng[7], ple_w_gate[layer].astype(BF16),
               ple_w_in[layer].astype(BF16))
        x2 = _ffn_block(x2, ng[4], ng[5], ffn_w_in[layer, 1].astype(BF16), ffn_w_out[layer, 1].astype(BF16), ple)
        x = x2.reshape(b, s, d)
    return x
```

```python
import functools

import jax
import jax.numpy as jnp
from jax import lax
from jax.experimental import pallas as pl
from jax.experimental.pallas import tpu as pltpu

F32 = jnp.float32
BF16 = jnp.bfloat16
NEG_INF = float("-inf")

D_MODEL = 1024
NORM_EPS = 1e-6
ROPE_THETA = 10000.0

NSA_HEADS = 16
NSA_GROUPS = 4
NSA_HPG = NSA_HEADS // NSA_GROUPS
NSA_HD = 64
NSA_KV_COLS = NSA_GROUPS * NSA_HD
CMP_BLOCK = 32
CMP_STRIDE = 16
SEL_BLOCK = 64
N_SELECT = 16
WINDOW = 512
Q_BLOCK = 128

HGRN_HEADS = 8
HGRN_DK = 128
HGRN_SUB = 16
LOG2E = 1.4426950408889634

LANES = 128
VMEM_LIMIT = 56 * 1024 * 1024


def _params(*sem):
    return pltpu.CompilerParams(dimension_semantics=sem, vmem_limit_bytes=VMEM_LIMIT)


def _rms(x, gain):
    return x * lax.rsqrt(jnp.mean(x * x, axis=-1, keepdims=True) + NORM_EPS) * gain


def _dot(a, b):
    return jnp.dot(a, b, preferred_element_type=F32)


def _dot_nt(a, b):
    return lax.dot_general(a, b, (((1,), (1,)), ((), ())), preferred_element_type=F32)


def _dot_tn(a, b):
    return lax.dot_general(a, b, (((0,), (0,)), ((), ())), preferred_element_type=F32)


def _dot_split(x, w):
    hi = x.astype(BF16)
    lo = (x - hi.astype(F32)).astype(BF16)
    return _dot(hi, w) + _dot(lo, w)


def _softmax_rows(s):
    m = jnp.max(s, axis=-1, keepdims=True)
    m = jnp.where(m == NEG_INF, 0.0, m)
    e = jnp.exp(s - m)
    return e / jnp.maximum(jnp.sum(e, axis=-1, keepdims=True), 1e-30)


def _ffn_kernel(x_ref, g0_ref, g1_ref, wi_ref, wo_ref, *rest, with_ple):
    x = x_ref[...]
    dff = wo_ref.shape[0]
    gu = _dot(_rms(x, g0_ref[...]).astype(BF16), wi_ref[...])
    g = gu[:, :dff]
    h = (g * jax.nn.sigmoid(g) * gu[:, dff:]).astype(BF16)
    x = x + 0.5 * _rms(_dot(h, wo_ref[...]), g1_ref[...])
    if with_ple:
        p_ref, g2_ref, g3_ref, wg_ref, wp_ref, o_ref = rest
        gate = jax.nn.sigmoid(_dot(_rms(x, g2_ref[...]).astype(BF16), wg_ref[...]))
        e = _dot(p_ref[...].astype(BF16), wp_ref[...])
        x = x + _rms(e * gate, g3_ref[...])
    else:
        o_ref, = rest
    o_ref[...] = x


def _ffn_block(x, g_pre, g_post, w_in, w_out, ple=None, tm=512):
    t, d = x.shape
    tm = min(tm, t)
    assert t % tm == 0
    row = pl.BlockSpec((tm, d), lambda i: (i, 0))
    vec = pl.BlockSpec((1, d), lambda i: (0, 0))

    def resident(w):
        return pl.BlockSpec(w.shape, lambda i: (0, 0), pipeline_mode=pl.Buffered(1))

    args = [x, g_pre, g_post, w_in, w_out]
    specs = [row, vec, vec, resident(w_in), resident(w_out)]
    if ple is not None:
        p, g2, g3, w_gate, w_p = ple
        args += [p, g2, g3, w_gate, w_p]
        specs += [pl.BlockSpec((tm, p.shape[1]), lambda i: (i, 0)), vec, vec, resident(w_gate), resident(w_p)]
    return pl.pallas_call(
        functools.partial(_ffn_kernel, with_ple=ple is not None),
        name="ffn_ple" if ple is not None else "ffn",
        grid=(t // tm,),
        in_specs=specs,
        out_specs=row,
        out_shape=jax.ShapeDtypeStruct((t, d), F32),
        compiler_params=_params("parallel"),
    )(*args)


def _rope_tile(xb, cos_t, sin_t, first_half):
    fwd = pltpu.roll(xb, LANES - NSA_HD // 2, 1)
    bwd = pltpu.roll(xb, NSA_HD // 2, 1)
    return xb * cos_t + jnp.where(first_half, fwd, bwd) * sin_t


def _nsa_in_kernel(x_ref, gain_ref, w_ref, cos_ref, sin_ref,
                   q_ref, kc_ref, vc_ref, ks_ref, vs_ref, kw_ref, vw_ref, gates_ref):
    xn = _rms(x_ref[...], gain_ref[...]).astype(BF16)
    cos_t = cos_ref[...]
    sin_t = sin_ref[...]
    tm = cos_t.shape[0]
    lane = lax.broadcasted_iota(jnp.int32, (tm, LANES), 1)
    first_half = (lane % NSA_HD) < (NSA_HD // 2)

    def chunk(c):
        return _dot(xn, w_ref[:, 256 * c:256 * (c + 1)])

    def roped(r):
        return [_rope_tile(r[:, LANES * h:LANES * (h + 1)], cos_t, sin_t, first_half) for h in range(2)]

    def store_heads(ref, base, tiles, scale):
        for h, tile in enumerate(tiles):
            ref[base + 2 * h] = (tile[:, :NSA_HD] * scale).astype(ref.dtype)
            ref[base + 2 * h + 1] = (tile[:, NSA_HD:] * scale).astype(ref.dtype)

    for c in range(4):
        store_heads(q_ref, 4 * c, roped(chunk(c)), NSA_HD ** -0.5)
    kc = roped(chunk(4))
    kc_ref[...] = jnp.concatenate(kc, axis=1)
    vc_ref[...] = chunk(5)
    store_heads(ks_ref, 0, roped(chunk(6)), 1.0)
    r = chunk(7)
    store_heads(vs_ref, 0, [r[:, :LANES], r[:, LANES:]], 1.0)
    store_heads(kw_ref, 0, roped(chunk(8)), 1.0)
    r = chunk(9)
    store_heads(vw_ref, 0, [r[:, :LANES], r[:, LANES:]], 1.0)
    gates_ref[:, :256] = jax.nn.sigmoid(chunk(10))
    gates_ref[:, 256:] = jax.nn.sigmoid(chunk(11))


def _nsa_in_proj(x, gain, w, cos_t, sin_t, tm=512):
    b, s, d = x.shape
    tm = min(tm, s)
    assert s % tm == 0
    g, hd = NSA_GROUPS, NSA_HD
    kv_heads = pl.BlockSpec((None, g, tm, hd), lambda bi, i: (bi, 0, i, 0))
    kv_flat = pl.BlockSpec((None, tm, g * hd), lambda bi, i: (bi, i, 0))
    tab = pl.BlockSpec((tm, LANES), lambda bi, i: (i, 0))
    return pl.pallas_call(
        _nsa_in_kernel,
        name="nsa_in",
        grid=(b, s // tm),
        in_specs=[pl.BlockSpec((None, tm, d), lambda bi, i: (bi, i, 0)),
                  pl.BlockSpec((1, d), lambda bi, i: (0, 0)),
                  pl.BlockSpec(w.shape, lambda bi, i: (0, 0)),
                  tab, tab],
        out_specs=[pl.BlockSpec((None, NSA_HEADS, tm, hd), lambda bi, i: (bi, 0, i, 0)),
                   kv_flat, kv_flat, kv_heads, kv_heads, kv_heads, kv_heads,
                   pl.BlockSpec((None, tm, 4 * LANES), lambda bi, i: (bi, i, 0))],
        out_shape=[jax.ShapeDtypeStruct((b, NSA_HEADS, s, hd), BF16),
                   jax.ShapeDtypeStruct((b, s, g * hd), F32),
                   jax.ShapeDtypeStruct((b, s, g * hd), F32),
                   jax.ShapeDtypeStruct((b, g, s, hd), BF16),
                   jax.ShapeDtypeStruct((b, g, s, hd), BF16),
                   jax.ShapeDtypeStruct((b, g, s, hd), BF16),
                   jax.ShapeDtypeStruct((b, g, s, hd), BF16),
                   jax.ShapeDtypeStruct((b, s, 4 * LANES), F32)],
        compiler_params=_params("parallel", "parallel"),
    )(x, gain, w, cos_t, sin_t)


def _cmp_proj_kernel(x_ref, plo_ref, phi_ref, a_ref, b_ref, ya_ref, yb_ref):
    x = x_ref[...]
    ya_ref[...] = _dot((x + plo_ref[...]).astype(BF16), a_ref[...])
    yb_ref[...] = _dot((x + phi_ref[...]).astype(BF16), b_ref[...])


def _cmp_mix_kernel(ya_ref, yb_ref, w2_ref, o_ref):
    n = ya_ref.shape[0]
    z = ya_ref[...] + pltpu.roll(yb_ref[...], n - 1, 0)
    h = (z * jax.nn.sigmoid(z)).astype(BF16)
    out = _dot(h, w2_ref[...])
    for g in range(NSA_GROUPS):
        o_ref[g] = out[:, NSA_HD * g:NSA_HD * (g + 1)].astype(o_ref.dtype)


def _compress(kv, pos, w1, w2, tc=256):
    b, s, _ = kv.shape
    g, hd = NSA_GROUPS, NSA_HD
    n = s // CMP_STRIDE
    wide = CMP_STRIDE * g * hd
    tc = min(tc, n)
    assert n % tc == 0
    x = kv.reshape(b, n, wide)
    eye = jnp.eye(g, dtype=F32)
    w1r = w1.reshape(CMP_BLOCK, hd, hd)
    mat_a = jnp.einsum("pde,gh->pgdhe", w1r[:CMP_STRIDE], eye).reshape(wide, g * hd).astype(BF16)
    mat_b = jnp.einsum("pde,gh->pgdhe", w1r[CMP_STRIDE:], eye).reshape(wide, g * hd).astype(BF16)
    p_lo = jnp.broadcast_to(pos[:CMP_STRIDE, None, :], (CMP_STRIDE, g, hd)).reshape(1, wide)
    p_hi = jnp.broadcast_to(pos[CMP_STRIDE:, None, :], (CMP_STRIDE, g, hd)).reshape(1, wide)
    w2bd = jnp.einsum("de,gh->gdhe", w2, eye).reshape(g * hd, g * hd).astype(BF16)
    vec = pl.BlockSpec((1, wide), lambda bi, i: (0, 0))
    mat = pl.BlockSpec((wide, g * hd), lambda bi, i: (0, 0))
    yspec = pl.BlockSpec((None, tc, g * hd), lambda bi, i: (bi, i, 0))
    ya, yb = pl.pallas_call(
        _cmp_proj_kernel,
        name="cmp_proj",
        grid=(b, n // tc),
        in_specs=[pl.BlockSpec((None, tc, wide), lambda bi, i: (bi, i, 0)), vec, vec, mat, mat],
        out_specs=[yspec, yspec],
        out_shape=[jax.ShapeDtypeStruct((b, n, g * hd), F32)] * 2,
        compiler_params=_params("parallel", "parallel"),
    )(x, p_lo, p_hi, mat_a, mat_b)
    full = pl.BlockSpec((None, n, g * hd), lambda bi: (bi, 0, 0))
    return pl.pallas_call(
        _cmp_mix_kernel,
        name="cmp_mix",
        grid=(b,),
        in_specs=[full, full, pl.BlockSpec((g * hd, g * hd), lambda bi: (0, 0))],
        out_specs=pl.BlockSpec((None, g, n, hd), lambda bi: (bi, 0, 0, 0)),
        out_shape=jax.ShapeDtypeStruct((b, g, n, hd), BF16),
        compiler_params=_params("parallel"),
    )(ya, yb, w2bd)


def _nsa_cw_kernel(q_ref, kc_ref, vc_ref, kw_ref, vw_ref, gates_ref, amat_ref, ocw_ref, idx_ref, *, chunk, tiles):
    t_end = (pl.program_id(2) + 1) * tiles * Q_BLOCK
    n_chunk = kc_ref.shape[0] // chunk
    last_cmp = (t_end - CMP_BLOCK) // CMP_STRIDE
    need = jnp.minimum(last_cmp // chunk + 1, n_chunk)

    def variant(nc):
        for sub in range(tiles):
            _nsa_cw_tile(q_ref, kc_ref, vc_ref, kw_ref, vw_ref, gates_ref, amat_ref, ocw_ref, idx_ref, sub=sub,
                         tiles=tiles, n_cmp=nc * chunk, n_sel=nc * chunk * CMP_STRIDE // SEL_BLOCK)

    for nc in range(1, n_chunk + 1):
        pl.when(need == nc)(functools.partial(variant, nc))


def _nsa_cw_tile(q_ref, kc_ref, vc_ref, kw_ref, vw_ref, gates_ref, amat_ref, ocw_ref, idx_ref, *, sub, tiles, n_cmp,
                 n_sel):
    qb = Q_BLOCK
    rows = NSA_HPG * qb
    t0 = (pl.program_id(2) * tiles + sub) * qb
    tok = slice(sub * qb, (sub + 1) * qb)
    q = q_ref[:, tok, :].reshape(rows, NSA_HD)
    t_row = t0 + lax.broadcasted_iota(jnp.int32, (rows, 1), 0) % qb

    cmp_end = lax.broadcasted_iota(jnp.int32, (1, n_cmp), 1) * CMP_STRIDE + (CMP_BLOCK - 1)
    cmp_ok = cmp_end <= t_row[0:qb]
    o_heads = []
    p_grp = None
    for h in range(NSA_HPG):
        s = _dot_nt(q[h * qb:(h + 1) * qb], kc_ref[0:n_cmp, :])
        p = _softmax_rows(jnp.where(cmp_ok, s, NEG_INF))
        o_heads.append(_dot(p.astype(BF16), vc_ref[0:n_cmp, :]))
        p_grp = p if p_grp is None else p_grp + p
    o_c = jnp.concatenate(o_heads, axis=0)
    p_hi = p_grp.astype(BF16)
    p_lo = (p_grp - p_hi.astype(F32)).astype(BF16)
    amat = amat_ref[0:n_sel, 0:n_cmp]
    imp = _dot_nt(amat, p_hi) + _dot_nt(amat, p_lo)

    t_tok = t0 + lax.broadcasted_iota(jnp.int32, (1, qb), 1)
    cur = (t_tok // SEL_BLOCK).astype(F32)
    blk = lax.broadcasted_iota(jnp.int32, (n_sel, 1), 0).astype(F32)
    forced = (blk == 0.0) | (blk == cur - 1.0)
    score = jnp.where(forced, jnp.inf, jnp.where(blk < cur, imp, NEG_INF))
    score = jnp.where(blk == cur, NEG_INF, score)
    slot = lax.broadcasted_iota(jnp.int32, (N_SELECT, qb), 0)
    sel = jnp.where(slot == 0, cur, 0.0)
    for n in range(1, N_SELECT):
        m = jnp.max(score, axis=0, keepdims=True)
        pick = jnp.min(jnp.where(score == m, blk, float(n_sel)), axis=0, keepdims=True)
        score = jnp.where(blk == pick, NEG_INF, score)
        sel = jnp.where(slot == n, pick, sel)
    idx_ref[sub] = sel.astype(jnp.int32)

    w_len = WINDOW + qb
    w_start = pl.multiple_of(jnp.maximum(t0 - WINDOW, 0), qb)
    s_w = _dot_nt(q, kw_ref[pl.ds(w_start, w_len), :])
    kpos = w_start + lax.broadcasted_iota(jnp.int32, (1, w_len), 1)
    mask_w = (kpos <= t_row) & (kpos > t_row - WINDOW)
    p_w = _softmax_rows(jnp.where(mask_w, s_w, NEG_INF))
    o_w = _dot(p_w.astype(BF16), vw_ref[pl.ds(w_start, w_len), :])

    gates = gates_ref[tok, :]
    outs = []
    for h in range(NSA_HPG):
        g_c = gates[:, 3 * h:3 * h + 1]
        g_w = gates[:, 3 * h + 2:3 * h + 3]
        outs.append(g_c * o_c[h * qb:(h + 1) * qb] + g_w * o_w[h * qb:(h + 1) * qb])
    ocw_ref[tok, :] = jnp.concatenate(outs, axis=1)


def _nsa_cmp_win(q, k_cmp, v_cmp, k_w, v_w, gates, amat, chunk=2 * LANES, tiles=4):
    b, _, s, hd = q.shape
    g = NSA_GROUPS
    n_cmp = k_cmp.shape[2]
    cmp_spec = pl.BlockSpec((None, None, n_cmp, hd), lambda bi, gi, j: (bi, gi, 0, 0))
    win_spec = pl.BlockSpec((None, None, s, hd), lambda bi, gi, j: (bi, gi, 0, 0))
    chunk = min(n_cmp, chunk)
    tiles = min(tiles, s // Q_BLOCK)
    step = tiles * Q_BLOCK
    assert n_cmp % chunk == 0 and s % step == 0 and (chunk * CMP_STRIDE) % step == 0
    return pl.pallas_call(
        functools.partial(_nsa_cw_kernel, chunk=chunk, tiles=tiles),
        name="nsa_cw",
        grid=(b, g, s // step),
        in_specs=[pl.BlockSpec((None, NSA_HPG, step, hd), lambda bi, gi, j: (bi, gi, j, 0)),
                  cmp_spec, cmp_spec, win_spec, win_spec,
                  pl.BlockSpec((None, step, LANES), lambda bi, gi, j: (bi, j, gi)),
                  pl.BlockSpec(amat.shape, lambda bi, gi, j: (0, 0))],
        out_specs=[pl.BlockSpec((None, step, NSA_HPG * hd), lambda bi, gi, j: (bi, j, gi)),
                   pl.BlockSpec((None, None, tiles, N_SELECT, Q_BLOCK), lambda bi, gi, j: (bi, gi, j, 0, 0))],
        out_shape=[jax.ShapeDtypeStruct((b, s, NSA_HEADS * hd), F32),
                   jax.ShapeDtypeStruct((b, g, s // Q_BLOCK, N_SELECT, Q_BLOCK), jnp.int32)],
        compiler_params=_params("parallel", "parallel", "arbitrary"),
    )(q, k_cmp, v_cmp, k_w, v_w, gates, amat)


def _roll_rows(x, shift):
    return jnp.concatenate([pltpu.roll(x[r:r + 8], shift, 0) for r in range(0, x.shape[0], 8)], axis=0)


def _nsa_sel_kernel(idx_ref, q_ref, kv2_ref, o_ref, s_a, s_b, v_a, v_b, *, tq, pairs):
    half_blk = SEL_BLOCK // 2
    n_col = N_SELECT * half_blk
    t0 = pl.program_id(2) * tq
    row = lax.broadcasted_iota(jnp.int32, (16, n_col), 0)
    col = lax.broadcasted_iota(jnp.int32, (16, n_col), 1)
    pos0 = 2 * col + (row % 8) // NSA_HPG
    in_slot0 = col < half_blk
    second = (row >= 8).astype(jnp.int32)
    row8 = lax.broadcasted_iota(jnp.int32, (8, LANES), 0)

    def stage_scores(grp, s_buf, v_buf):
        for u in range(pairs):
            tp = grp * pairs + u
            if u % 2 == 0:
                q_four = q_ref[pl.ds(pl.multiple_of(8 * tp, 16), 16), :].astype(F32)
            q8 = q_four[8 * (u % 2):8 * (u % 2) + 8]
            wide = jnp.concatenate([q8, jnp.zeros_like(q8)], axis=1)
            shifted = pltpu.roll(wide, NSA_HD, 1)
            q16 = jnp.concatenate(
                [jnp.where(row8 < NSA_HPG, wide, pltpu.roll(shifted, NSA_HPG, 0)),
                 jnp.where(row8 < NSA_HPG, pltpu.roll(wide, NSA_HPG, 0), shifted)], axis=0).astype(BF16)
            halves = []
            for w in range(2):
                tok = 2 * tp + w
                base = (tok // Q_BLOCK) * (N_SELECT * Q_BLOCK) + tok % Q_BLOCK
                blocks = []
                for n in range(N_SELECT):
                    off = pl.multiple_of(idx_ref[0, base + n * Q_BLOCK] * half_blk, half_blk)
                    blocks.append(kv2_ref[pl.ds(off, half_blk), :])
                keys = jnp.concatenate([blk[:, :LANES] for blk in blocks], axis=0)
                v_buf[(2 * u + w) * n_col:(2 * u + w + 1) * n_col, :] = jnp.concatenate(
                    [blk[:, LANES:] for blk in blocks], axis=0)
                halves.append(_dot_nt(q16, keys)[8 * w:8 * w + 8])
            s_buf[16 * u:16 * u + 16, :] = jnp.concatenate(halves, axis=0)

    def stage_output(grp, s_buf, v_buf):
        for u in range(pairs):
            tp = grp * pairs + u
            ta = t0 + 2 * tp
            n_valid = jnp.minimum(ta // SEL_BLOCK + 1, N_SELECT)
            ok = jnp.where(in_slot0, pos0 - (ta % SEL_BLOCK + second), col - n_valid * half_blk + 1) <= 0
            s = jnp.where(ok, s_buf[16 * u:16 * u + 16, :], NEG_INF)
            m = jnp.max(s, axis=-1, keepdims=True)
            m = jnp.maximum(m, _roll_rows(m, NSA_HPG))
            e = jnp.exp(s - m)
            l = jnp.sum(e, axis=-1, keepdims=True)
            l = l + _roll_rows(l, NSA_HPG)
            p = (e / l).astype(BF16)
            xa = _dot(p, v_buf[(2 * u) * n_col:(2 * u + 1) * n_col, :])
            xb = _dot(p, v_buf[(2 * u + 1) * n_col:(2 * u + 2) * n_col, :])
            x = jnp.concatenate([xa[:8], xb[8:]], axis=0)
            y = x + pltpu.roll(_roll_rows(x, NSA_HPG), NSA_HD, 1)
            out = jnp.where(row8 < NSA_HPG, y[:8], pltpu.roll(y[8:], NSA_HPG, 0))
            o_ref[pl.ds(pl.multiple_of(8 * tp, 8), 8), :] = out

    n_grp = tq // (2 * pairs)
    stage_scores(0, s_a, v_a)

    def double_step(d):
        stage_scores(2 * d + 1, s_b, v_b)
        stage_output(2 * d, s_a, v_a)
        stage_scores(2 * d + 2, s_a, v_a)
        stage_output(2 * d + 1, s_b, v_b)

    for d in range(n_grp // 2 - 1):
        double_step(d)
    stage_scores(n_grp - 1, s_b, v_b)
    stage_output(n_grp - 2, s_a, v_a)
    stage_output(n_grp - 1, s_b, v_b)


def _nsa_selected(idx, q_rows, kv2, tq=512, pairs=8):
    b, g = idx.shape[:2]
    s = kv2.shape[2] * 2
    tq = min(tq, s)
    nt = s // tq
    assert (tq // (2 * pairs)) % 2 == 0 and tq % Q_BLOCK == 0
    idx_flat = idx.reshape(b * g * nt, 1, tq * N_SELECT)
    n_col = N_SELECT * SEL_BLOCK // 2
    scratch = [pltpu.VMEM((pairs * 16, n_col), F32)] * 2 + [pltpu.VMEM((pairs * 2 * n_col, LANES), BF16)] * 2
    return pl.pallas_call(
        functools.partial(_nsa_sel_kernel, tq=tq, pairs=pairs),
        scratch_shapes=scratch,
        name="nsa_sel",
        grid=(b, g, nt),
        in_specs=[pl.BlockSpec((None, 1, tq * N_SELECT), lambda bi, gi, j: ((bi * g + gi) * nt + j, 0, 0),
                               memory_space=pltpu.SMEM),
                  pl.BlockSpec((None, None, tq * NSA_HPG, NSA_HD), lambda bi, gi, j: (bi, gi, j, 0)),
                  pl.BlockSpec((None, None, s // 2, 2 * LANES), lambda bi, gi, j: (bi, gi, 0, 0))],
        out_specs=pl.BlockSpec((None, None, tq * NSA_HPG, LANES), lambda bi, gi, j: (bi, gi, j, 0)),
        out_shape=jax.ShapeDtypeStruct((b, g, s * NSA_HPG, LANES), F32),
        compiler_params=_params("parallel", "parallel", "arbitrary"),
    )(idx_flat, q_rows, kv2)


def _nsa_out_kernel(x_ref, ocw_ref, os0_ref, os1_ref, os2_ref, os3_ref, gates_ref, emat_ref, w_ref, gain_ref,
                    o_ref):
    tm = x_ref.shape[0]
    left = lax.broadcasted_iota(jnp.int32, (tm, LANES), 1) < NSA_HD
    tiles = []
    for os_ref in (os0_ref, os1_ref, os2_ref, os3_ref):
        heads = [os_ref[pl.ds(h, tm, stride=NSA_HPG), :] for h in range(NSA_HPG)]
        for h in range(0, NSA_HPG, 2):
            tiles.append(jnp.where(left, heads[h], pltpu.roll(heads[h + 1], NSA_HD, 1)))
    o_s = jnp.concatenate(tiles, axis=1)
    g_s = _dot_split(gates_ref[...], emat_ref[...])
    a = (ocw_ref[...] + g_s * o_s).astype(BF16)
    o_ref[...] = x_ref[...] + _rms(_dot(a, w_ref[...]), gain_ref[...])


def _nsa_out_proj(x, o_cw, o_s, gates, emat, w, gain, tm=512):
    b, s, d = x.shape
    tm = min(tm, s)
    assert s % tm == 0
    row = pl.BlockSpec((None, tm, d), lambda bi, i: (bi, i, 0))
    sel = [pl.BlockSpec((None, None, tm * NSA_HPG, LANES), functools.partial(lambda bi, i, gi: (bi, gi, i, 0), gi=gi))
           for gi in range(NSA_GROUPS)]
    return pl.pallas_call(
        _nsa_out_kernel,
        name="nsa_out",
        grid=(b, s // tm),
        in_specs=[row, row, *sel,
                  pl.BlockSpec((None, tm, gates.shape[2]), lambda bi, i: (bi, i, 0)),
                  pl.BlockSpec(emat.shape, lambda bi, i: (0, 0)),
                  pl.BlockSpec(w.shape, lambda bi, i: (0, 0)),
                  pl.BlockSpec((1, d), lambda bi, i: (0, 0))],
        out_specs=row,
        out_shape=jax.ShapeDtypeStruct((b, s, d), F32),
        compiler_params=_params("parallel", "parallel"),
    )(x, o_cw, o_s, o_s, o_s, o_s, gates, emat, w, gain)


def _proj_out_kernel(x_ref, a_ref, w_ref, gain_ref, o_ref):
    o_ref[...] = x_ref[...] + _rms(_dot(a_ref[...], w_ref[...]), gain_ref[...])


def _proj_out(x, a, w, gain, tm=1024):
    t, d = x.shape
    tm = min(tm, t)
    assert t % tm == 0
    row = pl.BlockSpec((tm, d), lambda i: (i, 0))
    return pl.pallas_call(
        _proj_out_kernel,
        name="proj_out",
        grid=(t // tm,),
        in_specs=[row, pl.BlockSpec((tm, a.shape[1]), lambda i: (i, 0)),
                  pl.BlockSpec(w.shape, lambda i: (0, 0)),
                  pl.BlockSpec((1, d), lambda i: (0, 0))],
        out_specs=row,
        out_shape=jax.ShapeDtypeStruct((t, d), F32),
        compiler_params=_params("parallel"),
    )(x, a, w, gain)


def _nsa_mixer(x, gain_pre, gain_post, w_in, w_out, cmp_pos, cmp_w1, cmp_w2):
    b, s, d = x.shape
    g, hd, hpg = NSA_GROUPS, NSA_HD, NSA_HPG
    n_gate = 3 * hpg
    q_cols = NSA_HEADS * hd
    kv_end = q_cols + 6 * NSA_KV_COLS

    gate_w = w_in[:, kv_end:].reshape(d, g, n_gate)
    gate_w = jnp.pad(gate_w, ((0, 0), (0, 0), (0, LANES - n_gate))).reshape(d, g * LANES)
    w_all = jnp.concatenate([w_in[:, :kv_end], gate_w], axis=1).astype(BF16)

    half = hd // 2
    inv = ROPE_THETA ** (-jnp.arange(half, dtype=F32) / half)
    ang = jnp.arange(s).astype(F32)[:, None] * inv[None, :]
    cos, sin = jnp.cos(ang), jnp.sin(ang)
    cos_t = jnp.tile(cos, (1, LANES // half))
    sin_t = jnp.tile(jnp.concatenate([-sin, sin], axis=1), (1, LANES // hd))

    q, k_c, v_c, k_s, v_s, k_w, v_w, gates = _nsa_in_proj(x, gain_pre, w_all, cos_t, sin_t)
    k_cmp = _compress(k_c, cmp_pos[0], cmp_w1[0], cmp_w2[0])
    v_cmp = _compress(v_c, cmp_pos[1], cmp_w1[1], cmp_w2[1])

    n_cmp_rows = s // CMP_STRIDE
    n_sel = s // SEL_BLOCK
    ratio = SEL_BLOCK // CMP_STRIDE
    c_id = jnp.arange(n_cmp_rows)[:, None]
    n_id = jnp.arange(n_sel)[None, :]
    lo = ratio * n_id - (CMP_BLOCK // CMP_STRIDE - 1)
    amat = ((c_id >= lo) & (c_id <= ratio * n_id + ratio - 1) & (c_id < n_cmp_rows - 1)).astype(BF16)

    o_cw, idx = _nsa_cmp_win(q, k_cmp, v_cmp, k_w, v_w, gates, amat.T)

    q_rows = q.reshape(b, g, hpg, s, hd).transpose(0, 1, 3, 2, 4).reshape(b, g, s * hpg, hd)
    kv2 = jnp.concatenate([k_s.reshape(b, g, s // 2, 2 * hd), v_s.reshape(b, g, s // 2, 2 * hd)], axis=-1)
    o_s = _nsa_selected(idx, q_rows, kv2)

    r_id = jnp.arange(g * LANES)[:, None]
    col = jnp.arange(q_cols)[None, :]
    emat = (r_id == (col // (hpg * hd)) * LANES + 3 * ((col // hd) % hpg) + 1).astype(BF16)

    return _nsa_out_proj(x, o_cw, o_s, gates, emat, w_out.astype(BF16), gain_post)


def _hgrn_in_kernel(x_ref, gain_ref, w_ref, lb_ref, q_ref, k_ref, lf_ref, v_ref, gs_ref):
    xn = _rms(x_ref[...], gain_ref[...]).astype(BF16)
    n_chunk = D_MODEL // 256

    def chunk(c):
        return _dot(xn, w_ref[:, 256 * c:256 * (c + 1)])

    for c in range(n_chunk):
        cols = slice(256 * c, 256 * (c + 1))
        r = chunk(c)
        q_ref[:, cols] = r * jax.nn.sigmoid(r)
        f = chunk(n_chunk + c)
        lb = lb_ref[:, cols]
        sig = jax.nn.sigmoid(f)
        lf_ref[:, cols] = jnp.log(lb + (1.0 - lb) * sig)
        k_ref[:, cols] = (1.0 - lb) * (1.0 - sig)
        v_ref[:, cols] = chunk(2 * n_chunk + c)
        r = chunk(3 * n_chunk + c)
        gs_ref[:, cols] = r * jax.nn.sigmoid(r)


def _hgrn_in_proj(x, gain, w, lb, tm=512):
    t, d = x.shape
    tm = min(tm, t)
    assert t % tm == 0
    row = pl.BlockSpec((tm, d), lambda i: (i, 0))
    vec = pl.BlockSpec((1, d), lambda i: (0, 0))
    return pl.pallas_call(
        _hgrn_in_kernel,
        name="hgrn_in",
        grid=(t // tm,),
        in_specs=[row, vec, pl.BlockSpec(w.shape, lambda i: (0, 0)), vec],
        out_specs=[row] * 5,
        out_shape=[jax.ShapeDtypeStruct((t, d), F32)] * 5,
        compiler_params=_params("parallel"),
    )(x, gain, w, lb)


def _hgrn_scan_kernel(q_ref, k_ref, lf_ref, v_ref, gs_ref, gain_ref, o_ref, state_ref, *, rows, heads):
    sub = HGRN_SUB
    n_sub = rows // sub

    @pl.when(pl.program_id(2) == 0)
    def _():
        state_ref[...] = jnp.zeros_like(state_ref)

    r_id = lax.broadcasted_iota(jnp.int32, (rows, rows), 0)
    c_id = lax.broadcasted_iota(jnp.int32, (rows, rows), 1)
    tri = jnp.where(((r_id // sub) == (c_id // sub)) & (c_id <= r_id), 1.0, 0.0)
    row_id = lax.broadcasted_iota(jnp.int32, (sub, HGRN_DK), 0)

    for h in range(heads):
        lanes = slice(LANES * h, LANES * (h + 1))
        cum = jnp.dot(tri, lf_ref[:, lanes], preferred_element_type=F32, precision=lax.Precision.HIGHEST) * LOG2E
        q = q_ref[:, lanes]
        k = k_ref[:, lanes]
        v = v_ref[:, lanes]
        q_dec = (q * jnp.exp2(cum)).astype(BF16)

        intra, update, decay = [], [], []
        for i in range(n_sub):
            sl = slice(sub * i, sub * (i + 1))
            qi, ki, vi, ci = q[sl], k[sl], v[sl], cum[sl]
            total = ci[sub - 1:sub]
            o_i = None
            for j in range(sub):
                w = jnp.exp2(jnp.where(row_id >= j, ci - ci[j:j + 1], NEG_INF))
                att = jnp.sum(qi * w * ki[j:j + 1], axis=-1, keepdims=True)
                o_i = att * vi[j:j + 1] if o_i is None else o_i + att * vi[j:j + 1]
            intra.append(o_i)
            update.append(_dot_tn(vi.astype(BF16), (ki * jnp.exp2(total - ci)).astype(BF16)))
            decay.append(jnp.exp2(total))

        state = state_ref[h]
        outs = []
        for i in range(n_sub):
            outs.append(intra[i] + _dot_nt(q_dec[sub * i:sub * (i + 1)], state.astype(BF16)))
            state = state * decay[i] + update[i]
        state_ref[h] = state

        o = jnp.concatenate(outs, axis=0)
        o_ref[:, lanes] = (_rms(o, gain_ref[...]) * gs_ref[:, lanes]).astype(o_ref.dtype)


def _hgrn_scan(q, k, lf, v, gs, gain, rows=128, heads=HGRN_HEADS):
    b, s, d = q.shape
    rows = min(rows, s)
    blk = pl.BlockSpec((None, rows, heads * LANES), lambda bi, h, l: (bi, l, h))
    return pl.pallas_call(
        functools.partial(_hgrn_scan_kernel, rows=rows, heads=heads),
        name="hgrn_scan",
        grid=(b, d // (heads * LANES), s // rows),
        in_specs=[blk] * 5 + [pl.BlockSpec((1, LANES), lambda bi, h, l: (0, 0))],
        out_specs=blk,
        out_shape=jax.ShapeDtypeStruct((b, s, d), BF16),
        scratch_shapes=[pltpu.VMEM((heads, LANES, HGRN_DK), F32)],
        compiler_params=_params("parallel", "parallel", "arbitrary"),
    )(q, k, lf, v, gs, gain)


def _hgrn_mixer(x, gain_pre, gain_post, w_in, w_out, norm_gain, lower_bound):
    b, s, d = x.shape
    x2 = x.reshape(b * s, d)
    q, k, lf, v, gs = _hgrn_in_proj(x2, gain_pre, w_in.astype(BF16), lower_bound.reshape(1, d))
    shape = (b, s, d)
    o = _hgrn_scan(q.reshape(shape), k.reshape(shape), lf.reshape(shape), v.reshape(shape),
                   gs.reshape(shape), norm_gain.reshape(1, LANES))
    return _proj_out(x2, o.reshape(b * s, d), w_out.astype(BF16), gain_post).reshape(shape)


def kernel(x, p, norm_gains, ffn_w_in, ffn_w_out, ple_w_in, ple_w_gate, nsa_w_in, nsa_w_out, nsa_cmp_pos,
           nsa_cmp_w1, nsa_cmp_w2, hgrn_w_in, hgrn_w_out, hgrn_norm, hgrn_lb_logits):
    b, s, d = x.shape
    depth = norm_gains.shape[0]
    lb_sm = jax.nn.softmax(hgrn_lb_logits.astype(F32), axis=0)
    lower_bounds = jnp.cumsum(lb_sm, axis=0) - lb_sm[0]
    for layer in range(depth):
        ng = norm_gains[layer].reshape(8, 1, d)
        x2 = x.reshape(b * s, d)
        x2 = _ffn_block(x2, ng[0], ng[1], ffn_w_in[layer, 0].astype(BF16), ffn_w_out[layer, 0].astype(BF16))
        x = x2.reshape(b, s, d)
        j = layer // 2
        if layer % 2 == 0:
            x = _nsa_mixer(x, ng[2], ng[3], nsa_w_in[j], nsa_w_out[j], nsa_cmp_pos[j], nsa_cmp_w1[j],
                           nsa_cmp_w2[j])
        else:
            x = _hgrn_mixer(x, ng[2], ng[3], hgrn_w_in[j], hgrn_w_out[j], hgrn_norm[j], lower_bounds[layer])
        x2 = x.reshape(b * s, d)
        ple = (p[layer].reshape(b * s, -1), ng[6], ng[7], ple_w_gate[layer].astype(BF16),
               ple_w_in[layer].astype(BF16))
        x2 = _ffn_block(x2, ng[4], ng[5], ffn_w_in[layer, 1].astype(BF16), ffn_w_out[layer, 1].astype(BF16), ple)
        x = x2.reshape(b, s, d)
    return x
```
